```python
import jax, jax.numpy as jnp
from jax import lax
import numpy as np

D_MODEL = 2048
BATCH = 4
SEQ = 4096
DEPTH = 4

N_META = 16
BLOCK = 128
WINDOW = 128
HEAD_DIM = 64
SWA_HEADS = 16
SWA_KV_HEADS = 4
SWA_GROUP = SWA_HEADS // SWA_KV_HEADS
SB_HEADS = 16
SWA_WIDTH = SWA_HEADS * HEAD_DIM
SWA_KV_WIDTH = SWA_KV_HEADS * HEAD_DIM
SB_WIDTH = SB_HEADS * HEAD_DIM
MIX_WIDTH = SWA_WIDTH + SB_WIDTH
PROJ_WIDTH = SWA_WIDTH + 2 * SWA_KV_WIDTH + 3 * SB_WIDTH
D_FF = ((8 * D_MODEL + 3 * 256 - 1) // (3 * 256)) * 256
EPS = 1e-6
NEG = -1e30

kernel_name = "hybrid_swa_sink_stickbreaking_swiglu"


def rmsnorm(x, g):
    xf = x.astype(jnp.float32)
    y = xf * lax.rsqrt(jnp.mean(xf * xf, axis=-1, keepdims=True) + EPS)
    return (y * g.astype(jnp.float32)).astype(x.dtype)


def alibi_slopes(n_heads):
    h = jnp.arange(1, n_heads + 1, dtype=jnp.float32)
    return jnp.exp2(-8.0 * h / n_heads)


def swa_attention(q, k, v, sinks):
    b, L, _, d = q.shape
    nb = (L - N_META) // BLOCK
    f32 = jnp.float32
    scale = d ** -0.5
    slopes = alibi_slopes(SWA_HEADS).reshape(SWA_KV_HEADS, SWA_GROUP)[:, :, None, None]
    sink = sinks.astype(f32).reshape(SWA_KV_HEADS, SWA_GROUP)

    qg = q.reshape(b, L, SWA_KV_HEADS, SWA_GROUP, d)
    qm, km, vm = qg[:, :N_META], k[:, :N_META], v[:, :N_META]
    qr = qg[:, N_META:].reshape(b, nb, BLOCK, SWA_KV_HEADS, SWA_GROUP, d)
    kr = k[:, N_META:].reshape(b, nb, BLOCK, SWA_KV_HEADS, d)
    vr = v[:, N_META:].reshape(b, nb, BLOCK, SWA_KV_HEADS, d)
    pad = jnp.zeros_like(kr[:, :1])
    kb = jnp.concatenate([jnp.concatenate([pad, kr[:, :-1]], axis=1), kr], axis=2)
    vb = jnp.concatenate([jnp.concatenate([pad, vr[:, :-1]], axis=1), vr], axis=2)

    blk = jnp.arange(nb)[:, None, None]
    qi = jnp.arange(BLOCK)[None, :, None]
    kj = jnp.arange(2 * BLOCK)[None, None, :]
    delta = qi - kj + BLOCK
    key_real = blk * BLOCK + kj - BLOCK
    band_ok = (delta >= 0) & (delta < WINDOW) & (key_real >= 0)

    s_band = jnp.einsum('bnqhgd,bnkhd->bnhgqk', qr, kb).astype(f32) * scale
    s_band = s_band - slopes * delta.astype(f32)
    s_band = jnp.where(band_ok[None, :, None, None], s_band, NEG)

    t_pos = N_META + blk * BLOCK + qi
    delta_m = (t_pos - jnp.arange(N_META)[None, None, :]).astype(f32)
    s_meta = jnp.einsum('bnqhgd,bmhd->bnhgqm', qr, km).astype(f32) * scale
    s_meta = s_meta - slopes * delta_m[:, None, None]

    sink_col = jnp.broadcast_to(sink[None, None, :, :, None, None], s_band.shape[:-1] + (1,))
    p = jax.nn.softmax(jnp.concatenate([s_band, s_meta, sink_col], axis=-1), axis=-1)
    p_band = p[..., :2 * BLOCK].astype(v.dtype)
    p_meta = p[..., 2 * BLOCK:2 * BLOCK + N_META].astype(v.dtype)
    out_r = (jnp.einsum('bnhgqk,bnkhd->bnqhgd', p_band, vb)
             + jnp.einsum('bnhgqm,bmhd->bnqhgd', p_meta, vm))
    out_r = out_r.reshape(b, nb * BLOCK, SWA_WIDTH)

    mpos = jnp.arange(N_META)
    delta_mm = (mpos[:, None] - mpos[None, :])
    s_mm = jnp.einsum('bqhgd,bkhd->bhgqk', qm, km).astype(f32) * scale
    s_mm = s_mm - slopes * delta_mm.astype(f32)
    s_mm = jnp.where((delta_mm >= 0) & (delta_mm < WINDOW), s_mm, NEG)
    sink_mm = jnp.broadcast_to(sink[None, :, :, None, None], s_mm.shape[:-1] + (1,))
    p_mm = jax.nn.softmax(jnp.concatenate([s_mm, sink_mm], axis=-1), axis=-1)[..., :N_META]
    out_m = jnp.einsum('bhgqk,bkhd->bqhgd', p_mm.astype(v.dtype), vm).reshape(b, N_META, SWA_WIDTH)
    return jnp.concatenate([out_m, out_r], axis=1)


def sb_block(qb, qpos, k, v, kpos):
    d = qb.shape[-1]
    z = jnp.einsum('bqhd,bkhd->bhqk', qb, k).astype(jnp.float32) * (d ** -0.5)
    causal = kpos[None, :] < qpos[:, None]
    log1m = jnp.where(causal, -jax.nn.softplus(z), 0.0)
    after = lax.cumsum(log1m, axis=3, reverse=True) - log1m
    a = jnp.where(causal, jnp.exp(jax.nn.log_sigmoid(z) + after), 0.0)
    return jnp.einsum('bhqk,bkhd->bqhd', a.astype(v.dtype), v)


def stick_breaking_attention(q, k, v):
    b, L, h, d = q.shape
    nb = (L - N_META) // BLOCK
    kpos = jnp.arange(L)
    qr = q[:, N_META:].reshape(b, nb, BLOCK, h, d).swapaxes(0, 1)
    qpos = N_META + jnp.arange(nb * BLOCK).reshape(nb, BLOCK)
    out_r = lax.map(lambda xs: sb_block(xs[0], xs[1], k, v, kpos), (qr, qpos))
    out_r = out_r.swapaxes(0, 1).reshape(b, nb * BLOCK, h * d)
    mpos = jnp.arange(N_META)
    out_m = sb_block(q[:, :N_META], mpos, k[:, :N_META], v[:, :N_META], mpos).reshape(b, N_META, h * d)
    return jnp.concatenate([out_m, out_r], axis=1)


def setup_inputs(seed: int = 0) -> dict:
    key = jax.random.key(seed)
    ks = jax.random.split(key, 16)
    f32 = jnp.float32
    nrm = lambda k, shape, scale: jax.random.normal(k, shape, f32) * scale
    gain = lambda k, shape: 1.0 + 0.05 * jax.random.normal(k, shape, f32)
    return {
        "x": nrm(ks[0], (BATCH, SEQ, D_MODEL), 1.0),
        "meta_tokens": nrm(ks[1], (N_META, D_MODEL), 1.0),
        "attn_norm_g": gain(ks[2], (DEPTH, D_MODEL)),
        "w_in": nrm(ks[3], (DEPTH, D_MODEL, PROJ_WIDTH), D_MODEL ** -0.5),
        "q_norm_g": gain(ks[4], (DEPTH, HEAD_DIM)),
        "k_norm_g": gain(ks[5], (DEPTH, HEAD_DIM)),
        "attn_sinks": nrm(ks[6], (DEPTH, SWA_HEADS), 0.5),
        "swa_out_g": gain(ks[7], (DEPTH, SWA_WIDTH)),
        "sb_out_g": gain(ks[8], (DEPTH, SB_WIDTH)),
        "w_o": nrm(ks[9], (DEPTH, MIX_WIDTH, D_MODEL), MIX_WIDTH ** -0.5),
        "ffn_norm_g": gain(ks[10], (DEPTH, D_MODEL)),
        "w_gate": nrm(ks[11], (DEPTH, D_MODEL, D_FF), D_MODEL ** -0.5),
        "w_up": nrm(ks[12], (DEPTH, D_MODEL, D_FF), D_MODEL ** -0.5),
        "w_down": nrm(ks[13], (DEPTH, D_FF, D_MODEL), D_FF ** -0.5),
    }


def reference(x, meta_tokens, attn_norm_g, w_in, q_norm_g, k_norm_g, attn_sinks,
              swa_out_g, sb_out_g, w_o, ffn_norm_g, w_gate, w_up, w_down):
    b = x.shape[0]
    meta = jnp.broadcast_to(meta_tokens[None].astype(x.dtype), (b, N_META, D_MODEL))
    h = jnp.concatenate([meta, x], axis=1)
    L = h.shape[1]
    split_at = np.cumsum([SWA_WIDTH, SWA_KV_WIDTH, SWA_KV_WIDTH, SB_WIDTH, SB_WIDTH]).tolist()
    for l in range(DEPTH):
        hn = rmsnorm(h, attn_norm_g[l])
        proj = jnp.einsum('bld,dp->blp', hn, w_in[l])
        qa, ka, va, qb, kb, vb = jnp.split(proj, split_at, axis=-1)
        qa = rmsnorm(qa.reshape(b, L, SWA_HEADS, HEAD_DIM), q_norm_g[l])
        ka = rmsnorm(ka.reshape(b, L, SWA_KV_HEADS, HEAD_DIM), k_norm_g[l])
        va = va.reshape(b, L, SWA_KV_HEADS, HEAD_DIM)
        out_a = swa_attention(qa, ka, va, attn_sinks[l])
        out_b = stick_breaking_attention(qb.reshape(b, L, SB_HEADS, HEAD_DIM),
                                         kb.reshape(b, L, SB_HEADS, HEAD_DIM),
                                         vb.reshape(b, L, SB_HEADS, HEAD_DIM))
        mixed = jnp.concatenate([rmsnorm(out_a, swa_out_g[l]), rmsnorm(out_b, sb_out_g[l])], axis=-1)
        h = h + jnp.einsum('blm,md->bld', mixed, w_o[l])
        hn = rmsnorm(h, ffn_norm_g[l])
        g = jnp.einsum('bld,df->blf', hn, w_gate[l])
        u = jnp.einsum('bld,df->blf', hn, w_up[l])
        h = h + jnp.einsum('blf,fd->bld', jax.nn.silu(g) * u, w_down[l])
    return h[:, N_META:]
```

```python
import functools

import jax
import jax.numpy as jnp
from jax import lax
from jax.experimental import pallas as pl
from jax.experimental.pallas import tpu as pltpu

F32 = jnp.float32
BF16 = jnp.bfloat16

N_META = 16
BLOCK = 128
HEAD_DIM = 64
LANES = 128
SWA_HEADS = 16
SWA_KV_HEADS = 4
SB_HEADS = 16
EPS = 1e-6
NEG = -1e30
SCALE = HEAD_DIM ** -0.5
VMEM_LIMIT = 56 * 1024 * 1024


def _cparams(sem):
    return pltpu.CompilerParams(dimension_semantics=sem, vmem_limit_bytes=VMEM_LIMIT)


def _dot(a, b):
    return jnp.dot(a, b, preferred_element_type=F32)


def _dot_nt(a, b):
    return lax.dot_general(a, b, (((1,), (1,)), ((), ())), preferred_element_type=F32)


def _split_bf16(x):
    hi = x.astype(BF16)
    lo = (x - hi.astype(F32)).astype(BF16)
    return hi, lo


def _iota(shape, dim):
    return lax.broadcasted_iota(jnp.int32, shape, dim)


def _norm_mm_kernel(x_ref, g_ref, w_ref, o_ref, xn_ref):
    @pl.when(pl.program_id(1) == 0)
    def _():
        x = x_ref[...]
        ms = jnp.mean(x * x, axis=-1, keepdims=True)
        xn_ref[...] = ((x * lax.rsqrt(ms + EPS)) * g_ref[...]).astype(BF16)

    o_ref[...] = _dot(xn_ref[...], w_ref[...])


def _norm_matmul(x, g, w, tm, tn):
    m, d = x.shape
    n = w.shape[1]
    return pl.pallas_call(
        _norm_mm_kernel,
        grid=(m // tm, n // tn),
        in_specs=[
            pl.BlockSpec((tm, d), lambda i, j: (i, 0)),
            pl.BlockSpec((1, d), lambda i, j: (0, 0)),
            pl.BlockSpec((d, tn), lambda i, j: (0, j)),
        ],
        out_specs=pl.BlockSpec((tm, tn), lambda i, j: (i, j)),
        out_shape=jax.ShapeDtypeStruct((m, n), F32),
        scratch_shapes=[pltpu.VMEM((tm, d), BF16)],
        compiler_params=_cparams(("arbitrary", "arbitrary")),
        name="norm_inproj",
    )(x, g, w)


def _outproj_kernel(a_ref, b_ref, ga_ref, gb_ref, wa_ref, wb_ref, h_ref, o_ref, an_ref, bn_ref):
    @pl.when(pl.program_id(1) == 0)
    def _():
        for src, g, dst in ((a_ref, ga_ref, an_ref), (b_ref, gb_ref, bn_ref)):
            x = src[...]
            ms = jnp.mean(x * x, axis=-1, keepdims=True)
            dst[...] = ((x * lax.rsqrt(ms + EPS)) * g[...]).astype(BF16)

    o_ref[...] = h_ref[...] + (_dot(an_ref[...], wa_ref[...]) + _dot(bn_ref[...], wb_ref[...]))


def _outproj(a, b, ga, gb, wa, wb, h, tm, tn):
    m, wdt = a.shape
    d = h.shape[1]
    return pl.pallas_call(
        _outproj_kernel,
        grid=(m // tm, d // tn),
        in_specs=[
            pl.BlockSpec((tm, wdt), lambda i, j: (i, 0)),
            pl.BlockSpec((tm, wdt), lambda i, j: (i, 0)),
            pl.BlockSpec((1, wdt), lambda i, j: (0, 0)),
            pl.BlockSpec((1, wdt), lambda i, j: (0, 0)),
            pl.BlockSpec((wdt, tn), lambda i, j: (0, j)),
            pl.BlockSpec((wdt, tn), lambda i, j: (0, j)),
            pl.BlockSpec((tm, tn), lambda i, j: (i, j)),
        ],
        out_specs=pl.BlockSpec((tm, tn), lambda i, j: (i, j)),
        out_shape=jax.ShapeDtypeStruct((m, d), F32),
        scratch_shapes=[pltpu.VMEM((tm, wdt), BF16), pltpu.VMEM((tm, wdt), BF16)],
        compiler_params=_cparams(("arbitrary", "arbitrary")),
        name="outproj",
    )(a, b, ga, gb, wa, wb, h)


def _ffn_kernel(x_ref, g_ref, wg_ref, wu_ref, wd_ref, o_ref, xn_ref, acc_ref):
    j = pl.program_id(1)

    @pl.when(j == 0)
    def _():
        x = x_ref[...]
        ms = jnp.mean(x * x, axis=-1, keepdims=True)
        xn_ref[...] = ((x * lax.rsqrt(ms + EPS)) * g_ref[...]).astype(BF16)
        acc_ref[...] = jnp.zeros_like(acc_ref)

    xn = xn_ref[...]
    gate = _dot(xn, wg_ref[...])
    up = _dot(xn, wu_ref[...])
    act = (gate / (1.0 + jnp.exp(-gate))) * up
    acc_ref[...] += _dot(act.astype(BF16), wd_ref[...])

    @pl.when(j == pl.num_programs(1) - 1)
    def _():
        o_ref[...] = x_ref[...] + acc_ref[...]


def _ffn(x, g, wg, wu, wd, tm, tf):
    m, d = x.shape
    f = wg.shape[1]
    return pl.pallas_call(
        _ffn_kernel,
        grid=(m // tm, f // tf),
        in_specs=[
            pl.BlockSpec((tm, d), lambda i, j: (i, 0)),
            pl.BlockSpec((1, d), lambda i, j: (0, 0)),
            pl.BlockSpec((d, tf), lambda i, j: (0, j)),
            pl.BlockSpec((d, tf), lambda i, j: (0, j)),
            pl.BlockSpec((tf, d), lambda i, j: (j, 0)),
        ],
        out_specs=pl.BlockSpec((tm, d), lambda i, j: (i, 0)),
        out_shape=jax.ShapeDtypeStruct((m, d), F32),
        scratch_shapes=[pltpu.VMEM((tm, d), BF16), pltpu.VMEM((tm, d), F32)],
        compiler_params=_cparams(("arbitrary", "arbitrary")),
        name="ffn",
    )(x, g, wg, wu, wd)


def _head_rmsnorm(x, g):
    r, c = x.shape
    same_head = (_iota((LANES, LANES), 0) // HEAD_DIM) == (_iota((LANES, LANES), 1) // HEAD_DIM)
    bd = jnp.where(same_head, 1.0, 0.0).astype(BF16)
    outs = []
    for t in range(c // LANES):
        xb = x[:, t * LANES:(t + 1) * LANES]
        hi, lo = _split_bf16(xb * xb)
        ss = _dot(hi, bd) + _dot(lo, bd)
        outs.append((xb * lax.rsqrt(ss * (1.0 / HEAD_DIM) + EPS)) * g[:, t * LANES:(t + 1) * LANES])
    return outs


def _half_select(half):
    src = _iota((LANES, LANES), 0)
    dst = _iota((LANES, LANES), 1)
    return jnp.where(src == (dst % HEAD_DIM) + HEAD_DIM * half, 1.0, 0.0).astype(BF16)


def _lane_halves(x):
    lo_half = _iota(x.shape, 1) < HEAD_DIM
    zero = jnp.zeros_like(x)
    return jnp.where(lo_half, x, zero), jnp.where(lo_half, zero, x)


def _merge_halves(lo, hi):
    return jnp.where(_iota(lo.shape, 1) < HEAD_DIM, lo, hi)


def _alibi_slope(h):
    return 2.0 ** (-8.0 * (h + 1) / SWA_HEADS)


def _softplus(z):
    return jnp.maximum(z, 0.0) + jnp.log1p(jnp.exp(-jnp.abs(z)))


def _suffix_matrix(n):
    return jnp.where(_iota((n, n), 0) > _iota((n, n), 1), 1.0, 0.0).astype(BF16)


def _sb_block(qe, kblk, vblk, suffix, carry, mask):
    z = _dot_nt(qe, kblk)
    sp = _softplus(z)
    if mask is not None:
        sp = jnp.where(mask, sp, 0.0)
    hi, lo = _split_bf16(sp)
    later = _dot(hi, suffix) + _dot(lo, suffix)
    a = jnp.exp((z - sp) - (later + carry))
    if mask is not None:
        a = jnp.where(mask, a, 0.0)
    contrib = _dot(a.astype(BF16), vblk)
    return contrib, carry + jnp.sum(sp, axis=-1, keepdims=True)


def _swa_head(qe, kdup, vdup, bias, mask, sink):
    s = jnp.where(mask, _dot_nt(qe, kdup) - bias, NEG)
    mx = jnp.maximum(jnp.max(s, axis=-1, keepdims=True), sink)
    e = jnp.exp(s - mx)
    den = jnp.sum(e, axis=-1, keepdims=True) + jnp.exp(sink - mx)
    return _dot(e.astype(BF16), vdup) / den


def _pad_rows(x, rows):
    return jnp.concatenate([x, jnp.zeros((rows - x.shape[0], x.shape[1]), x.dtype)], axis=0)


def _dup_kv(x_bf, kv_head):
    t = kv_head // 2
    return _dot(x_bf[:, t * LANES:(t + 1) * LANES], _half_select(kv_head % 2)).astype(BF16)


def _swa_kernel(sink_ref, q_ref, kc_ref, kp_ref, vc_ref, vp_ref, km_ref, vm_ref, qg_ref, kg_ref, o_ref):
    n = pl.program_id(1)
    group = SWA_HEADS // SWA_KV_HEADS
    qn = [(x * SCALE).astype(BF16) for x in _head_rmsnorm(q_ref[...], qg_ref[...])]
    kcat = jnp.concatenate([_pad_rows(km_ref[...], BLOCK), kp_ref[...], kc_ref[...]], axis=0)
    kn = jnp.concatenate(_head_rmsnorm(kcat, kg_ref[...]), axis=1).astype(BF16)
    vcat = jnp.concatenate([_pad_rows(vm_ref[...], BLOCK), vp_ref[...], vc_ref[...]], axis=0).astype(BF16)

    qi = _iota((BLOCK, 3 * BLOCK), 0)
    col = _iota((BLOCK, 3 * BLOCK), 1)
    is_meta = col < BLOCK
    band = qi - col + 2 * BLOCK
    dist = jnp.where(is_meta, N_META + n * BLOCK + qi - col, band).astype(F32)
    band_ok = (band >= 0) & (band < BLOCK) & ((col >= 2 * BLOCK) | (n > 0))
    mask = (col < N_META) | ((col >= BLOCK) & band_ok)

    for pair in range(SWA_HEADS // 2):
        kv = (2 * pair) // group
        kdup, vdup = _dup_kv(kn, kv), _dup_kv(vcat, kv)
        outs = []
        for e, qe in enumerate(_lane_halves(qn[pair])):
            h = 2 * pair + e
            outs.append(_swa_head(qe, kdup, vdup, _alibi_slope(h) * dist, mask, sink_ref[h]))
        o_ref[:, pair * LANES:(pair + 1) * LANES] = _merge_halves(outs[0], outs[1])


def _swa(proj, proj_meta, sinks, qg, kg, batch):
    m = proj.shape[0]
    nb = m // batch // BLOCK
    qw = SWA_HEADS * HEAD_DIM
    kvw = SWA_KV_HEADS * HEAD_DIM
    kcol, vcol = qw // kvw, qw // kvw + 1
    row = lambda b, n: b * nb + n
    return pl.pallas_call(
        _swa_kernel,
        grid=(batch, nb),
        in_specs=[
            pl.BlockSpec(memory_space=pltpu.SMEM),
            pl.BlockSpec((BLOCK, qw), lambda b, n: (row(b, n), 0)),
            pl.BlockSpec((BLOCK, kvw), lambda b, n: (row(b, n), kcol)),
            pl.BlockSpec((BLOCK, kvw), lambda b, n: (row(b, jnp.maximum(n - 1, 0)), kcol)),
            pl.BlockSpec((BLOCK, kvw), lambda b, n: (row(b, n), vcol)),
            pl.BlockSpec((BLOCK, kvw), lambda b, n: (row(b, jnp.maximum(n - 1, 0)), vcol)),
            pl.BlockSpec((N_META, kvw), lambda b, n: (b, kcol)),
            pl.BlockSpec((N_META, kvw), lambda b, n: (b, vcol)),
            pl.BlockSpec((1, qw), lambda b, n: (0, 0)),
            pl.BlockSpec((1, kvw), lambda b, n: (0, 0)),
        ],
        out_specs=pl.BlockSpec((BLOCK, qw), lambda b, n: (row(b, n), 0)),
        out_shape=jax.ShapeDtypeStruct((m, qw), F32),
        compiler_params=_cparams(("arbitrary", "arbitrary")),
        name="swa",
    )(sinks, proj, proj, proj, proj, proj, proj_meta, proj_meta, qg, kg)


def _sb_kernel(q_ref, k_ref, v_ref, km_ref, vm_ref, o_ref, kbf_ref, vbf_ref, kmp_ref, vmp_ref, *, tq):
    i = pl.program_id(2)

    @pl.when(i == 0)
    def _():
        kbf_ref[...] = k_ref[...].astype(BF16)
        vbf_ref[...] = v_ref[...].astype(BF16)
        kmp_ref[...] = jnp.zeros_like(kmp_ref)
        vmp_ref[...] = jnp.zeros_like(vmp_ref)
        kmp_ref[0:N_META, :] = km_ref[...].astype(BF16)
        vmp_ref[0:N_META, :] = vm_ref[...].astype(BF16)

    q_lo, q_hi = _lane_halves((q_ref[...] * SCALE).astype(BF16))
    suffix = _suffix_matrix(tq)
    suffix_m = _suffix_matrix(LANES)
    zero_c = jnp.zeros((tq, 1), F32)

    diag = _iota((tq, tq), 1) < _iota((tq, tq), 0)
    start = pl.multiple_of(i * tq, tq)
    kd, vd = kbf_ref[pl.ds(start, tq), :], vbf_ref[pl.ds(start, tq), :]
    acc_lo, c_lo = _sb_block(q_lo, kd, vd, suffix, zero_c, diag)
    acc_hi, c_hi = _sb_block(q_hi, kd, vd, suffix, zero_c, diag)

    def body(t, carry):
        acc_lo, c_lo, acc_hi, c_hi = carry
        s = pl.multiple_of((i - 1 - t) * tq, tq)
        kb, vb = kbf_ref[pl.ds(s, tq), :], vbf_ref[pl.ds(s, tq), :]
        d_lo, c_lo = _sb_block(q_lo, kb, vb, suffix, c_lo, None)
        d_hi, c_hi = _sb_block(q_hi, kb, vb, suffix, c_hi, None)
        return acc_lo + d_lo, c_lo, acc_hi + d_hi, c_hi

    acc_lo, c_lo, acc_hi, c_hi = lax.fori_loop(0, i, body, (acc_lo, c_lo, acc_hi, c_hi))

    meta_ok = _iota((tq, LANES), 1) < N_META
    d_lo, _ = _sb_block(q_lo, kmp_ref[...], vmp_ref[...], suffix_m, c_lo, meta_ok)
    d_hi, _ = _sb_block(q_hi, kmp_ref[...], vmp_ref[...], suffix_m, c_hi, meta_ok)
    o_ref[...] = _merge_halves(acc_lo + d_lo, acc_hi + d_hi)


def _sb(proj, proj_meta, batch, tq):
    m = proj.shape[0]
    seq = m // batch
    nq = seq // tq
    qcol = (SWA_HEADS + 2 * SWA_KV_HEADS) * HEAD_DIM // LANES
    pairs = SB_HEADS // 2
    kcol, vcol = qcol + pairs, qcol + 2 * pairs
    return pl.pallas_call(
        functools.partial(_sb_kernel, tq=tq),
        grid=(batch, pairs, nq),
        in_specs=[
            pl.BlockSpec((tq, LANES), lambda b, p, i: (b * nq + i, qcol + p)),
            pl.BlockSpec((seq, LANES), lambda b, p, i: (b, kcol + p)),
            pl.BlockSpec((seq, LANES), lambda b, p, i: (b, vcol + p)),
            pl.BlockSpec((N_META, LANES), lambda b, p, i: (b, kcol + p)),
            pl.BlockSpec((N_META, LANES), lambda b, p, i: (b, vcol + p)),
        ],
        out_specs=pl.BlockSpec((tq, LANES), lambda b, p, i: (b * nq + i, p)),
        out_shape=jax.ShapeDtypeStruct((m, SB_HEADS * HEAD_DIM), F32),
        scratch_shapes=[pltpu.VMEM((seq, LANES), BF16), pltpu.VMEM((seq, LANES), BF16),
                        pltpu.VMEM((LANES, LANES), BF16), pltpu.VMEM((LANES, LANES), BF16)],
        compiler_params=_cparams(("arbitrary", "arbitrary", "arbitrary")),
        name="stickbreak",
    )(proj, proj, proj, proj_meta, proj_meta)


def _meta_attn_kernel(sink_ref, p_ref, qg_ref, kg_ref, oa_ref, ob_ref):
    qw = SWA_HEADS * HEAD_DIM
    kvw = SWA_KV_HEADS * HEAD_DIM
    group = SWA_HEADS // SWA_KV_HEADS
    x = p_ref[...]
    xk = _pad_rows(x, LANES)
    qi = _iota((N_META, LANES), 0)
    kj = _iota((N_META, LANES), 1)

    qn = [(t * SCALE).astype(BF16) for t in _head_rmsnorm(x[:, 0:qw], qg_ref[...])]
    kn = jnp.concatenate(_head_rmsnorm(xk[:, qw:qw + kvw], kg_ref[...]), axis=1).astype(BF16)
    vb = xk[:, qw + kvw:qw + 2 * kvw].astype(BF16)
    delta_f = (qi - kj).astype(F32)
    causal = kj <= qi
    for pair in range(SWA_HEADS // 2):
        kv = (2 * pair) // group
        kdup, vdup = _dup_kv(kn, kv), _dup_kv(vb, kv)
        outs = []
        for e, qe in enumerate(_lane_halves(qn[pair])):
            h = 2 * pair + e
            outs.append(_swa_head(qe, kdup, vdup, _alibi_slope(h) * delta_f, causal, sink_ref[h]))
        oa_ref[:, pair * LANES:(pair + 1) * LANES] = _merge_halves(outs[0], outs[1])

    base = qw + 2 * kvw
    sbw = SB_HEADS * HEAD_DIM
    strict = kj < qi
    suffix = _suffix_matrix(LANES)
    zero_c = jnp.zeros((N_META, 1), F32)
    for pair in range(SB_HEADS // 2):
        sl = lambda src, part: src[:, base + part * sbw + pair * LANES: base + part * sbw + (pair + 1) * LANES]
        kb, vb2 = sl(xk, 1).astype(BF16), sl(xk, 2).astype(BF16)
        outs = [_sb_block(qe, kb, vb2, suffix, zero_c, strict)[0]
                for qe in _lane_halves((sl(x, 0) * SCALE).astype(BF16))]
        ob_ref[:, pair * LANES:(pair + 1) * LANES] = _merge_halves(outs[0], outs[1])


def _meta_attn(proj_meta, sinks, qg, kg, batch):
    pw = proj_meta.shape[1]
    qw = SWA_HEADS * HEAD_DIM
    kvw = SWA_KV_HEADS * HEAD_DIM
    sbw = SB_HEADS * HEAD_DIM
    return pl.pallas_call(
        _meta_attn_kernel,
        grid=(batch,),
        in_specs=[
            pl.BlockSpec(memory_space=pltpu.SMEM),
            pl.BlockSpec((N_META, pw), lambda b: (b, 0)),
            pl.BlockSpec((1, qw), lambda b: (0, 0)),
            pl.BlockSpec((1, kvw), lambda b: (0, 0)),
        ],
        out_specs=[pl.BlockSpec((N_META, qw), lambda b: (b, 0)),
                   pl.BlockSpec((N_META, sbw), lambda b: (b, 0))],
        out_shape=[jax.ShapeDtypeStruct((batch * N_META, qw), F32),
                   jax.ShapeDtypeStruct((batch * N_META, sbw), F32)],
        compiler_params=_cparams(("arbitrary",)),
        name="meta_attn",
    )(sinks, proj_meta, qg, kg)


def kernel(x, meta_tokens, attn_norm_g, w_in, q_norm_g, k_norm_g, attn_sinks,
           swa_out_g, sb_out_g, w_o, ffn_norm_g, w_gate, w_up, w_down):
    batch, seq, d = x.shape
    depth = w_in.shape[0]
    qw = SWA_HEADS * HEAD_DIM
    assert seq % 512 == 0 and meta_tokens.shape[0] == N_META

    tm = 512
    tmeta = batch * N_META
    tn_in = 1536
    tn_out = 1024
    tf = 512
    tq = 256

    h = x.reshape(batch * seq, d)
    hm = jnp.broadcast_to(meta_tokens[None].astype(x.dtype), (batch, N_META, d)).reshape(tmeta, d)

    w_in_b, w_o_b = w_in.astype(BF16), w_o.astype(BF16)
    w_gate_b, w_up_b, w_down_b = w_gate.astype(BF16), w_up.astype(BF16), w_down.astype(BF16)

    for l in range(depth):
        g_attn = attn_norm_g[l].reshape(1, d)
        qg = jnp.tile(q_norm_g[l], SWA_HEADS).reshape(1, -1)
        kg = jnp.tile(k_norm_g[l], SWA_KV_HEADS).reshape(1, -1)
        sinks = attn_sinks[l].astype(F32)
        ga, gb = swa_out_g[l].reshape(1, -1), sb_out_g[l].reshape(1, -1)
        wa, wb = w_o_b[l, :qw], w_o_b[l, qw:]
        g_ffn = ffn_norm_g[l].reshape(1, d)

        proj = _norm_matmul(h, g_attn, w_in_b[l], tm, tn_in)
        proj_m = _norm_matmul(hm, g_attn, w_in_b[l], tmeta, tn_in)

        out_a = _swa(proj, proj_m, sinks, qg, kg, batch)
        out_b = _sb(proj, proj_m, batch, tq)
        out_am, out_bm = _meta_attn(proj_m, sinks, qg, kg, batch)

        h = _outproj(out_a, out_b, ga, gb, wa, wb, h, tm, tn_out)
        hm = _outproj(out_am, out_bm, ga, gb, wa, wb, hm, tmeta, tn_out)

        h = _ffn(h, g_ffn, w_gate_b[l], w_up_b[l], w_down_b[l], tm, tf)
        hm = _ffn(hm, g_ffn, w_gate_b[l], w_up_b[l], w_down_b[l], tmeta, tf)

    return h.reshape(batch, seq, d)
```

```python
import functools

import jax
import jax.numpy as jnp
from jax import lax
from jax.experimental import pallas as pl
from jax.experimental.pallas import tpu as pltpu

F32 = jnp.float32
BF16 = jnp.bfloat16

N_META = 16
BLOCK = 128
HEAD_DIM = 64
LANES = 128
SWA_HEADS = 16
SWA_KV_HEADS = 4
SB_HEADS = 16
EPS = 1e-6
NEG = -1e30
SCALE = HEAD_DIM ** -0.5
LOG2E = 1.4426950408889634
SB_QSCALE = SCALE * LOG2E
VMEM_LIMIT = 56 * 1024 * 1024


def _cparams(sem):
    return pltpu.CompilerParams(dimension_semantics=sem, vmem_limit_bytes=VMEM_LIMIT)


def _dot(a, b):
    return jnp.dot(a, b, preferred_element_type=F32)


def _dot_nt(a, b):
    return lax.dot_general(a, b, (((1,), (1,)), ((), ())), preferred_element_type=F32)


def _split_bf16(x):
    hi = x.astype(BF16)
    lo = (x - hi.astype(F32)).astype(BF16)
    return hi, lo


def _iota(shape, dim):
    return lax.broadcasted_iota(jnp.int32, shape, dim)


def _norm_mm_kernel(x_ref, g_ref, w_ref, o_ref, xn_ref):
    @pl.when(pl.program_id(1) == 0)
    def _():
        x = x_ref[...]
        ms = jnp.mean(x * x, axis=-1, keepdims=True)
        xn_ref[...] = ((x * lax.rsqrt(ms + EPS)) * g_ref[...]).astype(BF16)

    o_ref[...] = _dot(xn_ref[...], w_ref[...])


def _norm_matmul(x, g, w, tm, tn):
    m, d = x.shape
    n = w.shape[1]
    return pl.pallas_call(
        _norm_mm_kernel,
        grid=(m // tm, n // tn),
        in_specs=[
            pl.BlockSpec((tm, d), lambda i, j: (i, 0)),
            pl.BlockSpec((1, d), lambda i, j: (0, 0)),
            pl.BlockSpec((d, tn), lambda i, j: (0, j)),
        ],
        out_specs=pl.BlockSpec((tm, tn), lambda i, j: (i, j)),
        out_shape=jax.ShapeDtypeStruct((m, n), F32),
        scratch_shapes=[pltpu.VMEM((tm, d), BF16)],
        compiler_params=_cparams(("arbitrary", "arbitrary")),
        name="norm_inproj",
    )(x, g, w)


def _outproj_kernel(a_ref, b_ref, ga_ref, gb_ref, wa_ref, wb_ref, h_ref, o_ref, an_ref, bn_ref):
    @pl.when(pl.program_id(1) == 0)
    def _():
        for src, g, dst in ((a_ref, ga_ref, an_ref), (b_ref, gb_ref, bn_ref)):
            x = src[...]
            ms = jnp.mean(x * x, axis=-1, keepdims=True)
            dst[...] = ((x * lax.rsqrt(ms + EPS)) * g[...]).astype(BF16)

    o_ref[...] = h_ref[...] + (_dot(an_ref[...], wa_ref[...]) + _dot(bn_ref[...], wb_ref[...]))


def _outproj(a, b, ga, gb, wa, wb, h, tm, tn):
    m, wdt = a.shape
    d = h.shape[1]
    return pl.pallas_call(
        _outproj_kernel,
        grid=(m // tm, d // tn),
        in_specs=[
            pl.BlockSpec((tm, wdt), lambda i, j: (i, 0)),
            pl.BlockSpec((tm, wdt), lambda i, j: (i, 0)),
            pl.BlockSpec((1, wdt), lambda i, j: (0, 0)),
            pl.BlockSpec((1, wdt), lambda i, j: (0, 0)),
            pl.BlockSpec((wdt, tn), lambda i, j: (0, j)),
            pl.BlockSpec((wdt, tn), lambda i, j: (0, j)),
            pl.BlockSpec((tm, tn), lambda i, j: (i, j)),
        ],
        out_specs=pl.BlockSpec((tm, tn), lambda i, j: (i, j)),
        out_shape=jax.ShapeDtypeStruct((m, d), F32),
        scratch_shapes=[pltpu.VMEM((tm, wdt), BF16), pltpu.VMEM((tm, wdt), BF16)],
        compiler_params=_cparams(("arbitrary", "arbitrary")),
        name="outproj",
    )(a, b, ga, gb, wa, wb, h)


def _ffn_kernel(x_ref, g_ref, wg_ref, wu_ref, wd_ref, o_ref, xn_ref, acc_ref):
    j = pl.program_id(1)

    @pl.when(j == 0)
    def _():
        x = x_ref[...]
        ms = jnp.mean(x * x, axis=-1, keepdims=True)
        xn_ref[...] = ((x * lax.rsqrt(ms + EPS)) * g_ref[...]).astype(BF16)
        acc_ref[...] = jnp.zeros_like(acc_ref)

    xn = xn_ref[...]
    gate = _dot(xn, wg_ref[...])
    up = _dot(xn, wu_ref[...])
    act = (gate / (1.0 + jnp.exp(-gate))) * up
    acc_ref[...] += _dot(act.astype(BF16), wd_ref[...])

    @pl.when(j == pl.num_programs(1) - 1)
    def _():
        o_ref[...] = x_ref[...] + acc_ref[...]


def _ffn(x, g, wg, wu, wd, tm, tf):
    m, d = x.shape
    f = wg.shape[1]
    return pl.pallas_call(
        _ffn_kernel,
        grid=(m // tm, f // tf),
        in_specs=[
            pl.BlockSpec((tm, d), lambda i, j: (i, 0)),
            pl.BlockSpec((1, d), lambda i, j: (0, 0)),
            pl.BlockSpec((d, tf), lambda i, j: (0, j)),
            pl.BlockSpec((d, tf), lambda i, j: (0, j)),
            pl.BlockSpec((tf, d), lambda i, j: (j, 0)),
        ],
        out_specs=pl.BlockSpec((tm, d), lambda i, j: (i, 0)),
        out_shape=jax.ShapeDtypeStruct((m, d), F32),
        scratch_shapes=[pltpu.VMEM((tm, d), BF16), pltpu.VMEM((tm, d), F32)],
        compiler_params=_cparams(("arbitrary", "arbitrary")),
        name="ffn",
    )(x, g, wg, wu, wd)


def _head_rmsnorm(x, g):
    r, c = x.shape
    same_head = (_iota((LANES, LANES), 0) // HEAD_DIM) == (_iota((LANES, LANES), 1) // HEAD_DIM)
    bd = jnp.where(same_head, 1.0, 0.0).astype(BF16)
    outs = []
    for t in range(c // LANES):
        xb = x[:, t * LANES:(t + 1) * LANES]
        hi, lo = _split_bf16(xb * xb)
        ss = _dot(hi, bd) + _dot(lo, bd)
        outs.append((xb * lax.rsqrt(ss * (1.0 / HEAD_DIM) + EPS)) * g[:, t * LANES:(t + 1) * LANES])
    return outs


def _half_select(half):
    src = _iota((LANES, LANES), 0)
    dst = _iota((LANES, LANES), 1)
    return jnp.where(src == (dst % HEAD_DIM) + HEAD_DIM * half, 1.0, 0.0).astype(BF16)


def _lane_halves(x):
    lo_half = _iota(x.shape, 1) < HEAD_DIM
    zero = jnp.zeros_like(x)
    return jnp.where(lo_half, x, zero), jnp.where(lo_half, zero, x)


def _merge_halves(lo, hi):
    return jnp.where(_iota(lo.shape, 1) < HEAD_DIM, lo, hi)


def _alibi_slope(h):
    return 2.0 ** (-8.0 * (h + 1) / SWA_HEADS)


def _softplus2(z2):
    neg_abs = lax.bitcast_convert_type(
        lax.bitcast_convert_type(z2, jnp.uint32) | jnp.uint32(0x80000000), F32)
    return jnp.maximum(z2, 0.0) + jnp.log2(1.0 + jnp.exp2(neg_abs))


def _suffix_matrix(n):
    return jnp.where(_iota((n, n), 0) > _iota((n, n), 1), 1.0, 0.0).astype(BF16)


def _sb_block(qe, kblk, vblk, suffix, carry, mask):
    z2 = _dot_nt(qe, kblk)
    sp = _softplus2(z2)
    if mask is not None:
        sp = jnp.where(mask, sp, 0.0)
    later = _dot(sp.astype(BF16), suffix)
    a = jnp.exp2((z2 - sp) - (later + carry))
    if mask is not None:
        a = jnp.where(mask, a, 0.0)
    contrib = _dot(a.astype(BF16), vblk)
    return contrib, carry + jnp.sum(sp, axis=-1, keepdims=True)


def _swa_head(qe, kdup, vdup, bias, mask, sink):
    s = jnp.where(mask, _dot_nt(qe, kdup) - bias, NEG)
    mx = jnp.maximum(jnp.max(s, axis=-1, keepdims=True), sink)
    e = jnp.exp(s - mx)
    den = jnp.sum(e, axis=-1, keepdims=True) + jnp.exp(sink - mx)
    return _dot(e.astype(BF16), vdup) / den


def _pad_rows(x, rows):
    return jnp.concatenate([x, jnp.zeros((rows - x.shape[0], x.shape[1]), x.dtype)], axis=0)


def _dup_kv(x_bf, kv_head):
    t = kv_head // 2
    return _dot(x_bf[:, t * LANES:(t + 1) * LANES], _half_select(kv_head % 2)).astype(BF16)


def _swa_kernel(sink_ref, q_ref, kc_ref, kp_ref, vc_ref, vp_ref, km_ref, vm_ref, qg_ref, kg_ref, o_ref):
    n = pl.program_id(1)
    group = SWA_HEADS // SWA_KV_HEADS
    qn = [(x * SCALE).astype(BF16) for x in _head_rmsnorm(q_ref[...], qg_ref[...])]
    kcat = jnp.concatenate([_pad_rows(km_ref[...], BLOCK), kp_ref[...], kc_ref[...]], axis=0)
    kn = jnp.concatenate(_head_rmsnorm(kcat, kg_ref[...]), axis=1).astype(BF16)
    vcat = jnp.concatenate([_pad_rows(vm_ref[...], BLOCK), vp_ref[...], vc_ref[...]], axis=0).astype(BF16)

    qi = _iota((BLOCK, 3 * BLOCK), 0)
    col = _iota((BLOCK, 3 * BLOCK), 1)
    is_meta = col < BLOCK
    band = qi - col + 2 * BLOCK
    dist = jnp.where(is_meta, N_META + n * BLOCK + qi - col, band).astype(F32)
    band_ok = (band >= 0) & (band < BLOCK) & ((col >= 2 * BLOCK) | (n > 0))
    mask = (col < N_META) | ((col >= BLOCK) & band_ok)

    for pair in range(SWA_HEADS // 2):
        kv = (2 * pair) // group
        kdup, vdup = _dup_kv(kn, kv), _dup_kv(vcat, kv)
        outs = []
        for e, qe in enumerate(_lane_halves(qn[pair])):
            h = 2 * pair + e
            outs.append(_swa_head(qe, kdup, vdup, _alibi_slope(h) * dist, mask, sink_ref[h]))
        o_ref[:, pair * LANES:(pair + 1) * LANES] = _merge_halves(outs[0], outs[1])


def _swa(proj, proj_meta, sinks, qg, kg, batch):
    m = proj.shape[0]
    nb = m // batch // BLOCK
    qw = SWA_HEADS * HEAD_DIM
    kvw = SWA_KV_HEADS * HEAD_DIM
    kcol, vcol = qw // kvw, qw // kvw + 1
    row = lambda b, n: b * nb + n
    return pl.pallas_call(
        _swa_kernel,
        grid=(batch, nb),
        in_specs=[
            pl.BlockSpec(memory_space=pltpu.SMEM),
            pl.BlockSpec((BLOCK, qw), lambda b, n: (row(b, n), 0)),
            pl.BlockSpec((BLOCK, kvw), lambda b, n: (row(b, n), kcol)),
            pl.BlockSpec((BLOCK, kvw), lambda b, n: (row(b, jnp.maximum(n - 1, 0)), kcol)),
            pl.BlockSpec((BLOCK, kvw), lambda b, n: (row(b, n), vcol)),
            pl.BlockSpec((BLOCK, kvw), lambda b, n: (row(b, jnp.maximum(n - 1, 0)), vcol)),
            pl.BlockSpec((N_META, kvw), lambda b, n: (b, kcol)),
            pl.BlockSpec((N_META, kvw), lambda b, n: (b, vcol)),
            pl.BlockSpec((1, qw), lambda b, n: (0, 0)),
            pl.BlockSpec((1, kvw), lambda b, n: (0, 0)),
        ],
        out_specs=pl.BlockSpec((BLOCK, qw), lambda b, n: (row(b, n), 0)),
        out_shape=jax.ShapeDtypeStruct((m, qw), F32),
        compiler_params=_cparams(("arbitrary", "arbitrary")),
        name="swa",
    )(sinks, proj, proj, proj, proj, proj, proj_meta, proj_meta, qg, kg)


def _sb_kernel(q_ref, k_ref, v_ref, km_ref, vm_ref, o_ref, kbf_ref, vbf_ref, kmp_ref, vmp_ref, *, tq, hp):
    i = pl.program_id(2)

    @pl.when(i == 0)
    def _():
        kbf_ref[...] = k_ref[...].astype(BF16)
        vbf_ref[...] = v_ref[...].astype(BF16)
        kmp_ref[...] = jnp.zeros_like(kmp_ref)
        vmp_ref[...] = jnp.zeros_like(vmp_ref)
        kmp_ref[0:N_META, :] = km_ref[...].astype(BF16)
        vmp_ref[0:N_META, :] = vm_ref[...].astype(BF16)

    lanes = lambda p: slice(p * LANES, (p + 1) * LANES)
    qs = [jnp.concatenate(_lane_halves((q_ref[:, lanes(p)] * SB_QSCALE).astype(BF16)), axis=0)
          for p in range(hp)]
    rows = 2 * tq
    suffix = _suffix_matrix(tq)

    def step(kv_of_pair, suffix, acc, carry, mask):
        kv = [kv_of_pair(p) for p in range(hp)]
        z2 = [_dot_nt(qs[p], kv[p][0]) for p in range(hp)]
        sp = [_softplus2(z) for z in z2]
        if mask is not None:
            sp = [jnp.where(mask, s, 0.0) for s in sp]
        zs = [z2[p] - sp[p] for p in range(hp)]
        later = _dot(jnp.concatenate([s.astype(BF16) for s in sp], axis=0), suffix)
        a = [jnp.exp2(zs[p] - (later[p * rows:(p + 1) * rows] + carry[p])) for p in range(hp)]
        if mask is not None:
            a = [jnp.where(mask, x, 0.0) for x in a]
        d = [_dot(a[p].astype(BF16), kv[p][1]) for p in range(hp)]
        acc = d if acc is None else [acc[p] + d[p] for p in range(hp)]
        return acc, [carry[p] + jnp.sum(sp[p], axis=-1, keepdims=True) for p in range(hp)]

    diag = _iota((rows, tq), 1) < (_iota((rows, tq), 0) % tq)
    start = pl.multiple_of(i * tq, tq)
    acc, carry = step(lambda p: (kbf_ref[pl.ds(start, tq), lanes(p)], vbf_ref[pl.ds(start, tq), lanes(p)]),
                      suffix, None, [jnp.zeros((rows, 1), F32)] * hp, diag)

    def body(t, state):
        s = pl.multiple_of((i - 1 - t) * tq, tq)
        acc, carry = step(lambda p: (kbf_ref[pl.ds(s, tq), lanes(p)], vbf_ref[pl.ds(s, tq), lanes(p)]),
                          suffix, list(state[:hp]), list(state[hp:]), None)
        return tuple(acc) + tuple(carry)

    state = lax.fori_loop(0, i, body, tuple(acc) + tuple(carry))

    meta_ok = _iota((rows, LANES), 1) < N_META
    acc, _ = step(lambda p: (kmp_ref[:, lanes(p)], vmp_ref[:, lanes(p)]),
                  _suffix_matrix(LANES), list(state[:hp]), list(state[hp:]), meta_ok)
    for p in range(hp):
        o_ref[:, lanes(p)] = _merge_halves(acc[p][:tq], acc[p][tq:])


def _sb(proj, proj_meta, batch, tq, hp):
    m = proj.shape[0]
    seq = m // batch
    nq = seq // tq
    w = hp * LANES
    qcol = (SWA_HEADS + 2 * SWA_KV_HEADS) * HEAD_DIM // w
    steps = SB_HEADS * HEAD_DIM // w
    kcol, vcol = qcol + steps, qcol + 2 * steps
    return pl.pallas_call(
        functools.partial(_sb_kernel, tq=tq, hp=hp),
        grid=(batch, steps, nq),
        in_specs=[
            pl.BlockSpec((tq, w), lambda b, p, i: (b * nq + i, qcol + p)),
            pl.BlockSpec((seq, w), lambda b, p, i: (b, kcol + p)),
            pl.BlockSpec((seq, w), lambda b, p, i: (b, vcol + p)),
            pl.BlockSpec((N_META, w), lambda b, p, i: (b, kcol + p)),
            pl.BlockSpec((N_META, w), lambda b, p, i: (b, vcol + p)),
        ],
        out_specs=pl.BlockSpec((tq, w), lambda b, p, i: (b * nq + i, p)),
        out_shape=jax.ShapeDtypeStruct((m, SB_HEADS * HEAD_DIM), F32),
        scratch_shapes=[pltpu.VMEM((seq, w), BF16), pltpu.VMEM((seq, w), BF16),
                        pltpu.VMEM((LANES, w), BF16), pltpu.VMEM((LANES, w), BF16)],
        compiler_params=_cparams(("arbitrary", "arbitrary", "arbitrary")),
        name="stickbreak",
    )(proj, proj, proj, proj_meta, proj_meta)


def _meta_attn_kernel(sink_ref, p_ref, qg_ref, kg_ref, oa_ref, ob_ref):
    qw = SWA_HEADS * HEAD_DIM
    kvw = SWA_KV_HEADS * HEAD_DIM
    group = SWA_HEADS // SWA_KV_HEADS
    x = p_ref[...]
    xk = _pad_rows(x, LANES)
    qi = _iota((N_META, LANES), 0)
    kj = _iota((N_META, LANES), 1)

    qn = [(t * SCALE).astype(BF16) for t in _head_rmsnorm(x[:, 0:qw], qg_ref[...])]
    kn = jnp.concatenate(_head_rmsnorm(xk[:, qw:qw + kvw], kg_ref[...]), axis=1).astype(BF16)
    vb = xk[:, qw + kvw:qw + 2 * kvw].astype(BF16)
    delta_f = (qi - kj).astype(F32)
    causal = kj <= qi
    for pair in range(SWA_HEADS // 2):
        kv = (2 * pair) // group
        kdup, vdup = _dup_kv(kn, kv), _dup_kv(vb, kv)
        outs = []
        for e, qe in enumerate(_lane_halves(qn[pair])):
            h = 2 * pair + e
            outs.append(_swa_head(qe, kdup, vdup, _alibi_slope(h) * delta_f, causal, sink_ref[h]))
        oa_ref[:, pair * LANES:(pair + 1) * LANES] = _merge_halves(outs[0], outs[1])

    base = qw + 2 * kvw
    sbw = SB_HEADS * HEAD_DIM
    strict = kj < qi
    suffix = _suffix_matrix(LANES)
    zero_c = jnp.zeros((N_META, 1), F32)
    for pair in range(SB_HEADS // 2):
        sl = lambda src, part: src[:, base + part * sbw + pair * LANES: base + part * sbw + (pair + 1) * LANES]
        kb, vb2 = sl(xk, 1).astype(BF16), sl(xk, 2).astype(BF16)
        outs = [_sb_block(qe, kb, vb2, suffix, zero_c, strict)[0]
                for qe in _lane_halves((sl(x, 0) * SB_QSCALE).astype(BF16))]
        ob_ref[:, pair * LANES:(pair + 1) * LANES] = _merge_halves(outs[0], outs[1])


def _meta_attn(proj_meta, sinks, qg, kg, batch):
    pw = proj_meta.shape[1]
    qw = SWA_HEADS * HEAD_DIM
    kvw = SWA_KV_HEADS * HEAD_DIM
    sbw = SB_HEADS * HEAD_DIM
    return pl.pallas_call(
        _meta_attn_kernel,
        grid=(batch,),
        in_specs=[
            pl.BlockSpec(memory_space=pltpu.SMEM),
            pl.BlockSpec((N_META, pw), lambda b: (b, 0)),
            pl.BlockSpec((1, qw), lambda b: (0, 0)),
            pl.BlockSpec((1, kvw), lambda b: (0, 0)),
        ],
        out_specs=[pl.BlockSpec((N_META, qw), lambda b: (b, 0)),
                   pl.BlockSpec((N_META, sbw), lambda b: (b, 0))],
        out_shape=[jax.ShapeDtypeStruct((batch * N_META, qw), F32),
                   jax.ShapeDtypeStruct((batch * N_META, sbw), F32)],
        compiler_params=_cparams(("arbitrary",)),
        name="meta_attn",
    )(sinks, proj_meta, qg, kg)


def kernel(x, meta_tokens, attn_norm_g, w_in, q_norm_g, k_norm_g, attn_sinks,
           swa_out_g, sb_out_g, w_o, ffn_norm_g, w_gate, w_up, w_down):
    batch, seq, d = x.shape
    depth = w_in.shape[0]
    qw = SWA_HEADS * HEAD_DIM
    assert seq % 512 == 0 and meta_tokens.shape[0] == N_META

    tm = 512
    tmeta = batch * N_META
    tn_in = 1536
    tn_out = 1024
    tf = 512
    tq = 256
    hp = 2

    h = x.reshape(batch * seq, d)
    hm = jnp.broadcast_to(meta_tokens[None].astype(x.dtype), (batch, N_META, d)).reshape(tmeta, d)

    w_in_b, w_o_b = w_in.astype(BF16), w_o.astype(BF16)
    w_gate_b, w_up_b, w_down_b = w_gate.astype(BF16), w_up.astype(BF16), w_down.astype(BF16)

    for l in range(depth):
        g_attn = attn_norm_g[l].reshape(1, d)
        qg = jnp.tile(q_norm_g[l], SWA_HEADS).reshape(1, -1)
        kg = jnp.tile(k_norm_g[l], SWA_KV_HEADS).reshape(1, -1)
        sinks = attn_sinks[l].astype(F32)
        ga, gb = swa_out_g[l].reshape(1, -1), sb_out_g[l].reshape(1, -1)
        wa, wb = w_o_b[l, :qw], w_o_b[l, qw:]
        g_ffn = ffn_norm_g[l].reshape(1, d)

        proj = _norm_matmul(h, g_attn, w_in_b[l], tm, tn_in)
        proj_m = _norm_matmul(hm, g_attn, w_in_b[l], tmeta, tn_in)

        out_a = _swa(proj, proj_m, sinks, qg, kg, batch)
        out_b = _sb(proj, proj_m, batch, tq, hp)
        out_am, out_bm = _meta_attn(proj_m, sinks, qg, kg, batch)

        h = _outproj(out_a, out_b, ga, gb, wa, wb, h, tm, tn_out)
        hm = _outproj(out_am, out_bm, ga, gb, wa, wb, hm, tmeta, tn_out)

        h = _ffn(h, g_ffn, w_gate_b[l], w_up_b[l], w_down_b[l], tm, tf)
        hm = _ffn(hm, g_ffn, w_gate_b[l], w_up_b[l], w_down_b[l], tmeta, tf)

    return h.reshape(batch, seq, d)
```

```python
import functools

import jax
import jax.numpy as jnp
from jax import lax
from jax.experimental import pallas as pl
from jax.experimental.pallas import tpu as pltpu

F32 = jnp.float32
BF16 = jnp.bfloat16

N_META = 16
BLOCK = 128
HEAD_DIM = 64
LANES = 128
SWA_HEADS = 16
SWA_KV_HEADS = 4
SB_HEADS = 16
EPS = 1e-6
NEG = -1e30
SCALE = HEAD_DIM ** -0.5
LOG2E = 1.4426950408889634
SB_QSCALE = SCALE * LOG2E
F32_UNDERFLOW_LOG2 = 160.0
VMEM_LIMIT = 56 * 1024 * 1024


def _cparams(sem):
    return pltpu.CompilerParams(dimension_semantics=sem, vmem_limit_bytes=VMEM_LIMIT)


def _dot(a, b):
    return jnp.dot(a, b, preferred_element_type=F32)


def _dot_nt(a, b):
    return lax.dot_general(a, b, (((1,), (1,)), ((), ())), preferred_element_type=F32)


def _split_bf16(x):
    hi = x.astype(BF16)
    lo = (x - hi.astype(F32)).astype(BF16)
    return hi, lo


def _iota(shape, dim):
    return lax.broadcasted_iota(jnp.int32, shape, dim)


def _norm_mm_kernel(x_ref, g_ref, w_ref, o_ref, xn_ref):
    @pl.when(pl.program_id(1) == 0)
    def _():
        x = x_ref[...]
        ms = jnp.mean(x * x, axis=-1, keepdims=True)
        xn_ref[...] = ((x * lax.rsqrt(ms + EPS)) * g_ref[...]).astype(BF16)

    o_ref[...] = _dot(xn_ref[...], w_ref[...])


def _norm_matmul(x, g, w, tm, tn):
    m, d = x.shape
    n = w.shape[1]
    return pl.pallas_call(
        _norm_mm_kernel,
        grid=(m // tm, n // tn),
        in_specs=[
            pl.BlockSpec((tm, d), lambda i, j: (i, 0)),
            pl.BlockSpec((1, d), lambda i, j: (0, 0)),
            pl.BlockSpec((d, tn), lambda i, j: (0, j)),
        ],
        out_specs=pl.BlockSpec((tm, tn), lambda i, j: (i, j)),
        out_shape=jax.ShapeDtypeStruct((m, n), F32),
        scratch_shapes=[pltpu.VMEM((tm, d), BF16)],
        compiler_params=_cparams(("arbitrary", "arbitrary")),
        name="norm_inproj",
    )(x, g, w)


def _outproj_kernel(a_ref, b_ref, ga_ref, gb_ref, wa_ref, wb_ref, h_ref, o_ref, an_ref, bn_ref):
    @pl.when(pl.program_id(1) == 0)
    def _():
        for src, g, dst in ((a_ref, ga_ref, an_ref), (b_ref, gb_ref, bn_ref)):
            x = src[...]
            ms = jnp.mean(x * x, axis=-1, keepdims=True)
            dst[...] = ((x * lax.rsqrt(ms + EPS)) * g[...]).astype(BF16)

    o_ref[...] = h_ref[...] + (_dot(an_ref[...], wa_ref[...]) + _dot(bn_ref[...], wb_ref[...]))


def _outproj(a, b, ga, gb, wa, wb, h, tm, tn):
    m, wdt = a.shape
    d = h.shape[1]
    return pl.pallas_call(
        _outproj_kernel,
        grid=(m // tm, d // tn),
        in_specs=[
            pl.BlockSpec((tm, wdt), lambda i, j: (i, 0)),
            pl.BlockSpec((tm, wdt), lambda i, j: (i, 0)),
            pl.BlockSpec((1, wdt), lambda i, j: (0, 0)),
            pl.BlockSpec((1, wdt), lambda i, j: (0, 0)),
            pl.BlockSpec((wdt, tn), lambda i, j: (0, j)),
            pl.BlockSpec((wdt, tn), lambda i, j: (0, j)),
            pl.BlockSpec((tm, tn), lambda i, j: (i, j)),
        ],
        out_specs=pl.BlockSpec((tm, tn), lambda i, j: (i, j)),
        out_shape=jax.ShapeDtypeStruct((m, d), F32),
        scratch_shapes=[pltpu.VMEM((tm, wdt), BF16), pltpu.VMEM((tm, wdt), BF16)],
        compiler_params=_cparams(("arbitrary", "arbitrary")),
        name="outproj",
    )(a, b, ga, gb, wa, wb, h)


def _ffn_kernel(x_ref, g_ref, wg_ref, wu_ref, wd_ref, o_ref, xn_ref, acc_ref):
    j = pl.program_id(1)

    @pl.when(j == 0)
    def _():
        x = x_ref[...]
        ms = jnp.mean(x * x, axis=-1, keepdims=True)
        xn_ref[...] = ((x * lax.rsqrt(ms + EPS)) * g_ref[...]).astype(BF16)
        acc_ref[...] = jnp.zeros_like(acc_ref)

    xn = xn_ref[...]
    gate = _dot(xn, wg_ref[...])
    up = _dot(xn, wu_ref[...])
    act = (gate / (1.0 + jnp.exp(-gate))) * up
    acc_ref[...] += _dot(act.astype(BF16), wd_ref[...])

    @pl.when(j == pl.num_programs(1) - 1)
    def _():
        o_ref[...] = x_ref[...] + acc_ref[...]


def _ffn(x, g, wg, wu, wd, tm, tf):
    m, d = x.shape
    f = wg.shape[1]
    return pl.pallas_call(
        _ffn_kernel,
        grid=(m // tm, f // tf),
        in_specs=[
            pl.BlockSpec((tm, d), lambda i, j: (i, 0)),
            pl.BlockSpec((1, d), lambda i, j: (0, 0)),
            pl.BlockSpec((d, tf), lambda i, j: (0, j)),
            pl.BlockSpec((d, tf), lambda i, j: (0, j)),
            pl.BlockSpec((tf, d), lambda i, j: (j, 0)),
        ],
        out_specs=pl.BlockSpec((tm, d), lambda i, j: (i, 0)),
        out_shape=jax.ShapeDtypeStruct((m, d), F32),
        scratch_shapes=[pltpu.VMEM((tm, d), BF16), pltpu.VMEM((tm, d), F32)],
        compiler_params=_cparams(("arbitrary", "arbitrary")),
        name="ffn",
    )(x, g, wg, wu, wd)


def _head_rmsnorm(x, g):
    r, c = x.shape
    same_head = (_iota((LANES, LANES), 0) // HEAD_DIM) == (_iota((LANES, LANES), 1) // HEAD_DIM)
    bd = jnp.where(same_head, 1.0, 0.0).astype(BF16)
    outs = []
    for t in range(c // LANES):
        xb = x[:, t * LANES:(t + 1) * LANES]
        hi, lo = _split_bf16(xb * xb)
        ss = _dot(hi, bd) + _dot(lo, bd)
        outs.append((xb * lax.rsqrt(ss * (1.0 / HEAD_DIM) + EPS)) * g[:, t * LANES:(t + 1) * LANES])
    return outs


def _half_select(half):
    src = _iota((LANES, LANES), 0)
    dst = _iota((LANES, LANES), 1)
    return jnp.where(src == (dst % HEAD_DIM) + HEAD_DIM * half, 1.0, 0.0).astype(BF16)


def _lane_halves(x):
    lo_half = _iota(x.shape, 1) < HEAD_DIM
    zero = jnp.zeros_like(x)
    return jnp.where(lo_half, x, zero), jnp.where(lo_half, zero, x)


def _merge_halves(lo, hi):
    return jnp.where(_iota(lo.shape, 1) < HEAD_DIM, lo, hi)


def _alibi_slope(h):
    return 2.0 ** (-8.0 * (h + 1) / SWA_HEADS)


def _softplus2(z2):
    neg_abs = lax.bitcast_convert_type(
        lax.bitcast_convert_type(z2, jnp.uint32) | jnp.uint32(0x80000000), F32)
    return jnp.maximum(z2, 0.0) + jnp.log2(1.0 + jnp.exp2(neg_abs))


def _suffix_matrix(n):
    return jnp.where(_iota((n, n), 0) > _iota((n, n), 1), 1.0, 0.0).astype(BF16)


def _sb_block(qe, kblk, vblk, suffix, carry, mask):
    z2 = _dot_nt(qe, kblk)
    sp = _softplus2(z2)
    if mask is not None:
        sp = jnp.where(mask, sp, 0.0)
    later = _dot(sp.astype(BF16), suffix)
    a = jnp.exp2((z2 - sp) - (later + carry))
    if mask is not None:
        a = jnp.where(mask, a, 0.0)
    contrib = _dot(a.astype(BF16), vblk)
    return contrib, carry + jnp.sum(sp, axis=-1, keepdims=True)


def _swa_head(qe, kdup, vdup, bias, mask, sink):
    s = jnp.where(mask, _dot_nt(qe, kdup) - bias, NEG)
    mx = jnp.maximum(jnp.max(s, axis=-1, keepdims=True), sink)
    e = jnp.exp(s - mx)
    den = jnp.sum(e, axis=-1, keepdims=True) + jnp.exp(sink - mx)
    return _dot(e.astype(BF16), vdup) / den


def _pad_rows(x, rows):
    return jnp.concatenate([x, jnp.zeros((rows - x.shape[0], x.shape[1]), x.dtype)], axis=0)


def _dup_kv(x_bf, kv_head):
    t = kv_head // 2
    return _dot(x_bf[:, t * LANES:(t + 1) * LANES], _half_select(kv_head % 2)).astype(BF16)


def _swa_kernel(sink_ref, q_ref, kc_ref, kp_ref, vc_ref, vp_ref, km_ref, vm_ref, qg_ref, kg_ref, o_ref):
    n = pl.program_id(1)
    group = SWA_HEADS // SWA_KV_HEADS
    qn = [(x * SCALE).astype(BF16) for x in _head_rmsnorm(q_ref[...], qg_ref[...])]
    kcat = jnp.concatenate([_pad_rows(km_ref[...], BLOCK), kp_ref[...], kc_ref[...]], axis=0)
    kn = jnp.concatenate(_head_rmsnorm(kcat, kg_ref[...]), axis=1).astype(BF16)
    vcat = jnp.concatenate([_pad_rows(vm_ref[...], BLOCK), vp_ref[...], vc_ref[...]], axis=0).astype(BF16)

    qi = _iota((BLOCK, 3 * BLOCK), 0)
    col = _iota((BLOCK, 3 * BLOCK), 1)
    is_meta = col < BLOCK
    band = qi - col + 2 * BLOCK
    dist = jnp.where(is_meta, N_META + n * BLOCK + qi - col, band).astype(F32)
    band_ok = (band >= 0) & (band < BLOCK) & ((col >= 2 * BLOCK) | (n > 0))
    mask = (col < N_META) | ((col >= BLOCK) & band_ok)

    for pair in range(SWA_HEADS // 2):
        kv = (2 * pair) // group
        kdup, vdup = _dup_kv(kn, kv), _dup_kv(vcat, kv)
        outs = []
        for e, qe in enumerate(_lane_halves(qn[pair])):
            h = 2 * pair + e
            outs.append(_swa_head(qe, kdup, vdup, _alibi_slope(h) * dist, mask, sink_ref[h]))
        o_ref[:, pair * LANES:(pair + 1) * LANES] = _merge_halves(outs[0], outs[1])


def _swa(proj, proj_meta, sinks, qg, kg, batch):
    m = proj.shape[0]
    nb = m // batch // BLOCK
    qw = SWA_HEADS * HEAD_DIM
    kvw = SWA_KV_HEADS * HEAD_DIM
    kcol, vcol = qw // kvw, qw // kvw + 1
    row = lambda b, n: b * nb + n
    return pl.pallas_call(
        _swa_kernel,
        grid=(batch, nb),
        in_specs=[
            pl.BlockSpec(memory_space=pltpu.SMEM),
            pl.BlockSpec((BLOCK, qw), lambda b, n: (row(b, n), 0)),
            pl.BlockSpec((BLOCK, kvw), lambda b, n: (row(b, n), kcol)),
            pl.BlockSpec((BLOCK, kvw), lambda b, n: (row(b, jnp.maximum(n - 1, 0)), kcol)),
            pl.BlockSpec((BLOCK, kvw), lambda b, n: (row(b, n), vcol)),
            pl.BlockSpec((BLOCK, kvw), lambda b, n: (row(b, jnp.maximum(n - 1, 0)), vcol)),
            pl.BlockSpec((N_META, kvw), lambda b, n: (b, kcol)),
            pl.BlockSpec((N_META, kvw), lambda b, n: (b, vcol)),
            pl.BlockSpec((1, qw), lambda b, n: (0, 0)),
            pl.BlockSpec((1, kvw), lambda b, n: (0, 0)),
        ],
        out_specs=pl.BlockSpec((BLOCK, qw), lambda b, n: (row(b, n), 0)),
        out_shape=jax.ShapeDtypeStruct((m, qw), F32),
        compiler_params=_cparams(("arbitrary", "arbitrary")),
        name="swa",
    )(sinks, proj, proj, proj, proj, proj, proj_meta, proj_meta, qg, kg)


def _sb_kernel(q_ref, k_ref, v_ref, km_ref, vm_ref, o_ref, kbf_ref, vbf_ref, kmp_ref, vmp_ref, *, tq, hp):
    i = pl.program_id(2)

    @pl.when(i == 0)
    def _():
        kbf_ref[...] = k_ref[...].astype(BF16)
        vbf_ref[...] = v_ref[...].astype(BF16)
        kmp_ref[...] = jnp.zeros_like(kmp_ref)
        vmp_ref[...] = jnp.zeros_like(vmp_ref)
        kmp_ref[0:N_META, :] = km_ref[...].astype(BF16)
        vmp_ref[0:N_META, :] = vm_ref[...].astype(BF16)

    lanes = lambda p: slice(p * LANES, (p + 1) * LANES)
    qs = [jnp.concatenate(_lane_halves((q_ref[:, lanes(p)] * SB_QSCALE).astype(BF16)), axis=0)
          for p in range(hp)]
    rows = 2 * tq
    suffix = _suffix_matrix(tq)

    def step(kv_of_pair, suffix, acc, carry, mask):
        kv = [kv_of_pair(p) for p in range(hp)]
        z2 = [_dot_nt(qs[p], kv[p][0]) for p in range(hp)]
        sp = [_softplus2(z) for z in z2]
        if mask is not None:
            sp = [jnp.where(mask, s, 0.0) for s in sp]
        zs = [z2[p] - sp[p] for p in range(hp)]
        later = _dot(jnp.concatenate([s.astype(BF16) for s in sp], axis=0), suffix)
        a = [jnp.exp2(zs[p] - (later[p * rows:(p + 1) * rows] + carry[p])) for p in range(hp)]
        if mask is not None:
            a = [jnp.where(mask, x, 0.0) for x in a]
        d = [_dot(a[p].astype(BF16), kv[p][1]) for p in range(hp)]
        acc = d if acc is None else [acc[p] + d[p] for p in range(hp)]
        return acc, [carry[p] + jnp.sum(sp[p], axis=-1, keepdims=True) for p in range(hp)]

    diag = _iota((rows, tq), 1) < (_iota((rows, tq), 0) % tq)
    start = pl.multiple_of(i * tq, tq)
    acc, carry = step(lambda p: (kbf_ref[pl.ds(start, tq), lanes(p)], vbf_ref[pl.ds(start, tq), lanes(p)]),
                      suffix, None, [jnp.zeros((rows, 1), F32)] * hp, diag)

    def more_keys_matter(carry):
        m = functools.reduce(jnp.minimum, carry)
        return (jnp.min(m) <= F32_UNDERFLOW_LOG2).astype(jnp.int32)

    def cond(state):
        return (state[0] < i) & (state[1] > 0)

    def body(state):
        t = state[0]
        s = pl.multiple_of((i - 1 - t) * tq, tq)
        acc, carry = step(lambda p: (kbf_ref[pl.ds(s, tq), lanes(p)], vbf_ref[pl.ds(s, tq), lanes(p)]),
                          suffix, list(state[2:2 + hp]), list(state[2 + hp:]), None)
        return (t + 1, more_keys_matter(carry)) + tuple(acc) + tuple(carry)

    state = lax.while_loop(cond, body, (jnp.int32(0), more_keys_matter(carry)) + tuple(acc) + tuple(carry))
    acc, carry = list(state[2:2 + hp]), list(state[2 + hp:])

    def meta_block():
        meta_ok = _iota((rows, LANES), 1) < N_META
        return tuple(step(lambda p: (kmp_ref[:, lanes(p)], vmp_ref[:, lanes(p)]),
                          _suffix_matrix(LANES), acc, carry, meta_ok)[0])

    acc = lax.cond(state[1] > 0, meta_block, lambda: tuple(acc))
    for p in range(hp):
        o_ref[:, lanes(p)] = _merge_halves(acc[p][:tq], acc[p][tq:])


def _sb(proj, proj_meta, batch, tq, hp):
    m = proj.shape[0]
    seq = m // batch
    nq = seq // tq
    w = hp * LANES
    qcol = (SWA_HEADS + 2 * SWA_KV_HEADS) * HEAD_DIM // w
    steps = SB_HEADS * HEAD_DIM // w
    kcol, vcol = qcol + steps, qcol + 2 * steps
    return pl.pallas_call(
        functools.partial(_sb_kernel, tq=tq, hp=hp),
        grid=(batch, steps, nq),
        in_specs=[
            pl.BlockSpec((tq, w), lambda b, p, i: (b * nq + i, qcol + p)),
            pl.BlockSpec((seq, w), lambda b, p, i: (b, kcol + p)),
            pl.BlockSpec((seq, w), lambda b, p, i: (b, vcol + p)),
            pl.BlockSpec((N_META, w), lambda b, p, i: (b, kcol + p)),
            pl.BlockSpec((N_META, w), lambda b, p, i: (b, vcol + p)),
        ],
        out_specs=pl.BlockSpec((tq, w), lambda b, p, i: (b * nq + i, p)),
        out_shape=jax.ShapeDtypeStruct((m, SB_HEADS * HEAD_DIM), F32),
        scratch_shapes=[pltpu.VMEM((seq, w), BF16), pltpu.VMEM((seq, w), BF16),
                        pltpu.VMEM((LANES, w), BF16), pltpu.VMEM((LANES, w), BF16)],
        compiler_params=_cparams(("arbitrary", "arbitrary", "arbitrary")),
        name="stickbreak",
    )(proj, proj, proj, proj_meta, proj_meta)


def _meta_attn_kernel(sink_ref, p_ref, qg_ref, kg_ref, oa_ref, ob_ref):
    qw = SWA_HEADS * HEAD_DIM
    kvw = SWA_KV_HEADS * HEAD_DIM
    group = SWA_HEADS // SWA_KV_HEADS
    x = p_ref[...]
    xk = _pad_rows(x, LANES)
    qi = _iota((N_META, LANES), 0)
    kj = _iota((N_META, LANES), 1)

    qn = [(t * SCALE).astype(BF16) for t in _head_rmsnorm(x[:, 0:qw], qg_ref[...])]
    kn = jnp.concatenate(_head_rmsnorm(xk[:, qw:qw + kvw], kg_ref[...]), axis=1).astype(BF16)
    vb = xk[:, qw + kvw:qw + 2 * kvw].astype(BF16)
    delta_f = (qi - kj).astype(F32)
    causal = kj <= qi
    for pair in range(SWA_HEADS // 2):
        kv = (2 * pair) // group
        kdup, vdup = _dup_kv(kn, kv), _dup_kv(vb, kv)
        outs = []
        for e, qe in enumerate(_lane_halves(qn[pair])):
            h = 2 * pair + e
            outs.append(_swa_head(qe, kdup, vdup, _alibi_slope(h) * delta_f, causal, sink_ref[h]))
        oa_ref[:, pair * LANES:(pair + 1) * LANES] = _merge_halves(outs[0], outs[1])

    base = qw + 2 * kvw
    sbw = SB_HEADS * HEAD_DIM
    strict = kj < qi
    suffix = _suffix_matrix(LANES)
    zero_c = jnp.zeros((N_META, 1), F32)
    for pair in range(SB_HEADS // 2):
        sl = lambda src, part: src[:, base + part * sbw + pair * LANES: base + part * sbw + (pair + 1) * LANES]
        kb, vb2 = sl(xk, 1).astype(BF16), sl(xk, 2).astype(BF16)
        outs = [_sb_block(qe, kb, vb2, suffix, zero_c, strict)[0]
                for qe in _lane_halves((sl(x, 0) * SB_QSCALE).astype(BF16))]
        ob_ref[:, pair * LANES:(pair + 1) * LANES] = _merge_halves(outs[0], outs[1])


def _meta_attn(proj_meta, sinks, qg, kg, batch):
    pw = proj_meta.shape[1]
    qw = SWA_HEADS * HEAD_DIM
    kvw = SWA_KV_HEADS * HEAD_DIM
    sbw = SB_HEADS * HEAD_DIM
    return pl.pallas_call(
        _meta_attn_kernel,
        grid=(batch,),
        in_specs=[
            pl.BlockSpec(memory_space=pltpu.SMEM),
            pl.BlockSpec((N_META, pw), lambda b: (b, 0)),
            pl.BlockSpec((1, qw), lambda b: (0, 0)),
            pl.BlockSpec((1, kvw), lambda b: (0, 0)),
        ],
        out_specs=[pl.BlockSpec((N_META, qw), lambda b: (b, 0)),
                   pl.BlockSpec((N_META, sbw), lambda b: (b, 0))],
        out_shape=[jax.ShapeDtypeStruct((batch * N_META, qw), F32),
                   jax.ShapeDtypeStruct((batch * N_META, sbw), F32)],
        compiler_params=_cparams(("arbitrary",)),
        name="meta_attn",
    )(sinks, proj_meta, qg, kg)


def kernel(x, meta_tokens, attn_norm_g, w_in, q_norm_g, k_norm_g, attn_sinks,
           swa_out_g, sb_out_g, w_o, ffn_norm_g, w_gate, w_up, w_down):
    batch, seq, d = x.shape
    depth = w_in.shape[0]
    qw = SWA_HEADS * HEAD_DIM
    assert seq % 512 == 0 and meta_tokens.shape[0] == N_META

    tm = 512
    tmeta = batch * N_META
    tn_in = 1536
    tn_out = 1024
    tf = 512
    tq = 256
    hp = 2

    h = x.reshape(batch * seq, d)
    hm = jnp.broadcast_to(meta_tokens[None].astype(x.dtype), (batch, N_META, d)).reshape(tmeta, d)

    w_in_b, w_o_b = w_in.astype(BF16), w_o.astype(BF16)
    w_gate_b, w_up_b, w_down_b = w_gate.astype(BF16), w_up.astype(BF16), w_down.astype(BF16)

    for l in range(depth):
        g_attn = attn_norm_g[l].reshape(1, d)
        qg = jnp.tile(q_norm_g[l], SWA_HEADS).reshape(1, -1)
        kg = jnp.tile(k_norm_g[l], SWA_KV_HEADS).reshape(1, -1)
        sinks = attn_sinks[l].astype(F32)
        ga, gb = swa_out_g[l].reshape(1, -1), sb_out_g[l].reshape(1, -1)
        wa, wb = w_o_b[l, :qw], w_o_b[l, qw:]
        g_ffn = ffn_norm_g[l].reshape(1, d)

        proj = _norm_matmul(h, g_attn, w_in_b[l], tm, tn_in)
        proj_m = _norm_matmul(hm, g_attn, w_in_b[l], tmeta, tn_in)

        out_a = _swa(proj, proj_m, sinks, qg, kg, batch)
        out_b = _sb(proj, proj_m, batch, tq, hp)
        out_am, out_bm = _meta_attn(proj_m, sinks, qg, kg, batch)

        h = _outproj(out_a, out_b, ga, gb, wa, wb, h, tm, tn_out)
        hm = _outproj(out_am, out_bm, ga, gb, wa, wb, hm, tmeta, tn_out)

        h = _ffn(h, g_ffn, w_gate_b[l], w_up_b[l], w_down_b[l], tm, tf)
        hm = _ffn(hm, g_ffn, w_gate_b[l], w_up_b[l], w_down_b[l], tmeta, tf)

    return h.reshape(batch, seq, d)
```

```python
import functools

import jax
import jax.numpy as jnp
from jax import lax
from jax.experimental import pallas as pl
from jax.experimental.pallas import tpu as pltpu

F32 = jnp.float32
BF16 = jnp.bfloat16

N_META = 16
BLOCK = 128
HEAD_DIM = 64
LANES = 128
SWA_HEADS = 16
SWA_KV_HEADS = 4
SB_HEADS = 16
EPS = 1e-6
NEG = -1e30
SCALE = HEAD_DIM ** -0.5
LOG2E = 1.4426950408889634
SB_QSCALE = SCALE * LOG2E
F32_UNDERFLOW_LOG2 = 160.0
VMEM_LIMIT = 56 * 1024 * 1024


def _cparams(sem):
    return pltpu.CompilerParams(dimension_semantics=sem, vmem_limit_bytes=VMEM_LIMIT)


def _dot(a, b):
    return jnp.dot(a, b, preferred_element_type=F32)


def _dot_nt(a, b):
    return lax.dot_general(a, b, (((1,), (1,)), ((), ())), preferred_element_type=F32)


def _split_bf16(x):
    hi = x.astype(BF16)
    lo = (x - hi.astype(F32)).astype(BF16)
    return hi, lo


def _iota(shape, dim):
    return lax.broadcasted_iota(jnp.int32, shape, dim)


def _norm_mm_kernel(x_ref, g_ref, w_ref, o_ref, xn_ref):
    @pl.when(pl.program_id(1) == 0)
    def _():
        x = x_ref[...]
        ms = jnp.mean(x * x, axis=-1, keepdims=True)
        xn_ref[...] = ((x * lax.rsqrt(ms + EPS)) * g_ref[...]).astype(BF16)

    o_ref[...] = _dot(xn_ref[...], w_ref[...])


def _norm_matmul(x, g, w, tm, tn):
    m, d = x.shape
    n = w.shape[1]
    return pl.pallas_call(
        _norm_mm_kernel,
        grid=(m // tm, n // tn),
        in_specs=[
            pl.BlockSpec((tm, d), lambda i, j: (i, 0)),
            pl.BlockSpec((1, d), lambda i, j: (0, 0)),
            pl.BlockSpec((d, tn), lambda i, j: (0, j)),
        ],
        out_specs=pl.BlockSpec((tm, tn), lambda i, j: (i, j)),
        out_shape=jax.ShapeDtypeStruct((m, n), F32),
        scratch_shapes=[pltpu.VMEM((tm, d), BF16)],
        compiler_params=_cparams(("arbitrary", "arbitrary")),
        name="norm_inproj",
    )(x, g, w)


def _outproj_kernel(a_ref, b_ref, ga_ref, gb_ref, wa_ref, wb_ref, h_ref, o_ref, an_ref, bn_ref):
    @pl.when(pl.program_id(1) == 0)
    def _():
        for src, g, dst in ((a_ref, ga_ref, an_ref), (b_ref, gb_ref, bn_ref)):
            x = src[...]
            ms = jnp.mean(x * x, axis=-1, keepdims=True)
            dst[...] = ((x * lax.rsqrt(ms + EPS)) * g[...]).astype(BF16)

    o_ref[...] = h_ref[...] + (_dot(an_ref[...], wa_ref[...]) + _dot(bn_ref[...], wb_ref[...]))


def _outproj(a, b, ga, gb, wa, wb, h, tm, tn):
    m, wdt = a.shape
    d = h.shape[1]
    return pl.pallas_call(
        _outproj_kernel,
        grid=(m // tm, d // tn),
        in_specs=[
            pl.BlockSpec((tm, wdt), lambda i, j: (i, 0)),
            pl.BlockSpec((tm, wdt), lambda i, j: (i, 0)),
            pl.BlockSpec((1, wdt), lambda i, j: (0, 0)),
            pl.BlockSpec((1, wdt), lambda i, j: (0, 0)),
            pl.BlockSpec((wdt, tn), lambda i, j: (0, j)),
            pl.BlockSpec((wdt, tn), lambda i, j: (0, j)),
            pl.BlockSpec((tm, tn), lambda i, j: (i, j)),
        ],
        out_specs=pl.BlockSpec((tm, tn), lambda i, j: (i, j)),
        out_shape=jax.ShapeDtypeStruct((m, d), F32),
        scratch_shapes=[pltpu.VMEM((tm, wdt), BF16), pltpu.VMEM((tm, wdt), BF16)],
        compiler_params=_cparams(("arbitrary", "arbitrary")),
        name="outproj",
    )(a, b, ga, gb, wa, wb, h)


def _ffn_kernel(x_ref, g_ref, wg_ref, wu_ref, wd_ref, o_ref, xn_ref, acc_ref):
    j = pl.program_id(1)

    @pl.when(j == 0)
    def _():
        x = x_ref[...]
        ms = jnp.mean(x * x, axis=-1, keepdims=True)
        xn_ref[...] = ((x * lax.rsqrt(ms + EPS)) * g_ref[...]).astype(BF16)
        acc_ref[...] = jnp.zeros_like(acc_ref)

    xn = xn_ref[...]
    gate = _dot(xn, wg_ref[...])
    up = _dot(xn, wu_ref[...])
    act = (gate / (1.0 + jnp.exp(-gate))) * up
    acc_ref[...] += _dot(act.astype(BF16), wd_ref[...])

    @pl.when(j == pl.num_programs(1) - 1)
    def _():
        o_ref[...] = x_ref[...] + acc_ref[...]


def _ffn(x, g, wg, wu, wd, tm, tf):
    m, d = x.shape
    f = wg.shape[1]
    return pl.pallas_call(
        _ffn_kernel,
        grid=(m // tm, f // tf),
        in_specs=[
            pl.BlockSpec((tm, d), lambda i, j: (i, 0)),
            pl.BlockSpec((1, d), lambda i, j: (0, 0)),
            pl.BlockSpec((d, tf), lambda i, j: (0, j)),
            pl.BlockSpec((d, tf), lambda i, j: (0, j)),
            pl.BlockSpec((tf, d), lambda i, j: (j, 0)),
        ],
        out_specs=pl.BlockSpec((tm, d), lambda i, j: (i, 0)),
        out_shape=jax.ShapeDtypeStruct((m, d), F32),
        scratch_shapes=[pltpu.VMEM((tm, d), BF16), pltpu.VMEM((tm, d), F32)],
        compiler_params=_cparams(("arbitrary", "arbitrary")),
        name="ffn",
    )(x, g, wg, wu, wd)


def _head_rmsnorm(x, g):
    r, c = x.shape
    same_head = (_iota((LANES, LANES), 0) // HEAD_DIM) == (_iota((LANES, LANES), 1) // HEAD_DIM)
    bd = jnp.where(same_head, 1.0, 0.0).astype(BF16)
    outs = []
    for t in range(c // LANES):
        xb = x[:, t * LANES:(t + 1) * LANES]
        hi, lo = _split_bf16(xb * xb)
        ss = _dot(hi, bd) + _dot(lo, bd)
        outs.append((xb * lax.rsqrt(ss * (1.0 / HEAD_DIM) + EPS)) * g[:, t * LANES:(t + 1) * LANES])
    return outs


def _half_select(half):
    src = _iota((LANES, LANES), 0)
    dst = _iota((LANES, LANES), 1)
    return jnp.where(src == (dst % HEAD_DIM) + HEAD_DIM * half, 1.0, 0.0).astype(BF16)


def _lane_halves(x):
    lo_half = _iota(x.shape, 1) < HEAD_DIM
    zero = jnp.zeros_like(x)
    return jnp.where(lo_half, x, zero), jnp.where(lo_half, zero, x)


def _merge_halves(lo, hi):
    return jnp.where(_iota(lo.shape, 1) < HEAD_DIM, lo, hi)


def _alibi_slope(h):
    return 2.0 ** (-8.0 * (h + 1) / SWA_HEADS)


def _softplus2(z2):
    neg_abs = lax.bitcast_convert_type(
        lax.bitcast_convert_type(z2, jnp.uint32) | jnp.uint32(0x80000000), F32)
    return jnp.maximum(z2, 0.0) + jnp.log2(1.0 + jnp.exp2(neg_abs))


def _suffix_matrix(n):
    return jnp.where(_iota((n, n), 0) > _iota((n, n), 1), 1.0, 0.0).astype(BF16)


def _sb_block(qe, kblk, vblk, suffix, carry, mask):
    z2 = _dot_nt(qe, kblk)
    sp = _softplus2(z2)
    if mask is not None:
        sp = jnp.where(mask, sp, 0.0)
    later = _dot(sp.astype(BF16), suffix)
    a = jnp.exp2((z2 - sp) - (later + carry))
    if mask is not None:
        a = jnp.where(mask, a, 0.0)
    contrib = _dot(a.astype(BF16), vblk)
    return contrib, carry + jnp.sum(sp, axis=-1, keepdims=True)


def _swa_head(qe, kdup, vdup, bias, mask, sink):
    s = jnp.where(mask, _dot_nt(qe, kdup) - bias, NEG)
    mx = jnp.maximum(jnp.max(s, axis=-1, keepdims=True), sink)
    e = jnp.exp(s - mx)
    den = jnp.sum(e, axis=-1, keepdims=True) + jnp.exp(sink - mx)
    return _dot(e.astype(BF16), vdup) / den


def _pad_rows(x, rows):
    return jnp.concatenate([x, jnp.zeros((rows - x.shape[0], x.shape[1]), x.dtype)], axis=0)


def _dup_kv(x_bf, kv_head):
    t = kv_head // 2
    return _dot(x_bf[:, t * LANES:(t + 1) * LANES], _half_select(kv_head % 2)).astype(BF16)


def _swa_kernel(sink_ref, q_ref, kc_ref, kp_ref, vc_ref, vp_ref, km_ref, vm_ref, qg_ref, kg_ref, o_ref):
    n = pl.program_id(1)
    group = SWA_HEADS // SWA_KV_HEADS
    rows = group * BLOCK
    qn = [(x * (SCALE * LOG2E)).astype(BF16) for x in _head_rmsnorm(q_ref[...], qg_ref[...])]
    kcat = jnp.concatenate([_pad_rows(km_ref[...], BLOCK), kp_ref[...], kc_ref[...]], axis=0)
    kn = jnp.concatenate(_head_rmsnorm(kcat, kg_ref[...]), axis=1).astype(BF16)
    vcat = jnp.concatenate([_pad_rows(vm_ref[...], BLOCK), vp_ref[...], vc_ref[...]], axis=0).astype(BF16)

    qi = _iota((rows, BLOCK), 0) % BLOCK
    j = _iota((rows, BLOCK), 1)
    in_cur = j <= qi
    dist_band = jnp.where(in_cur, qi - j, qi - j + BLOCK).astype(F32)
    band_ok = in_cur | (n > 0)
    dist_meta = (N_META + n * BLOCK + qi - j).astype(F32)
    meta_ok = j < N_META
    head_in_group = _iota((rows, 1), 0) // BLOCK

    def per_head_column(values):
        col = jnp.full((rows, 1), values[group - 1], F32)
        for r in range(group - 2, -1, -1):
            col = jnp.where(head_in_group == r, values[r], col)
        return col

    kvs = [(_dup_kv(kn, g), _dup_kv(vcat, g)) for g in range(SWA_KV_HEADS)]
    logits = [_dot_nt(jnp.concatenate(_lane_halves(qn[2 * g]) + _lane_halves(qn[2 * g + 1]), axis=0),
                      kvs[g][0]) for g in range(SWA_KV_HEADS)]
    weights, dens = [], []
    for g in range(SWA_KV_HEADS):
        heads = range(group * g, group * (g + 1))
        slope = per_head_column([_alibi_slope(h) * LOG2E for h in heads])
        sink = per_head_column([sink_ref[h] * LOG2E for h in heads])
        s = logits[g]
        s_band = jnp.where(in_cur, s[:, 2 * BLOCK:], s[:, BLOCK:2 * BLOCK]) - slope * dist_band
        s_band = jnp.where(band_ok, s_band, NEG)
        s_meta = jnp.where(meta_ok, s[:, :BLOCK] - slope * dist_meta, NEG)
        mx = jnp.maximum(jnp.maximum(jnp.max(s_band, axis=-1, keepdims=True),
                                     jnp.max(s_meta, axis=-1, keepdims=True)), sink)
        e_band = jnp.exp2(s_band - mx)
        e_meta = jnp.exp2(s_meta - mx)
        dens.append(jnp.sum(e_band, axis=-1, keepdims=True) + jnp.sum(e_meta, axis=-1, keepdims=True)
                    + jnp.exp2(sink - mx))
        weights.append(jnp.concatenate(
            [e_meta, jnp.where(in_cur, 0.0, e_band), jnp.where(in_cur, e_band, 0.0)], axis=1).astype(BF16))
    for g in range(SWA_KV_HEADS):
        o = _dot(weights[g], kvs[g][1]) / dens[g]
        for t in range(2):
            o_ref[:, (2 * g + t) * LANES:(2 * g + t + 1) * LANES] = _merge_halves(
                o[2 * t * BLOCK:(2 * t + 1) * BLOCK], o[(2 * t + 1) * BLOCK:(2 * t + 2) * BLOCK])


def _swa(proj, proj_meta, sinks, qg, kg, batch):
    m = proj.shape[0]
    nb = m // batch // BLOCK
    qw = SWA_HEADS * HEAD_DIM
    kvw = SWA_KV_HEADS * HEAD_DIM
    kcol, vcol = qw // kvw, qw // kvw + 1
    row = lambda b, n: b * nb + n
    return pl.pallas_call(
        _swa_kernel,
        grid=(batch, nb),
        in_specs=[
            pl.BlockSpec(memory_space=pltpu.SMEM),
            pl.BlockSpec((BLOCK, qw), lambda b, n: (row(b, n), 0)),
            pl.BlockSpec((BLOCK, kvw), lambda b, n: (row(b, n), kcol)),
            pl.BlockSpec((BLOCK, kvw), lambda b, n: (row(b, jnp.maximum(n - 1, 0)), kcol)),
            pl.BlockSpec((BLOCK, kvw), lambda b, n: (row(b, n), vcol)),
            pl.BlockSpec((BLOCK, kvw), lambda b, n: (row(b, jnp.maximum(n - 1, 0)), vcol)),
            pl.BlockSpec((N_META, kvw), lambda b, n: (b, kcol)),
            pl.BlockSpec((N_META, kvw), lambda b, n: (b, vcol)),
            pl.BlockSpec((1, qw), lambda b, n: (0, 0)),
            pl.BlockSpec((1, kvw), lambda b, n: (0, 0)),
        ],
        out_specs=pl.BlockSpec((BLOCK, qw), lambda b, n: (row(b, n), 0)),
        out_shape=jax.ShapeDtypeStruct((m, qw), F32),
        compiler_params=_cparams(("arbitrary", "arbitrary")),
        name="swa",
    )(sinks, proj, proj, proj, proj, proj, proj_meta, proj_meta, qg, kg)


def _sb_kernel(q_ref, k_ref, v_ref, km_ref, vm_ref, o_ref, kbf_ref, vbf_ref, kmp_ref, vmp_ref, *, tq, hp):
    i = pl.program_id(2)

    @pl.when(i == 0)
    def _():
        kbf_ref[...] = k_ref[...].astype(BF16)
        vbf_ref[...] = v_ref[...].astype(BF16)
        kmp_ref[...] = jnp.zeros_like(kmp_ref)
        vmp_ref[...] = jnp.zeros_like(vmp_ref)
        kmp_ref[0:N_META, :] = km_ref[...].astype(BF16)
        vmp_ref[0:N_META, :] = vm_ref[...].astype(BF16)

    lanes = lambda p: slice(p * LANES, (p + 1) * LANES)
    qs = [jnp.concatenate(_lane_halves((q_ref[:, lanes(p)] * SB_QSCALE).astype(BF16)), axis=0)
          for p in range(hp)]
    rows = 2 * tq
    suffix = _suffix_matrix(tq)

    def step(kv_of_pair, suffix, acc, carry, mask):
        kv = [kv_of_pair(p) for p in range(hp)]
        z2 = [_dot_nt(qs[p], kv[p][0]) for p in range(hp)]
        sp = [_softplus2(z) for z in z2]
        if mask is not None:
            sp = [jnp.where(mask, s, 0.0) for s in sp]
        zs = [z2[p] - sp[p] for p in range(hp)]
        later = _dot(jnp.concatenate([s.astype(BF16) for s in sp], axis=0), suffix)
        a = [jnp.exp2(zs[p] - (later[p * rows:(p + 1) * rows] + carry[p])) for p in range(hp)]
        if mask is not None:
            a = [jnp.where(mask, x, 0.0) for x in a]
        d = [_dot(a[p].astype(BF16), kv[p][1]) for p in range(hp)]
        acc = d if acc is None else [acc[p] + d[p] for p in range(hp)]
        return acc, [carry[p] + jnp.sum(sp[p], axis=-1, keepdims=True) for p in range(hp)]

    diag = _iota((rows, tq), 1) < (_iota((rows, tq), 0) % tq)
    start = pl.multiple_of(i * tq, tq)
    acc, carry = step(lambda p: (kbf_ref[pl.ds(start, tq), lanes(p)], vbf_ref[pl.ds(start, tq), lanes(p)]),
                      suffix, None, [jnp.zeros((rows, 1), F32)] * hp, diag)

    def more_keys_matter(carry):
        m = functools.reduce(jnp.minimum, carry)
        return (jnp.min(m) <= F32_UNDERFLOW_LOG2).astype(jnp.int32)

    def cond(state):
        return (state[0] < i) & (state[1] > 0)

    def body(state):
        t = state[0]
        s = pl.multiple_of((i - 1 - t) * tq, tq)
        acc, carry = step(lambda p: (kbf_ref[pl.ds(s, tq), lanes(p)], vbf_ref[pl.ds(s, tq), lanes(p)]),
                          suffix, list(state[2:2 + hp]), list(state[2 + hp:]), None)
        return (t + 1, more_keys_matter(carry)) + tuple(acc) + tuple(carry)

    state = lax.while_loop(cond, body, (jnp.int32(0), more_keys_matter(carry)) + tuple(acc) + tuple(carry))
    acc, carry = list(state[2:2 + hp]), list(state[2 + hp:])

    def meta_block():
        meta_ok = _iota((rows, LANES), 1) < N_META
        return tuple(step(lambda p: (kmp_ref[:, lanes(p)], vmp_ref[:, lanes(p)]),
                          _suffix_matrix(LANES), acc, carry, meta_ok)[0])

    acc = lax.cond(state[1] > 0, meta_block, lambda: tuple(acc))
    for p in range(hp):
        o_ref[:, lanes(p)] = _merge_halves(acc[p][:tq], acc[p][tq:])


def _sb(proj, proj_meta, batch, tq, hp):
    m = proj.shape[0]
    seq = m // batch
    nq = seq // tq
    w = hp * LANES
    qcol = (SWA_HEADS + 2 * SWA_KV_HEADS) * HEAD_DIM // w
    steps = SB_HEADS * HEAD_DIM // w
    kcol, vcol = qcol + steps, qcol + 2 * steps
    return pl.pallas_call(
        functools.partial(_sb_kernel, tq=tq, hp=hp),
        grid=(batch, steps, nq),
        in_specs=[
            pl.BlockSpec((tq, w), lambda b, p, i: (b * nq + i, qcol + p)),
            pl.BlockSpec((seq, w), lambda b, p, i: (b, kcol + p)),
            pl.BlockSpec((seq, w), lambda b, p, i: (b, vcol + p)),
            pl.BlockSpec((N_META, w), lambda b, p, i: (b, kcol + p)),
            pl.BlockSpec((N_META, w), lambda b, p, i: (b, vcol + p)),
        ],
        out_specs=pl.BlockSpec((tq, w), lambda b, p, i: (b * nq + i, p)),
        out_shape=jax.ShapeDtypeStruct((m, SB_HEADS * HEAD_DIM), F32),
        scratch_shapes=[pltpu.VMEM((seq, w), BF16), pltpu.VMEM((seq, w), BF16),
                        pltpu.VMEM((LANES, w), BF16), pltpu.VMEM((LANES, w), BF16)],
        compiler_params=_cparams(("arbitrary", "arbitrary", "arbitrary")),
        name="stickbreak",
    )(proj, proj, proj, proj_meta, proj_meta)


def _meta_attn_kernel(sink_ref, p_ref, qg_ref, kg_ref, oa_ref, ob_ref):
    qw = SWA_HEADS * HEAD_DIM
    kvw = SWA_KV_HEADS * HEAD_DIM
    group = SWA_HEADS // SWA_KV_HEADS
    x = p_ref[...]
    xk = _pad_rows(x, LANES)
    qi = _iota((N_META, LANES), 0)
    kj = _iota((N_META, LANES), 1)

    qn = [(t * SCALE).astype(BF16) for t in _head_rmsnorm(x[:, 0:qw], qg_ref[...])]
    kn = jnp.concatenate(_head_rmsnorm(xk[:, qw:qw + kvw], kg_ref[...]), axis=1).astype(BF16)
    vb = xk[:, qw + kvw:qw + 2 * kvw].astype(BF16)
    delta_f = (qi - kj).astype(F32)
    causal = kj <= qi
    for pair in range(SWA_HEADS // 2):
        kv = (2 * pair) // group
        kdup, vdup = _dup_kv(kn, kv), _dup_kv(vb, kv)
        outs = []
        for e, qe in enumerate(_lane_halves(qn[pair])):
            h = 2 * pair + e
            outs.append(_swa_head(qe, kdup, vdup, _alibi_slope(h) * delta_f, causal, sink_ref[h]))
        oa_ref[:, pair * LANES:(pair + 1) * LANES] = _merge_halves(outs[0], outs[1])

    base = qw + 2 * kvw
    sbw = SB_HEADS * HEAD_DIM
    strict = kj < qi
    suffix = _suffix_matrix(LANES)
    zero_c = jnp.zeros((N_META, 1), F32)
    for pair in range(SB_HEADS // 2):
        sl = lambda src, part: src[:, base + part * sbw + pair * LANES: base + part * sbw + (pair + 1) * LANES]
        kb, vb2 = sl(xk, 1).astype(BF16), sl(xk, 2).astype(BF16)
        outs = [_sb_block(qe, kb, vb2, suffix, zero_c, strict)[0]
                for qe in _lane_halves((sl(x, 0) * SB_QSCALE).astype(BF16))]
        ob_ref[:, pair * LANES:(pair + 1) * LANES] = _merge_halves(outs[0], outs[1])


def _meta_attn(proj_meta, sinks, qg, kg, batch):
    pw = proj_meta.shape[1]
    qw = SWA_HEADS * HEAD_DIM
    kvw = SWA_KV_HEADS * HEAD_DIM
    sbw = SB_HEADS * HEAD_DIM
    return pl.pallas_call(
        _meta_attn_kernel,
        grid=(batch,),
        in_specs=[
            pl.BlockSpec(memory_space=pltpu.SMEM),
            pl.BlockSpec((N_META, pw), lambda b: (b, 0)),
            pl.BlockSpec((1, qw), lambda b: (0, 0)),
            pl.BlockSpec((1, kvw), lambda b: (0, 0)),
        ],
        out_specs=[pl.BlockSpec((N_META, qw), lambda b: (b, 0)),
                   pl.BlockSpec((N_META, sbw), lambda b: (b, 0))],
        out_shape=[jax.ShapeDtypeStruct((batch * N_META, qw), F32),
                   jax.ShapeDtypeStruct((batch * N_META, sbw), F32)],
        compiler_params=_cparams(("arbitrary",)),
        name="meta_attn",
    )(sinks, proj_meta, qg, kg)


def kernel(x, meta_tokens, attn_norm_g, w_in, q_norm_g, k_norm_g, attn_sinks,
           swa_out_g, sb_out_g, w_o, ffn_norm_g, w_gate, w_up, w_down):
    batch, seq, d = x.shape
    depth = w_in.shape[0]
    qw = SWA_HEADS * HEAD_DIM
    assert seq % 512 == 0 and meta_tokens.shape[0] == N_META

    tm = 512
    tmeta = batch * N_META
    tn_in = 1536
    tn_out = 1024
    tf = 512
    tq = 256
    hp = 2

    h = x.reshape(batch * seq, d)
    hm = jnp.broadcast_to(meta_tokens[None].astype(x.dtype), (batch, N_META, d)).reshape(tmeta, d)

    w_in_b, w_o_b = w_in.astype(BF16), w_o.astype(BF16)
    w_gate_b, w_up_b, w_down_b = w_gate.astype(BF16), w_up.astype(BF16), w_down.astype(BF16)

    for l in range(depth):
        g_attn = attn_norm_g[l].reshape(1, d)
        qg = jnp.tile(q_norm_g[l], SWA_HEADS).reshape(1, -1)
        kg = jnp.tile(k_norm_g[l], SWA_KV_HEADS).reshape(1, -1)
        sinks = attn_sinks[l].astype(F32)
        ga, gb = swa_out_g[l].reshape(1, -1), sb_out_g[l].reshape(1, -1)
        wa, wb = w_o_b[l, :qw], w_o_b[l, qw:]
        g_ffn = ffn_norm_g[l].reshape(1, d)

        proj = _norm_matmul(h, g_attn, w_in_b[l], tm, tn_in)
        proj_m = _norm_matmul(hm, g_attn, w_in_b[l], tmeta, tn_in)

        out_a = _swa(proj, proj_m, sinks, qg, kg, batch)
        out_b = _sb(proj, proj_m, batch, tq, hp)
        out_am, out_bm = _meta_attn(proj_m, sinks, qg, kg, batch)

        h = _outproj(out_a, out_b, ga, gb, wa, wb, h, tm, tn_out)
        hm = _outproj(out_am, out_bm, ga, gb, wa, wb, hm, tmeta, tn_out)

        h = _ffn(h, g_ffn, w_gate_b[l], w_up_b[l], w_down_b[l], tm, tf)
        hm = _ffn(hm, g_ffn, w_gate_b[l], w_up_b[l], w_down_b[l], tmeta, tf)

    return h.reshape(batch, seq, d)
```

```python
import functools

import jax
import jax.numpy as jnp
from jax import lax
from jax.experimental import pallas as pl
from jax.experimental.pallas import tpu as pltpu

F32 = jnp.float32
BF16 = jnp.bfloat16

N_META = 16
BLOCK = 128
HEAD_DIM = 64
LANES = 128
SWA_HEADS = 16
SWA_KV_HEADS = 4
SB_HEADS = 16
EPS = 1e-6
NEG = -1e30
SCALE = HEAD_DIM ** -0.5
LOG2E = 1.4426950408889634
SB_QSCALE = SCALE * LOG2E
F32_UNDERFLOW_LOG2 = 160.0
VMEM_LIMIT = 56 * 1024 * 1024


def _cparams(sem):
    return pltpu.CompilerParams(dimension_semantics=sem, vmem_limit_bytes=VMEM_LIMIT)


def _dot(a, b):
    return jnp.dot(a, b, preferred_element_type=F32)


def _dot_nt(a, b):
    return lax.dot_general(a, b, (((1,), (1,)), ((), ())), preferred_element_type=F32)


def _split_bf16(x):
    hi = x.astype(BF16)
    lo = (x - hi.astype(F32)).astype(BF16)
    return hi, lo


def _iota(shape, dim):
    return lax.broadcasted_iota(jnp.int32, shape, dim)


def _norm_mm_kernel(x_ref, g_ref, w_ref, o_ref, xn_ref):
    @pl.when(pl.program_id(1) == 0)
    def _():
        x = x_ref[...]
        ms = jnp.mean(x * x, axis=-1, keepdims=True)
        xn_ref[...] = ((x * lax.rsqrt(ms + EPS)) * g_ref[...]).astype(BF16)

    o_ref[...] = _dot(xn_ref[...], w_ref[...])


def _norm_matmul(x, g, w, layer, tm, tn):
    m, d = x.shape
    n = w.shape[2]
    return pl.pallas_call(
        _norm_mm_kernel,
        grid=(m // tm, n // tn),
        in_specs=[
            pl.BlockSpec((tm, d), lambda i, j: (i, 0)),
            pl.BlockSpec((1, d), lambda i, j: (0, 0)),
            pl.BlockSpec((None, d, tn), lambda i, j: (layer, 0, j)),
        ],
        out_specs=pl.BlockSpec((tm, tn), lambda i, j: (i, j)),
        out_shape=jax.ShapeDtypeStruct((m, n), F32),
        scratch_shapes=[pltpu.VMEM((tm, d), BF16)],
        compiler_params=_cparams(("arbitrary", "arbitrary")),
        name="norm_inproj",
    )(x, g, w)


def _outproj_kernel(a_ref, b_ref, ga_ref, gb_ref, wa_ref, wb_ref, h_ref, o_ref, an_ref, bn_ref):
    @pl.when(pl.program_id(1) == 0)
    def _():
        for src, g, dst in ((a_ref, ga_ref, an_ref), (b_ref, gb_ref, bn_ref)):
            x = src[...]
            ms = jnp.mean(x * x, axis=-1, keepdims=True)
            dst[...] = ((x * lax.rsqrt(ms + EPS)) * g[...]).astype(BF16)

    o_ref[...] = h_ref[...] + (_dot(an_ref[...], wa_ref[...]) + _dot(bn_ref[...], wb_ref[...]))


def _outproj(a, b, ga, gb, w, layer, h, tm, tn):
    m, wdt = a.shape
    d = h.shape[1]
    return pl.pallas_call(
        _outproj_kernel,
        grid=(m // tm, d // tn),
        in_specs=[
            pl.BlockSpec((tm, wdt), lambda i, j: (i, 0)),
            pl.BlockSpec((tm, wdt), lambda i, j: (i, 0)),
            pl.BlockSpec((1, wdt), lambda i, j: (0, 0)),
            pl.BlockSpec((1, wdt), lambda i, j: (0, 0)),
            pl.BlockSpec((None, wdt, tn), lambda i, j: (layer, 0, j)),
            pl.BlockSpec((None, wdt, tn), lambda i, j: (layer, 1, j)),
            pl.BlockSpec((tm, tn), lambda i, j: (i, j)),
        ],
        out_specs=pl.BlockSpec((tm, tn), lambda i, j: (i, j)),
        out_shape=jax.ShapeDtypeStruct((m, d), F32),
        scratch_shapes=[pltpu.VMEM((tm, wdt), BF16), pltpu.VMEM((tm, wdt), BF16)],
        compiler_params=_cparams(("arbitrary", "arbitrary")),
        name="outproj",
    )(a, b, ga, gb, w, w, h)


def _ffn_kernel(x_ref, g_ref, wg_ref, wu_ref, wd_ref, res_ref, o_ref, xn_ref, act_ref, *, n_up, tf):
    j = pl.program_id(1)

    @pl.when(j == 0)
    def _():
        x = x_ref[...]
        ms = jnp.mean(x * x, axis=-1, keepdims=True)
        xn_ref[...] = ((x * lax.rsqrt(ms + EPS)) * g_ref[...]).astype(BF16)

    @pl.when(j < n_up)
    def _():
        xn = xn_ref[...]
        gate = _dot(xn, wg_ref[...])
        up = _dot(xn, wu_ref[...])
        act_ref[j] = ((gate / (1.0 + jnp.exp(-gate))) * up).astype(BF16)

    @pl.when(j >= n_up)
    def _():
        acc = res_ref[...]
        for t in range(n_up):
            acc = acc + _dot(act_ref[t], wd_ref[t * tf:(t + 1) * tf, :])
        o_ref[...] = acc


def _ffn(x, g, wg, wu, wd, layer, tm, tf, tn):
    m, d = x.shape
    f = wg.shape[2]
    n_up, n_down = f // tf, d // tn
    up_blk = lambda j: jnp.minimum(j, n_up - 1)
    down_blk = lambda j: jnp.maximum(j - n_up, 0)
    return pl.pallas_call(
        functools.partial(_ffn_kernel, n_up=n_up, tf=tf),
        grid=(m // tm, n_up + n_down),
        in_specs=[
            pl.BlockSpec((tm, d), lambda i, j: (i, 0)),
            pl.BlockSpec((1, d), lambda i, j: (0, 0)),
            pl.BlockSpec((None, d, tf), lambda i, j: (layer, 0, up_blk(j))),
            pl.BlockSpec((None, d, tf), lambda i, j: (layer, 0, up_blk(j))),
            pl.BlockSpec((None, f, tn), lambda i, j: (layer, 0, down_blk(j))),
            pl.BlockSpec((tm, tn), lambda i, j: (i, down_blk(j))),
        ],
        out_specs=pl.BlockSpec((tm, tn), lambda i, j: (i, down_blk(j))),
        out_shape=jax.ShapeDtypeStruct((m, d), F32),
        scratch_shapes=[pltpu.VMEM((tm, d), BF16), pltpu.VMEM((n_up, tm, tf), BF16)],
        compiler_params=_cparams(("arbitrary", "arbitrary")),
        name="ffn",
    )(x, g, wg, wu, wd, x)


def _head_rmsnorm(x, g):
    r, c = x.shape
    same_head = (_iota((LANES, LANES), 0) // HEAD_DIM) == (_iota((LANES, LANES), 1) // HEAD_DIM)
    bd = jnp.where(same_head, 1.0, 0.0).astype(BF16)
    outs = []
    for t in range(c // LANES):
        xb = x[:, t * LANES:(t + 1) * LANES]
        hi, lo = _split_bf16(xb * xb)
        ss = _dot(hi, bd) + _dot(lo, bd)
        outs.append((xb * lax.rsqrt(ss * (1.0 / HEAD_DIM) + EPS)) * g[:, t * LANES:(t + 1) * LANES])
    return outs


def _half_select(half):
    src = _iota((LANES, LANES), 0)
    dst = _iota((LANES, LANES), 1)
    return jnp.where(src == (dst % HEAD_DIM) + HEAD_DIM * half, 1.0, 0.0).astype(BF16)


def _lane_halves(x):
    lo_half = _iota(x.shape, 1) < HEAD_DIM
    zero = jnp.zeros_like(x)
    return jnp.where(lo_half, x, zero), jnp.where(lo_half, zero, x)


def _merge_halves(lo, hi):
    return jnp.where(_iota(lo.shape, 1) < HEAD_DIM, lo, hi)


def _alibi_slope(h):
    return 2.0 ** (-8.0 * (h + 1) / SWA_HEADS)


def _softplus2(z2):
    neg_abs = lax.bitcast_convert_type(
        lax.bitcast_convert_type(z2, jnp.uint32) | jnp.uint32(0x80000000), F32)
    return jnp.maximum(z2, 0.0) + jnp.log2(1.0 + jnp.exp2(neg_abs))


def _suffix_matrix(n):
    return jnp.where(_iota((n, n), 0) > _iota((n, n), 1), 1.0, 0.0).astype(BF16)


def _sb_block(qe, kblk, vblk, suffix, carry, mask):
    z2 = _dot_nt(qe, kblk)
    sp = _softplus2(z2)
    if mask is not None:
        sp = jnp.where(mask, sp, 0.0)
    later = _dot(sp.astype(BF16), suffix)
    a = jnp.exp2((z2 - sp) - (later + carry))
    if mask is not None:
        a = jnp.where(mask, a, 0.0)
    contrib = _dot(a.astype(BF16), vblk)
    return contrib, carry + jnp.sum(sp, axis=-1, keepdims=True)


def _swa_head(qe, kdup, vdup, bias, mask, sink):
    s = jnp.where(mask, _dot_nt(qe, kdup) - bias, NEG)
    mx = jnp.maximum(jnp.max(s, axis=-1, keepdims=True), sink)
    e = jnp.exp(s - mx)
    den = jnp.sum(e, axis=-1, keepdims=True) + jnp.exp(sink - mx)
    return _dot(e.astype(BF16), vdup) / den


def _pad_rows(x, rows):
    return jnp.concatenate([x, jnp.zeros((rows - x.shape[0], x.shape[1]), x.dtype)], axis=0)


def _dup_kv(x_bf, kv_head):
    t = kv_head // 2
    return _dot(x_bf[:, t * LANES:(t + 1) * LANES], _half_select(kv_head % 2)).astype(BF16)


def _swa_kernel(sink_ref, q_ref, kc_ref, kp_ref, vc_ref, vp_ref, km_ref, vm_ref, qg_ref, kg_ref, o_ref):
    n = pl.program_id(1)
    group = SWA_HEADS // SWA_KV_HEADS
    rows = group * BLOCK
    qn = [(x * (SCALE * LOG2E)).astype(BF16) for x in _head_rmsnorm(q_ref[...], qg_ref[...])]
    kcat = jnp.concatenate([_pad_rows(km_ref[...], BLOCK), kp_ref[...], kc_ref[...]], axis=0)
    kn = jnp.concatenate(_head_rmsnorm(kcat, kg_ref[...]), axis=1).astype(BF16)
    vcat = jnp.concatenate([_pad_rows(vm_ref[...], BLOCK), vp_ref[...], vc_ref[...]], axis=0).astype(BF16)

    qi = _iota((rows, BLOCK), 0) % BLOCK
    j = _iota((rows, BLOCK), 1)
    in_cur = j <= qi
    dist_band = jnp.where(in_cur, qi - j, qi - j + BLOCK).astype(F32)
    band_ok = in_cur | (n > 0)
    dist_meta = (N_META + n * BLOCK + qi - j).astype(F32)
    meta_ok = j < N_META
    head_in_group = _iota((rows, 1), 0) // BLOCK

    def per_head_column(values):
        col = jnp.full((rows, 1), values[group - 1], F32)
        for r in range(group - 2, -1, -1):
            col = jnp.where(head_in_group == r, values[r], col)
        return col

    kvs = [(_dup_kv(kn, g), _dup_kv(vcat, g)) for g in range(SWA_KV_HEADS)]
    logits = [_dot_nt(jnp.concatenate(_lane_halves(qn[2 * g]) + _lane_halves(qn[2 * g + 1]), axis=0),
                      kvs[g][0]) for g in range(SWA_KV_HEADS)]
    weights, dens = [], []
    for g in range(SWA_KV_HEADS):
        heads = range(group * g, group * (g + 1))
        slope = per_head_column([_alibi_slope(h) * LOG2E for h in heads])
        sink = per_head_column([sink_ref[h] * LOG2E for h in heads])
        s = logits[g]
        s_band = jnp.where(in_cur, s[:, 2 * BLOCK:], s[:, BLOCK:2 * BLOCK]) - slope * dist_band
        s_band = jnp.where(band_ok, s_band, NEG)
        s_meta = jnp.where(meta_ok, s[:, :BLOCK] - slope * dist_meta, NEG)
        mx = jnp.maximum(jnp.maximum(jnp.max(s_band, axis=-1, keepdims=True),
                                     jnp.max(s_meta, axis=-1, keepdims=True)), sink)
        e_band = jnp.exp2(s_band - mx)
        e_meta = jnp.exp2(s_meta - mx)
        dens.append(jnp.sum(e_band, axis=-1, keepdims=True) + jnp.sum(e_meta, axis=-1, keepdims=True)
                    + jnp.exp2(sink - mx))
        weights.append(jnp.concatenate(
            [e_meta, jnp.where(in_cur, 0.0, e_band), jnp.where(in_cur, e_band, 0.0)], axis=1).astype(BF16))
    for g in range(SWA_KV_HEADS):
        o = _dot(weights[g], kvs[g][1]) / dens[g]
        for t in range(2):
            o_ref[:, (2 * g + t) * LANES:(2 * g + t + 1) * LANES] = _merge_halves(
                o[2 * t * BLOCK:(2 * t + 1) * BLOCK], o[(2 * t + 1) * BLOCK:(2 * t + 2) * BLOCK])


def _swa(proj, proj_meta, sinks, qg, kg, batch):
    m = proj.shape[0]
    nb = m // batch // BLOCK
    qw = SWA_HEADS * HEAD_DIM
    kvw = SWA_KV_HEADS * HEAD_DIM
    kcol, vcol = qw // kvw, qw // kvw + 1
    row = lambda b, n: b * nb + n
    return pl.pallas_call(
        _swa_kernel,
        grid=(batch, nb),
        in_specs=[
            pl.BlockSpec(memory_space=pltpu.SMEM),
            pl.BlockSpec((BLOCK, qw), lambda b, n: (row(b, n), 0)),
            pl.BlockSpec((BLOCK, kvw), lambda b, n: (row(b, n), kcol)),
            pl.BlockSpec((BLOCK, kvw), lambda b, n: (row(b, jnp.maximum(n - 1, 0)), kcol)),
            pl.BlockSpec((BLOCK, kvw), lambda b, n: (row(b, n), vcol)),
            pl.BlockSpec((BLOCK, kvw), lambda b, n: (row(b, jnp.maximum(n - 1, 0)), vcol)),
            pl.BlockSpec((N_META, kvw), lambda b, n: (0, kcol)),
            pl.BlockSpec((N_META, kvw), lambda b, n: (0, vcol)),
            pl.BlockSpec((1, qw), lambda b, n: (0, 0)),
            pl.BlockSpec((1, kvw), lambda b, n: (0, 0)),
        ],
        out_specs=pl.BlockSpec((BLOCK, qw), lambda b, n: (row(b, n), 0)),
        out_shape=jax.ShapeDtypeStruct((m, qw), F32),
        compiler_params=_cparams(("arbitrary", "arbitrary")),
        name="swa",
    )(sinks, proj, proj, proj, proj, proj, proj_meta, proj_meta, qg, kg)


def _sb_kernel(q_ref, k_ref, v_ref, km_ref, vm_ref, o_ref, kbf_ref, vbf_ref, kmp_ref, vmp_ref, *, tq, hp):
    i = pl.program_id(2)

    @pl.when(i == 0)
    def _():
        kbf_ref[...] = k_ref[...].astype(BF16)
        vbf_ref[...] = v_ref[...].astype(BF16)
        kmp_ref[...] = jnp.zeros_like(kmp_ref)
        vmp_ref[...] = jnp.zeros_like(vmp_ref)
        kmp_ref[0:N_META, :] = km_ref[...].astype(BF16)
        vmp_ref[0:N_META, :] = vm_ref[...].astype(BF16)

    lanes = lambda p: slice(p * LANES, (p + 1) * LANES)
    qs = [jnp.concatenate(_lane_halves((q_ref[:, lanes(p)] * SB_QSCALE).astype(BF16)), axis=0)
          for p in range(hp)]
    rows = 2 * tq
    suffix = _suffix_matrix(tq)

    def step(kv_of_pair, suffix, acc, carry, mask):
        kv = [kv_of_pair(p) for p in range(hp)]
        z2 = [_dot_nt(qs[p], kv[p][0]) for p in range(hp)]
        sp = [_softplus2(z) for z in z2]
        if mask is not None:
            sp = [jnp.where(mask, s, 0.0) for s in sp]
        zs = [z2[p] - sp[p] for p in range(hp)]
        later = _dot(jnp.concatenate([s.astype(BF16) for s in sp], axis=0), suffix)
        a = [jnp.exp2(zs[p] - (later[p * rows:(p + 1) * rows] + carry[p])) for p in range(hp)]
        if mask is not None:
            a = [jnp.where(mask, x, 0.0) for x in a]
        d = [_dot(a[p].astype(BF16), kv[p][1]) for p in range(hp)]
        acc = d if acc is None else [acc[p] + d[p] for p in range(hp)]
        return acc, [carry[p] + jnp.sum(sp[p], axis=-1, keepdims=True) for p in range(hp)]

    diag = _iota((rows, tq), 1) < (_iota((rows, tq), 0) % tq)
    start = pl.multiple_of(i * tq, tq)
    acc, carry = step(lambda p: (kbf_ref[pl.ds(start, tq), lanes(p)], vbf_ref[pl.ds(start, tq), lanes(p)]),
                      suffix, None, [jnp.zeros((rows, 1), F32)] * hp, diag)

    def more_keys_matter(carry):
        m = functools.reduce(jnp.minimum, carry)
        return (jnp.min(m) <= F32_UNDERFLOW_LOG2).astype(jnp.int32)

    def cond(state):
        return (state[0] < i) & (state[1] > 0)

    def body(state):
        t = state[0]
        s = pl.multiple_of((i - 1 - t) * tq, tq)
        acc, carry = step(lambda p: (kbf_ref[pl.ds(s, tq), lanes(p)], vbf_ref[pl.ds(s, tq), lanes(p)]),
                          suffix, list(state[2:2 + hp]), list(state[2 + hp:]), None)
        return (t + 1, more_keys_matter(carry)) + tuple(acc) + tuple(carry)

    state = lax.while_loop(cond, body, (jnp.int32(0), more_keys_matter(carry)) + tuple(acc) + tuple(carry))
    acc, carry = list(state[2:2 + hp]), list(state[2 + hp:])

    def meta_block():
        meta_ok = _iota((rows, LANES), 1) < N_META
        return tuple(step(lambda p: (kmp_ref[:, lanes(p)], vmp_ref[:, lanes(p)]),
                          _suffix_matrix(LANES), acc, carry, meta_ok)[0])

    acc = lax.cond(state[1] > 0, meta_block, lambda: tuple(acc))
    for p in range(hp):
        o_ref[:, lanes(p)] = _merge_halves(acc[p][:tq], acc[p][tq:])


def _sb(proj, proj_meta, batch, tq, hp):
    m = proj.shape[0]
    seq = m // batch
    nq = seq // tq
    w = hp * LANES
    qcol = (SWA_HEADS + 2 * SWA_KV_HEADS) * HEAD_DIM // w
    steps = SB_HEADS * HEAD_DIM // w
    kcol, vcol = qcol + steps, qcol + 2 * steps
    return pl.pallas_call(
        functools.partial(_sb_kernel, tq=tq, hp=hp),
        grid=(batch, steps, nq),
        in_specs=[
            pl.BlockSpec((tq, w), lambda b, p, i: (b * nq + i, qcol + p)),
            pl.BlockSpec((seq, w), lambda b, p, i: (b, kcol + p)),
            pl.BlockSpec((seq, w), lambda b, p, i: (b, vcol + p)),
            pl.BlockSpec((N_META, w), lambda b, p, i: (0, kcol + p)),
            pl.BlockSpec((N_META, w), lambda b, p, i: (0, vcol + p)),
        ],
        out_specs=pl.BlockSpec((tq, w), lambda b, p, i: (b * nq + i, p)),
        out_shape=jax.ShapeDtypeStruct((m, SB_HEADS * HEAD_DIM), F32),
        scratch_shapes=[pltpu.VMEM((seq, w), BF16), pltpu.VMEM((seq, w), BF16),
                        pltpu.VMEM((LANES, w), BF16), pltpu.VMEM((LANES, w), BF16)],
        compiler_params=_cparams(("arbitrary", "arbitrary", "arbitrary")),
        name="stickbreak",
    )(proj, proj, proj, proj_meta, proj_meta)


def _meta_attn_kernel(sink_ref, p_ref, qg_ref, kg_ref, oa_ref, ob_ref):
    qw = SWA_HEADS * HEAD_DIM
    kvw = SWA_KV_HEADS * HEAD_DIM
    group = SWA_HEADS // SWA_KV_HEADS
    x = p_ref[...]
    xk = _pad_rows(x, LANES)
    qi = _iota((N_META, LANES), 0)
    kj = _iota((N_META, LANES), 1)

    qn = [(t * SCALE).astype(BF16) for t in _head_rmsnorm(x[:, 0:qw], qg_ref[...])]
    kn = jnp.concatenate(_head_rmsnorm(xk[:, qw:qw + kvw], kg_ref[...]), axis=1).astype(BF16)
    vb = xk[:, qw + kvw:qw + 2 * kvw].astype(BF16)
    delta_f = (qi - kj).astype(F32)
    causal = kj <= qi
    for pair in range(SWA_HEADS // 2):
        kv = (2 * pair) // group
        kdup, vdup = _dup_kv(kn, kv), _dup_kv(vb, kv)
        outs = []
        for e, qe in enumerate(_lane_halves(qn[pair])):
            h = 2 * pair + e
            outs.append(_swa_head(qe, kdup, vdup, _alibi_slope(h) * delta_f, causal, sink_ref[h]))
        oa_ref[:, pair * LANES:(pair + 1) * LANES] = _merge_halves(outs[0], outs[1])

    base = qw + 2 * kvw
    sbw = SB_HEADS * HEAD_DIM
    strict = kj < qi
    suffix = _suffix_matrix(LANES)
    zero_c = jnp.zeros((N_META, 1), F32)
    for pair in range(SB_HEADS // 2):
        sl = lambda src, part: src[:, base + part * sbw + pair * LANES: base + part * sbw + (pair + 1) * LANES]
        kb, vb2 = sl(xk, 1).astype(BF16), sl(xk, 2).astype(BF16)
        outs = [_sb_block(qe, kb, vb2, suffix, zero_c, strict)[0]
                for qe in _lane_halves((sl(x, 0) * SB_QSCALE).astype(BF16))]
        ob_ref[:, pair * LANES:(pair + 1) * LANES] = _merge_halves(outs[0], outs[1])


def _meta_attn(proj_meta, sinks, qg, kg, batch):
    pw = proj_meta.shape[1]
    qw = SWA_HEADS * HEAD_DIM
    kvw = SWA_KV_HEADS * HEAD_DIM
    sbw = SB_HEADS * HEAD_DIM
    return pl.pallas_call(
        _meta_attn_kernel,
        grid=(batch,),
        in_specs=[
            pl.BlockSpec(memory_space=pltpu.SMEM),
            pl.BlockSpec((N_META, pw), lambda b: (b, 0)),
            pl.BlockSpec((1, qw), lambda b: (0, 0)),
            pl.BlockSpec((1, kvw), lambda b: (0, 0)),
        ],
        out_specs=[pl.BlockSpec((N_META, qw), lambda b: (b, 0)),
                   pl.BlockSpec((N_META, sbw), lambda b: (b, 0))],
        out_shape=[jax.ShapeDtypeStruct((batch * N_META, qw), F32),
                   jax.ShapeDtypeStruct((batch * N_META, sbw), F32)],
        compiler_params=_cparams(("arbitrary",)),
        name="meta_attn",
    )(sinks, proj_meta, qg, kg)


def kernel(x, meta_tokens, attn_norm_g, w_in, q_norm_g, k_norm_g, attn_sinks,
           swa_out_g, sb_out_g, w_o, ffn_norm_g, w_gate, w_up, w_down):
    batch, seq, d = x.shape
    depth = w_in.shape[0]
    qw = SWA_HEADS * HEAD_DIM
    assert seq % 512 == 0 and meta_tokens.shape[0] == N_META

    tm = 512
    tmeta = N_META
    tm_in = 1024
    tn_in = 1536
    tn_out = 2048
    tf = 512
    tn_down = 512
    tq = 256
    hp = 2

    h = x.reshape(batch * seq, d)
    hm = meta_tokens.astype(x.dtype)

    w_in_b, w_o_b = w_in.astype(BF16), w_o.astype(BF16)
    w_gate_b, w_up_b, w_down_b = w_gate.astype(BF16), w_up.astype(BF16), w_down.astype(BF16)

    for l in range(depth):
        g_attn = attn_norm_g[l].reshape(1, d)
        qg = jnp.tile(q_norm_g[l], SWA_HEADS).reshape(1, -1)
        kg = jnp.tile(k_norm_g[l], SWA_KV_HEADS).reshape(1, -1)
        sinks = attn_sinks[l].astype(F32)
        ga, gb = swa_out_g[l].reshape(1, -1), sb_out_g[l].reshape(1, -1)
        g_ffn = ffn_norm_g[l].reshape(1, d)

        proj = _norm_matmul(h, g_attn, w_in_b, l, tm_in, tn_in)
        proj_m = _norm_matmul(hm, g_attn, w_in_b, l, tmeta, tn_in)

        out_a = _swa(proj, proj_m, sinks, qg, kg, batch)
        out_b = _sb(proj, proj_m, batch, tq, hp)
        out_am, out_bm = _meta_attn(proj_m, sinks, qg, kg, 1)

        h = _outproj(out_a, out_b, ga, gb, w_o_b, l, h, tm, tn_out)
        hm = _outproj(out_am, out_bm, ga, gb, w_o_b, l, hm, tmeta, tn_out)

        h = _ffn(h, g_ffn, w_gate_b, w_up_b, w_down_b, l, tm, tf, tn_down)
        hm = _ffn(hm, g_ffn, w_gate_b, w_up_b, w_down_b, l, tmeta, tf, tn_down)

    return h.reshape(batch, seq, d)
```

```python
import functools

import jax
import jax.numpy as jnp
from jax import lax
from jax.experimental import pallas as pl
from jax.experimental.pallas import tpu as pltpu

F32 = jnp.float32
BF16 = jnp.bfloat16

N_META = 16
BLOCK = 128
HEAD_DIM = 64
LANES = 128
SWA_HEADS = 16
SWA_KV_HEADS = 4
SB_HEADS = 16
EPS = 1e-6
NEG = -1e30
SCALE = HEAD_DIM ** -0.5
LOG2E = 1.4426950408889634
SB_QSCALE = SCALE * LOG2E
F32_UNDERFLOW_LOG2 = 160.0
VMEM_LIMIT = 56 * 1024 * 1024


def _cparams(sem):
    return pltpu.CompilerParams(dimension_semantics=sem, vmem_limit_bytes=VMEM_LIMIT)


def _dot(a, b):
    return jnp.dot(a, b, preferred_element_type=F32)


def _dot_nt(a, b):
    return lax.dot_general(a, b, (((1,), (1,)), ((), ())), preferred_element_type=F32)


def _split_bf16(x):
    hi = x.astype(BF16)
    lo = (x - hi.astype(F32)).astype(BF16)
    return hi, lo


def _iota(shape, dim):
    return lax.broadcasted_iota(jnp.int32, shape, dim)


def _norm_mm_kernel(x_ref, g_ref, w_ref, o_ref, xn_ref):
    @pl.when(pl.program_id(1) == 0)
    def _():
        x = x_ref[...]
        ms = jnp.mean(x * x, axis=-1, keepdims=True)
        xn_ref[...] = ((x * lax.rsqrt(ms + EPS)) * g_ref[...]).astype(BF16)

    o_ref[...] = _dot(xn_ref[...], w_ref[...])


def _norm_matmul(x, g, w, layer, tm, tn):
    m, d = x.shape
    n = w.shape[2]
    return pl.pallas_call(
        _norm_mm_kernel,
        grid=(m // tm, n // tn),
        in_specs=[
            pl.BlockSpec((tm, d), lambda i, j: (i, 0)),
            pl.BlockSpec((1, d), lambda i, j: (0, 0)),
            pl.BlockSpec((None, d, tn), lambda i, j: (layer, 0, j)),
        ],
        out_specs=pl.BlockSpec((tm, tn), lambda i, j: (i, j)),
        out_shape=jax.ShapeDtypeStruct((m, n), F32),
        scratch_shapes=[pltpu.VMEM((tm, d), BF16)],
        compiler_params=_cparams(("arbitrary", "arbitrary")),
        name="norm_inproj",
    )(x, g, w)


def _outproj_kernel(a_ref, b_ref, ga_ref, gb_ref, wa_ref, wb_ref, h_ref, o_ref, an_ref, bn_ref):
    @pl.when(pl.program_id(1) == 0)
    def _():
        for src, g, dst in ((a_ref, ga_ref, an_ref), (b_ref, gb_ref, bn_ref)):
            x = src[...]
            ms = jnp.mean(x * x, axis=-1, keepdims=True)
            dst[...] = ((x * lax.rsqrt(ms + EPS)) * g[...]).astype(BF16)

    o_ref[...] = h_ref[...] + (_dot(an_ref[...], wa_ref[...]) + _dot(bn_ref[...], wb_ref[...]))


def _outproj(a, b, ga, gb, w, layer, h, tm, tn):
    m, wdt = a.shape
    d = h.shape[1]
    return pl.pallas_call(
        _outproj_kernel,
        grid=(m // tm, d // tn),
        in_specs=[
            pl.BlockSpec((tm, wdt), lambda i, j: (i, 0)),
            pl.BlockSpec((tm, wdt), lambda i, j: (i, 0)),
            pl.BlockSpec((1, wdt), lambda i, j: (0, 0)),
            pl.BlockSpec((1, wdt), lambda i, j: (0, 0)),
            pl.BlockSpec((None, wdt, tn), lambda i, j: (layer, 0, j)),
            pl.BlockSpec((None, wdt, tn), lambda i, j: (layer, 1, j)),
            pl.BlockSpec((tm, tn), lambda i, j: (i, j)),
        ],
        out_specs=pl.BlockSpec((tm, tn), lambda i, j: (i, j)),
        out_shape=jax.ShapeDtypeStruct((m, d), F32),
        scratch_shapes=[pltpu.VMEM((tm, wdt), BF16), pltpu.VMEM((tm, wdt), BF16)],
        compiler_params=_cparams(("arbitrary", "arbitrary")),
        name="outproj",
    )(a, b, ga, gb, w, w, h)


def _ffn_kernel(x_ref, g_ref, wg_ref, wu_ref, wd_ref, o_ref, xn_ref, acc_ref):
    j = pl.program_id(1)

    @pl.when(j == 0)
    def _():
        x = x_ref[...]
        ms = jnp.mean(x * x, axis=-1, keepdims=True)
        xn_ref[...] = ((x * lax.rsqrt(ms + EPS)) * g_ref[...]).astype(BF16)
        acc_ref[...] = jnp.zeros_like(acc_ref)

    xn = xn_ref[...]
    gate = _dot(xn, wg_ref[...])
    up = _dot(xn, wu_ref[...])
    act = (gate / (1.0 + jnp.exp(-gate))) * up
    acc_ref[...] += _dot(act.astype(BF16), wd_ref[...])

    @pl.when(j == pl.num_programs(1) - 1)
    def _():
        o_ref[...] = x_ref[...] + acc_ref[...]


def _ffn(x, g, wg, wu, wd, layer, tm, tf):
    m, d = x.shape
    f = wg.shape[2]
    return pl.pallas_call(
        _ffn_kernel,
        grid=(m // tm, f // tf),
        in_specs=[
            pl.BlockSpec((tm, d), lambda i, j: (i, 0)),
            pl.BlockSpec((1, d), lambda i, j: (0, 0)),
            pl.BlockSpec((None, d, tf), lambda i, j: (layer, 0, j)),
            pl.BlockSpec((None, d, tf), lambda i, j: (layer, 0, j)),
            pl.BlockSpec((None, tf, d), lambda i, j: (layer, j, 0)),
        ],
        out_specs=pl.BlockSpec((tm, d), lambda i, j: (i, 0)),
        out_shape=jax.ShapeDtypeStruct((m, d), F32),
        scratch_shapes=[pltpu.VMEM((tm, d), BF16), pltpu.VMEM((tm, d), F32)],
        compiler_params=_cparams(("arbitrary", "arbitrary")),
        name="ffn",
    )(x, g, wg, wu, wd)


def _head_rmsnorm(x, g):
    r, c = x.shape
    same_head = (_iota((LANES, LANES), 0) // HEAD_DIM) == (_iota((LANES, LANES), 1) // HEAD_DIM)
    bd = jnp.where(same_head, 1.0, 0.0).astype(BF16)
    outs = []
    for t in range(c // LANES):
        xb = x[:, t * LANES:(t + 1) * LANES]
        hi, lo = _split_bf16(xb * xb)
        ss = _dot(hi, bd) + _dot(lo, bd)
        outs.append((xb * lax.rsqrt(ss * (1.0 / HEAD_DIM) + EPS)) * g[:, t * LANES:(t + 1) * LANES])
    return outs


def _half_select(half):
    src = _iota((LANES, LANES), 0)
    dst = _iota((LANES, LANES), 1)
    return jnp.where(src == (dst % HEAD_DIM) + HEAD_DIM * half, 1.0, 0.0).astype(BF16)


def _lane_halves(x):
    lo_half = _iota(x.shape, 1) < HEAD_DIM
    zero = jnp.zeros_like(x)
    return jnp.where(lo_half, x, zero), jnp.where(lo_half, zero, x)


def _merge_halves(lo, hi):
    return jnp.where(_iota(lo.shape, 1) < HEAD_DIM, lo, hi)


def _alibi_slope(h):
    return 2.0 ** (-8.0 * (h + 1) / SWA_HEADS)


def _softplus2(z2):
    neg_abs = lax.bitcast_convert_type(
        lax.bitcast_convert_type(z2, jnp.uint32) | jnp.uint32(0x80000000), F32)
    return jnp.maximum(z2, 0.0) + jnp.log2(1.0 + jnp.exp2(neg_abs))


def _suffix_matrix(n):
    return jnp.where(_iota((n, n), 0) > _iota((n, n), 1), 1.0, 0.0).astype(BF16)


def _sb_block(qe, kblk, vblk, suffix, carry, mask):
    z2 = _dot_nt(qe, kblk)
    sp = _softplus2(z2)
    if mask is not None:
        sp = jnp.where(mask, sp, 0.0)
    later = _dot(sp.astype(BF16), suffix)
    a = jnp.exp2((z2 - sp) - (later + carry))
    if mask is not None:
        a = jnp.where(mask, a, 0.0)
    contrib = _dot(a.astype(BF16), vblk)
    return contrib, carry + jnp.sum(sp, axis=-1, keepdims=True)


def _swa_head(qe, kdup, vdup, bias, mask, sink):
    s = jnp.where(mask, _dot_nt(qe, kdup) - bias, NEG)
    mx = jnp.maximum(jnp.max(s, axis=-1, keepdims=True), sink)
    e = jnp.exp(s - mx)
    den = jnp.sum(e, axis=-1, keepdims=True) + jnp.exp(sink - mx)
    return _dot(e.astype(BF16), vdup) / den


def _pad_rows(x, rows):
    return jnp.concatenate([x, jnp.zeros((rows - x.shape[0], x.shape[1]), x.dtype)], axis=0)


def _dup_kv(x_bf, kv_head):
    t = kv_head // 2
    return _dot(x_bf[:, t * LANES:(t + 1) * LANES], _half_select(kv_head % 2)).astype(BF16)


def _swa_kernel(sink_ref, q_ref, kc_ref, kp_ref, vc_ref, vp_ref, km_ref, vm_ref, qg_ref, kg_ref, o_ref):
    n = pl.program_id(1)
    group = SWA_HEADS // SWA_KV_HEADS
    rows = group * BLOCK
    qn = [(x * (SCALE * LOG2E)).astype(BF16) for x in _head_rmsnorm(q_ref[...], qg_ref[...])]
    kcat = jnp.concatenate([_pad_rows(km_ref[...], BLOCK), kp_ref[...], kc_ref[...]], axis=0)
    kn = jnp.concatenate(_head_rmsnorm(kcat, kg_ref[...]), axis=1).astype(BF16)
    vcat = jnp.concatenate([_pad_rows(vm_ref[...], BLOCK), vp_ref[...], vc_ref[...]], axis=0).astype(BF16)

    qi = _iota((rows, BLOCK), 0) % BLOCK
    j = _iota((rows, BLOCK), 1)
    in_cur = j <= qi
    dist_band = jnp.where(in_cur, qi - j, qi - j + BLOCK).astype(F32)
    band_ok = in_cur | (n > 0)
    dist_meta = (N_META + n * BLOCK + qi - j).astype(F32)
    meta_ok = j < N_META
    head_in_group = _iota((rows, 1), 0) // BLOCK

    def per_head_column(values):
        col = jnp.full((rows, 1), values[group - 1], F32)
        for r in range(group - 2, -1, -1):
            col = jnp.where(head_in_group == r, values[r], col)
        return col

    kvs = [(_dup_kv(kn, g), _dup_kv(vcat, g)) for g in range(SWA_KV_HEADS)]
    logits = [_dot_nt(jnp.concatenate(_lane_halves(qn[2 * g]) + _lane_halves(qn[2 * g + 1]), axis=0),
                      kvs[g][0]) for g in range(SWA_KV_HEADS)]
    weights, dens = [], []
    for g in range(SWA_KV_HEADS):
        heads = range(group * g, group * (g + 1))
        slope = per_head_column([_alibi_slope(h) * LOG2E for h in heads])
        sink = per_head_column([sink_ref[h] * LOG2E for h in heads])
        s = logits[g]
        s_band = jnp.where(in_cur, s[:, 2 * BLOCK:], s[:, BLOCK:2 * BLOCK]) - slope * dist_band
        s_band = jnp.where(band_ok, s_band, NEG)
        s_meta = jnp.where(meta_ok, s[:, :BLOCK] - slope * dist_meta, NEG)
        mx = jnp.maximum(jnp.maximum(jnp.max(s_band, axis=-1, keepdims=True),
                                     jnp.max(s_meta, axis=-1, keepdims=True)), sink)
        e_band = jnp.exp2(s_band - mx)
        e_meta = jnp.exp2(s_meta - mx)
        dens.append(jnp.sum(e_band, axis=-1, keepdims=True) + jnp.sum(e_meta, axis=-1, keepdims=True)
                    + jnp.exp2(sink - mx))
        weights.append(jnp.concatenate(
            [e_meta, jnp.where(in_cur, 0.0, e_band), jnp.where(in_cur, e_band, 0.0)], axis=1).astype(BF16))
    for g in range(SWA_KV_HEADS):
        o = _dot(weights[g], kvs[g][1]) / dens[g]
        for t in range(2):
            o_ref[:, (2 * g + t) * LANES:(2 * g + t + 1) * LANES] = _merge_halves(
                o[2 * t * BLOCK:(2 * t + 1) * BLOCK], o[(2 * t + 1) * BLOCK:(2 * t + 2) * BLOCK])


def _swa(proj, proj_meta, sinks, qg, kg, batch):
    m = proj.shape[0]
    nb = m // batch // BLOCK
    qw = SWA_HEADS * HEAD_DIM
    kvw = SWA_KV_HEADS * HEAD_DIM
    kcol, vcol = qw // kvw, qw // kvw + 1
    row = lambda b, n: b * nb + n
    return pl.pallas_call(
        _swa_kernel,
        grid=(batch, nb),
        in_specs=[
            pl.BlockSpec(memory_space=pltpu.SMEM),
            pl.BlockSpec((BLOCK, qw), lambda b, n: (row(b, n), 0)),
            pl.BlockSpec((BLOCK, kvw), lambda b, n: (row(b, n), kcol)),
            pl.BlockSpec((BLOCK, kvw), lambda b, n: (row(b, jnp.maximum(n - 1, 0)), kcol)),
            pl.BlockSpec((BLOCK, kvw), lambda b, n: (row(b, n), vcol)),
            pl.BlockSpec((BLOCK, kvw), lambda b, n: (row(b, jnp.maximum(n - 1, 0)), vcol)),
            pl.BlockSpec((N_META, kvw), lambda b, n: (0, kcol)),
            pl.BlockSpec((N_META, kvw), lambda b, n: (0, vcol)),
            pl.BlockSpec((1, qw), lambda b, n: (0, 0)),
            pl.BlockSpec((1, kvw), lambda b, n: (0, 0)),
        ],
        out_specs=pl.BlockSpec((BLOCK, qw), lambda b, n: (row(b, n), 0)),
        out_shape=jax.ShapeDtypeStruct((m, qw), F32),
        compiler_params=_cparams(("arbitrary", "arbitrary")),
        name="swa",
    )(sinks, proj, proj, proj, proj, proj, proj_meta, proj_meta, qg, kg)


def _sb_kernel(q_ref, k_ref, v_ref, km_ref, vm_ref, o_ref,
               kbf_ref, vbf_ref, kmp_ref, vmp_ref, acc_ref, car_ref, more_ref, *, tq, hp):
    i = pl.program_id(2)

    @pl.when(i == 0)
    def _():
        kbf_ref[...] = k_ref[...].astype(BF16)
        vbf_ref[...] = v_ref[...].astype(BF16)
        kmp_ref[...] = jnp.zeros_like(kmp_ref)
        vmp_ref[...] = jnp.zeros_like(vmp_ref)
        kmp_ref[0:N_META, :] = km_ref[...].astype(BF16)
        vmp_ref[0:N_META, :] = vm_ref[...].astype(BF16)

    lanes = lambda p: slice(p * LANES, (p + 1) * LANES)
    rows = 2 * tq

    def queries():
        return [jnp.concatenate(_lane_halves((q_ref[:, lanes(p)] * SB_QSCALE).astype(BF16)), axis=0)
                for p in range(hp)]

    def note_carries(carries):
        m = functools.reduce(jnp.minimum, carries)
        while m.shape[0] > 8:
            half = m.shape[0] // 2
            m = jnp.minimum(m[:half], m[half:])
        more_ref[0] = (jnp.min(m) <= F32_UNDERFLOW_LOG2).astype(jnp.int32)

    def block_step(kv_of_pair, width, mask, first):
        qs = queries()
        kv = [kv_of_pair(p) for p in range(hp)]
        z2 = [_dot_nt(qs[p], kv[p][0]) for p in range(hp)]
        sp = [_softplus2(z) for z in z2]
        if mask is not None:
            sp = [jnp.where(mask, s, 0.0) for s in sp]
        zs = [z2[p] - sp[p] for p in range(hp)]
        later = _dot(jnp.concatenate([s.astype(BF16) for s in sp], axis=0), _suffix_matrix(width))
        carries = []
        for p in range(hp):
            carry = 0.0 if first else car_ref[p]
            a = jnp.exp2(zs[p] - (later[p * rows:(p + 1) * rows] + carry))
            if mask is not None:
                a = jnp.where(mask, a, 0.0)
            d = _dot(a.astype(BF16), kv[p][1])
            acc_ref[p] = d if first else acc_ref[p] + d
            carries.append(carry + jnp.sum(sp[p], axis=-1, keepdims=True))
            car_ref[p] = carries[-1]
        note_carries(carries)

    def kv_block(start):
        return lambda p: (kbf_ref[pl.ds(start, tq), lanes(p)], vbf_ref[pl.ds(start, tq), lanes(p)])

    below_diag = lambda: _iota((rows, tq), 1) < (_iota((rows, tq), 0) % tq)
    prev_start = pl.multiple_of(jnp.maximum(i - 1, 0) * tq, tq)

    @pl.when(i == 0)
    def _():
        block_step(kv_block(0), tq, below_diag(), True)

    @pl.when(i > 0)
    def _():
        qs = queries()
        in_cur = below_diag()
        cur_start = pl.multiple_of(i * tq, tq)
        zs, spb = [], []
        for p in range(hp):
            z_cur = _dot_nt(qs[p], kbf_ref[pl.ds(cur_start, tq), lanes(p)])
            z_prev = _dot_nt(qs[p], kbf_ref[pl.ds(prev_start, tq), lanes(p)])
            z2 = jnp.where(in_cur, z_cur, z_prev)
            sp = _softplus2(z2)
            zs.append(z2 - sp)
            spb.append(sp.astype(BF16))
        zero = jnp.zeros((rows, tq), BF16)
        prev_part = lambda x: jnp.where(in_cur, zero, x)
        cur_part = lambda x: jnp.where(in_cur, x, zero)
        suf = _dot(jnp.concatenate(spb, axis=0), _suffix_matrix(tq))
        prev_tot = _dot(jnp.concatenate([prev_part(s) for s in spb], axis=0), jnp.ones((tq, LANES), BF16))
        carries = []
        for p in range(hp):
            sf = suf[p * rows:(p + 1) * rows]
            ptw = jnp.concatenate([prev_tot[p * rows:(p + 1) * rows]] * (tq // LANES), axis=1)
            total = sf[:, 0:1] + spb[p][:, 0:1].astype(F32)
            later = sf + jnp.where(in_cur, -ptw, total - ptw)
            a = jnp.exp2(zs[p] - later).astype(BF16)
            v2 = jnp.concatenate([vbf_ref[pl.ds(prev_start, tq), lanes(p)],
                                  vbf_ref[pl.ds(cur_start, tq), lanes(p)]], axis=0)
            acc_ref[p] = _dot(jnp.concatenate([prev_part(a), cur_part(a)], axis=1), v2)
            car_ref[p] = total
            carries.append(total)
        note_carries(carries)

    @pl.when((i > 0) & (more_ref[0] > 0))
    def _():
        block_step(kv_block(prev_start), tq, below_diag(), False)

    def body(t):
        block_step(kv_block(pl.multiple_of((i - 2 - t) * tq, tq)), tq, None, False)
        return t + 1

    lax.while_loop(lambda t: (t < i - 1) & (more_ref[0] > 0), body, jnp.int32(0))

    @pl.when(more_ref[0] > 0)
    def _():
        block_step(lambda p: (kmp_ref[:, lanes(p)], vmp_ref[:, lanes(p)]), LANES,
                   _iota((rows, LANES), 1) < N_META, False)

    for p in range(hp):
        o_ref[:, lanes(p)] = _merge_halves(acc_ref[p, 0:tq], acc_ref[p, tq:rows])


def _sb(proj, proj_meta, batch, tq, hp):
    m = proj.shape[0]
    seq = m // batch
    nq = seq // tq
    w = hp * LANES
    qcol = (SWA_HEADS + 2 * SWA_KV_HEADS) * HEAD_DIM // w
    steps = SB_HEADS * HEAD_DIM // w
    kcol, vcol = qcol + steps, qcol + 2 * steps
    return pl.pallas_call(
        functools.partial(_sb_kernel, tq=tq, hp=hp),
        grid=(batch, steps, nq),
        in_specs=[
            pl.BlockSpec((tq, w), lambda b, p, i: (b * nq + i, qcol + p)),
            pl.BlockSpec((seq, w), lambda b, p, i: (b, kcol + p)),
            pl.BlockSpec((seq, w), lambda b, p, i: (b, vcol + p)),
            pl.BlockSpec((N_META, w), lambda b, p, i: (0, kcol + p)),
            pl.BlockSpec((N_META, w), lambda b, p, i: (0, vcol + p)),
        ],
        out_specs=pl.BlockSpec((tq, w), lambda b, p, i: (b * nq + i, p)),
        out_shape=jax.ShapeDtypeStruct((m, SB_HEADS * HEAD_DIM), F32),
        scratch_shapes=[pltpu.VMEM((seq, w), BF16), pltpu.VMEM((seq, w), BF16),
                        pltpu.VMEM((LANES, w), BF16), pltpu.VMEM((LANES, w), BF16),
                        pltpu.VMEM((hp, 2 * tq, LANES), F32), pltpu.VMEM((hp, 2 * tq, 1), F32),
                        pltpu.SMEM((1,), jnp.int32)],
        compiler_params=_cparams(("arbitrary", "arbitrary", "arbitrary")),
        name="stickbreak",
    )(proj, proj, proj, proj_meta, proj_meta)


def _meta_attn_kernel(sink_ref, p_ref, qg_ref, kg_ref, oa_ref, ob_ref):
    qw = SWA_HEADS * HEAD_DIM
    kvw = SWA_KV_HEADS * HEAD_DIM
    group = SWA_HEADS // SWA_KV_HEADS
    x = p_ref[...]
    xk = _pad_rows(x, LANES)
    qi = _iota((N_META, LANES), 0)
    kj = _iota((N_META, LANES), 1)

    qn = [(t * SCALE).astype(BF16) for t in _head_rmsnorm(x[:, 0:qw], qg_ref[...])]
    kn = jnp.concatenate(_head_rmsnorm(xk[:, qw:qw + kvw], kg_ref[...]), axis=1).astype(BF16)
    vb = xk[:, qw + kvw:qw + 2 * kvw].astype(BF16)
    delta_f = (qi - kj).astype(F32)
    causal = kj <= qi
    for pair in range(SWA_HEADS // 2):
        kv = (2 * pair) // group
        kdup, vdup = _dup_kv(kn, kv), _dup_kv(vb, kv)
        outs = []
        for e, qe in enumerate(_lane_halves(qn[pair])):
            h = 2 * pair + e
            outs.append(_swa_head(qe, kdup, vdup, _alibi_slope(h) * delta_f, causal, sink_ref[h]))
        oa_ref[:, pair * LANES:(pair + 1) * LANES] = _merge_halves(outs[0], outs[1])

    base = qw + 2 * kvw
    sbw = SB_HEADS * HEAD_DIM
    strict = kj < qi
    suffix = _suffix_matrix(LANES)
    zero_c = jnp.zeros((N_META, 1), F32)
    for pair in range(SB_HEADS // 2):
        sl = lambda src, part: src[:, base + part * sbw + pair * LANES: base + part * sbw + (pair + 1) * LANES]
        kb, vb2 = sl(xk, 1).astype(BF16), sl(xk, 2).astype(BF16)
        outs = [_sb_block(qe, kb, vb2, suffix, zero_c, strict)[0]
                for qe in _lane_halves((sl(x, 0) * SB_QSCALE).astype(BF16))]
        ob_ref[:, pair * LANES:(pair + 1) * LANES] = _merge_halves(outs[0], outs[1])


def _meta_attn(proj_meta, sinks, qg, kg, batch):
    pw = proj_meta.shape[1]
    qw = SWA_HEADS * HEAD_DIM
    kvw = SWA_KV_HEADS * HEAD_DIM
    sbw = SB_HEADS * HEAD_DIM
    return pl.pallas_call(
        _meta_attn_kernel,
        grid=(batch,),
        in_specs=[
            pl.BlockSpec(memory_space=pltpu.SMEM),
            pl.BlockSpec((N_META, pw), lambda b: (b, 0)),
            pl.BlockSpec((1, qw), lambda b: (0, 0)),
            pl.BlockSpec((1, kvw), lambda b: (0, 0)),
        ],
        out_specs=[pl.BlockSpec((N_META, qw), lambda b: (b, 0)),
                   pl.BlockSpec((N_META, sbw), lambda b: (b, 0))],
        out_shape=[jax.ShapeDtypeStruct((batch * N_META, qw), F32),
                   jax.ShapeDtypeStruct((batch * N_META, sbw), F32)],
        compiler_params=_cparams(("arbitrary",)),
        name="meta_attn",
    )(sinks, proj_meta, qg, kg)


def kernel(x, meta_tokens, attn_norm_g, w_in, q_norm_g, k_norm_g, attn_sinks,
           swa_out_g, sb_out_g, w_o, ffn_norm_g, w_gate, w_up, w_down):
    batch, seq, d = x.shape
    depth = w_in.shape[0]
    qw = SWA_HEADS * HEAD_DIM
    assert seq % 512 == 0 and meta_tokens.shape[0] == N_META

    tm = 512
    tmeta = N_META
    tm_in = 1024
    tn_in = 1536
    tn_out = 2048
    tf = 512
    tq = 256
    hp = 2

    h = x.reshape(batch * seq, d)
    hm = meta_tokens.astype(x.dtype)

    w_in_b, w_o_b = w_in.astype(BF16), w_o.astype(BF16)
    w_gate_b, w_up_b, w_down_b = w_gate.astype(BF16), w_up.astype(BF16), w_down.astype(BF16)

    for l in range(depth):
        g_attn = attn_norm_g[l].reshape(1, d)
        qg = jnp.tile(q_norm_g[l], SWA_HEADS).reshape(1, -1)
        kg = jnp.tile(k_norm_g[l], SWA_KV_HEADS).reshape(1, -1)
        sinks = attn_sinks[l].astype(F32)
        ga, gb = swa_out_g[l].reshape(1, -1), sb_out_g[l].reshape(1, -1)
        g_ffn = ffn_norm_g[l].reshape(1, d)

        proj = _norm_matmul(h, g_attn, w_in_b, l, tm_in, tn_in)
        proj_m = _norm_matmul(hm, g_attn, w_in_b, l, tmeta, tn_in)

        out_a = _swa(proj, proj_m, sinks, qg, kg, batch)
        out_b = _sb(proj, proj_m, batch, tq, hp)
        out_am, out_bm = _meta_attn(proj_m, sinks, qg, kg, 1)

        h = _outproj(out_a, out_b, ga, gb, w_o_b, l, h, tm, tn_out)
        hm = _outproj(out_am, out_bm, ga, gb, w_o_b, l, hm, tmeta, tn_out)

        h = _ffn(h, g_ffn, w_gate_b, w_up_b, w_down_b, l, tm, tf)
        hm = _ffn(hm, g_ffn, w_gate_b, w_up_b, w_down_b, l, tmeta, tf)

    return h.reshape(batch, seq, d)
```

```python
import functools

import jax
import jax.numpy as jnp
from jax import lax
from jax.experimental import pallas as pl
from jax.experimental.pallas import tpu as pltpu

F32 = jnp.float32
BF16 = jnp.bfloat16

N_META = 16
BLOCK = 128
HEAD_DIM = 64
LANES = 128
SWA_HEADS = 16
SWA_KV_HEADS = 4
SB_HEADS = 16
EPS = 1e-6
NEG = -1e30
SCALE = HEAD_DIM ** -0.5
LOG2E = 1.4426950408889634
SB_QSCALE = SCALE * LOG2E
F32_UNDERFLOW_LOG2 = 160.0
VMEM_LIMIT = 56 * 1024 * 1024


def _cparams(sem):
    return pltpu.CompilerParams(dimension_semantics=sem, vmem_limit_bytes=VMEM_LIMIT)


def _dot(a, b):
    return jnp.dot(a, b, preferred_element_type=F32)


def _dot_nt(a, b):
    return lax.dot_general(a, b, (((1,), (1,)), ((), ())), preferred_element_type=F32)


def _iota(shape, dim):
    return lax.broadcasted_iota(jnp.int32, shape, dim)


def _norm_mm_kernel(x_ref, g_ref, w_ref, o_ref, xn_ref):
    @pl.when(pl.program_id(1) == 0)
    def _():
        x = x_ref[...]
        ms = jnp.mean(x * x, axis=-1, keepdims=True)
        xn_ref[...] = ((x * lax.rsqrt(ms + EPS)) * g_ref[...]).astype(BF16)

    o_ref[...] = _dot(xn_ref[...], w_ref[...])


def _norm_matmul(x, g, w, layer, tm, tn):
    m, d = x.shape
    n = w.shape[2]
    return pl.pallas_call(
        _norm_mm_kernel,
        grid=(m // tm, n // tn),
        in_specs=[
            pl.BlockSpec((tm, d), lambda i, j: (i, 0)),
            pl.BlockSpec((1, d), lambda i, j: (0, 0)),
            pl.BlockSpec((None, d, tn), lambda i, j: (layer, 0, j)),
        ],
        out_specs=pl.BlockSpec((tm, tn), lambda i, j: (i, j)),
        out_shape=jax.ShapeDtypeStruct((m, n), F32),
        scratch_shapes=[pltpu.VMEM((tm, d), BF16)],
        compiler_params=_cparams(("arbitrary", "arbitrary")),
        name="norm_inproj",
    )(x, g, w)


def _outproj_kernel(a_ref, b_ref, ga_ref, gb_ref, wa_ref, wb_ref, h_ref, o_ref, an_ref, bn_ref):
    @pl.when(pl.program_id(1) == 0)
    def _():
        for src, g, dst in ((a_ref, ga_ref, an_ref), (b_ref, gb_ref, bn_ref)):
            x = src[...]
            ms = jnp.mean(x * x, axis=-1, keepdims=True)
            dst[...] = ((x * lax.rsqrt(ms + EPS)) * g[...]).astype(BF16)

    o_ref[...] = h_ref[...] + (_dot(an_ref[...], wa_ref[...]) + _dot(bn_ref[...], wb_ref[...]))


def _outproj(a, b, ga, gb, w, layer, h, tm, tn):
    m, wdt = a.shape
    d = h.shape[1]
    return pl.pallas_call(
        _outproj_kernel,
        grid=(m // tm, d // tn),
        in_specs=[
            pl.BlockSpec((tm, wdt), lambda i, j: (i, 0)),
            pl.BlockSpec((tm, wdt), lambda i, j: (i, 0)),
            pl.BlockSpec((1, wdt), lambda i, j: (0, 0)),
            pl.BlockSpec((1, wdt), lambda i, j: (0, 0)),
            pl.BlockSpec((None, wdt, tn), lambda i, j: (layer, 0, j)),
            pl.BlockSpec((None, wdt, tn), lambda i, j: (layer, 1, j)),
            pl.BlockSpec((tm, tn), lambda i, j: (i, j)),
        ],
        out_specs=pl.BlockSpec((tm, tn), lambda i, j: (i, j)),
        out_shape=jax.ShapeDtypeStruct((m, d), F32),
        scratch_shapes=[pltpu.VMEM((tm, wdt), BF16), pltpu.VMEM((tm, wdt), BF16)],
        compiler_params=_cparams(("arbitrary", "arbitrary")),
        name="outproj",
    )(a, b, ga, gb, w, w, h)


def _ffn_kernel(x_ref, g_ref, wg_ref, wu_ref, wd_ref, o_ref, xn_ref, acc_ref):
    j = pl.program_id(1)

    @pl.when(j == 0)
    def _():
        x = x_ref[...]
        ms = jnp.mean(x * x, axis=-1, keepdims=True)
        xn_ref[...] = ((x * lax.rsqrt(ms + EPS)) * g_ref[...]).astype(BF16)
        acc_ref[...] = jnp.zeros_like(acc_ref)

    xn = xn_ref[...]
    gate = _dot(xn, wg_ref[...])
    up = _dot(xn, wu_ref[...])
    act = (gate / (1.0 + jnp.exp(-gate))) * up
    acc_ref[...] += _dot(act.astype(BF16), wd_ref[...])

    @pl.when(j == pl.num_programs(1) - 1)
    def _():
        o_ref[...] = x_ref[...] + acc_ref[...]


def _ffn(x, g, wg, wu, wd, layer, tm, tf):
    m, d = x.shape
    f = wg.shape[2]
    return pl.pallas_call(
        _ffn_kernel,
        grid=(m // tm, f // tf),
        in_specs=[
            pl.BlockSpec((tm, d), lambda i, j: (i, 0)),
            pl.BlockSpec((1, d), lambda i, j: (0, 0)),
            pl.BlockSpec((None, d, tf), lambda i, j: (layer, 0, j)),
            pl.BlockSpec((None, d, tf), lambda i, j: (layer, 0, j)),
            pl.BlockSpec((None, tf, d), lambda i, j: (layer, j, 0)),
        ],
        out_specs=pl.BlockSpec((tm, d), lambda i, j: (i, 0)),
        out_shape=jax.ShapeDtypeStruct((m, d), F32),
        scratch_shapes=[pltpu.VMEM((tm, d), BF16), pltpu.VMEM((tm, d), F32)],
        compiler_params=_cparams(("arbitrary", "arbitrary")),
        name="ffn",
    )(x, g, wg, wu, wd)


def _head_rmsnorm(x, g):
    r, c = x.shape
    same_head = (_iota((LANES, LANES), 0) // HEAD_DIM) == (_iota((LANES, LANES), 1) // HEAD_DIM)
    bd = jnp.where(same_head, 1.0, 0.0).astype(BF16)
    outs = []
    for t in range(c // LANES):
        xb = x[:, t * LANES:(t + 1) * LANES]
        ss = _dot((xb * xb).astype(BF16), bd)
        outs.append((xb * lax.rsqrt(ss * (1.0 / HEAD_DIM) + EPS)) * g[:, t * LANES:(t + 1) * LANES])
    return outs


def _half_select(half):
    src = _iota((LANES, LANES), 0)
    dst = _iota((LANES, LANES), 1)
    return jnp.where(src == (dst % HEAD_DIM) + HEAD_DIM * half, 1.0, 0.0).astype(BF16)


def _lane_halves(x):
    lo_half = _iota(x.shape, 1) < HEAD_DIM
    zero = jnp.zeros_like(x)
    return jnp.where(lo_half, x, zero), jnp.where(lo_half, zero, x)


def _merge_halves(lo, hi):
    return jnp.where(_iota(lo.shape, 1) < HEAD_DIM, lo, hi)


def _alibi_slope(h):
    return 2.0 ** (-8.0 * (h + 1) / SWA_HEADS)


def _softplus2(z2):
    neg_abs = lax.bitcast_convert_type(
        lax.bitcast_convert_type(z2, jnp.uint32) | jnp.uint32(0x80000000), F32)
    return jnp.maximum(z2, 0.0) + jnp.log2(1.0 + jnp.exp2(neg_abs))


def _suffix_matrix(n):
    return jnp.where(_iota((n, n), 0) > _iota((n, n), 1), 1.0, 0.0).astype(BF16)


def _sb_block(qe, kblk, vblk, suffix, carry, mask):
    z2 = _dot_nt(qe, kblk)
    sp = _softplus2(z2)
    if mask is not None:
        sp = jnp.where(mask, sp, 0.0)
    later = _dot(sp.astype(BF16), suffix)
    a = jnp.exp2((z2 - sp) - (later + carry))
    if mask is not None:
        a = jnp.where(mask, a, 0.0)
    contrib = _dot(a.astype(BF16), vblk)
    return contrib, carry + jnp.sum(sp, axis=-1, keepdims=True)


def _swa_head(qe, kdup, vdup, bias, mask, sink):
    s = jnp.where(mask, _dot_nt(qe, kdup) - bias, NEG)
    mx = jnp.maximum(jnp.max(s, axis=-1, keepdims=True), sink)
    e = jnp.exp(s - mx)
    den = jnp.sum(e, axis=-1, keepdims=True) + jnp.exp(sink - mx)
    return _dot(e.astype(BF16), vdup) / den


def _pad_rows(x, rows):
    return jnp.concatenate([x, jnp.zeros((rows - x.shape[0], x.shape[1]), x.dtype)], axis=0)


def _dup_kv(x_bf, kv_head):
    t = kv_head // 2
    return _dot(x_bf[:, t * LANES:(t + 1) * LANES], _half_select(kv_head % 2)).astype(BF16)


def _swa_tables():
    group = SWA_HEADS // SWA_KV_HEADS
    slope = jnp.asarray([_alibi_slope(h) * LOG2E for h in range(SWA_HEADS)], F32)
    slope = jnp.repeat(slope.reshape(SWA_KV_HEADS, group, 1, 1), BLOCK, axis=2)
    qi = jnp.arange(BLOCK, dtype=F32)[:, None]
    j = jnp.arange(BLOCK, dtype=F32)[None, :]
    dist_band = jnp.where(j <= qi, qi - j, qi - j + BLOCK)
    dist_meta = jnp.where(j < N_META, N_META + qi - j, 0.0)
    shape = (SWA_KV_HEADS, group * BLOCK, BLOCK)
    return ((slope * dist_band).reshape(shape), (slope * dist_meta).reshape(shape),
            jnp.broadcast_to(slope * BLOCK, slope.shape[:3] + (BLOCK,)).reshape(shape))


def _swa_kernel(q_ref, kc_ref, kp_ref, vc_ref, vp_ref, km_ref, vm_ref, qg_ref, kg_ref,
                band_ref, meta_ref, step_ref, sink_ref, o_ref):
    n = pl.program_id(1)
    group = SWA_HEADS // SWA_KV_HEADS
    rows = group * BLOCK
    qn = [(x * (SCALE * LOG2E)).astype(BF16) for x in _head_rmsnorm(q_ref[...], qg_ref[...])]
    kcat = jnp.concatenate([_pad_rows(km_ref[...], BLOCK), kp_ref[...], kc_ref[...]], axis=0)
    kn = jnp.concatenate(_head_rmsnorm(kcat, kg_ref[...]), axis=1).astype(BF16)
    vcat = jnp.concatenate([_pad_rows(vm_ref[...], BLOCK), vp_ref[...], vc_ref[...]], axis=0).astype(BF16)

    qi = _iota((rows, BLOCK), 0) % BLOCK
    j = _iota((rows, BLOCK), 1)
    in_cur = j <= qi
    band_ok = in_cur | (n > 0)
    meta_ok = j < N_META
    blocks_before = n.astype(F32)

    kvs = [(_dup_kv(kn, g), _dup_kv(vcat, g)) for g in range(SWA_KV_HEADS)]
    logits = [_dot_nt(jnp.concatenate(_lane_halves(qn[2 * g]) + _lane_halves(qn[2 * g + 1]), axis=0),
                      kvs[g][0]) for g in range(SWA_KV_HEADS)]
    weights, dens = [], []
    for g in range(SWA_KV_HEADS):
        sink = sink_ref[g]
        s = logits[g]
        s_band = jnp.where(in_cur, s[:, 2 * BLOCK:], s[:, BLOCK:2 * BLOCK]) - band_ref[g]
        s_band = jnp.where(band_ok, s_band, NEG)
        s_meta = jnp.where(meta_ok, (s[:, :BLOCK] - meta_ref[g]) - step_ref[g] * blocks_before, NEG)
        mx = jnp.maximum(jnp.maximum(jnp.max(s_band, axis=-1, keepdims=True),
                                     jnp.max(s_meta, axis=-1, keepdims=True)), sink)
        e_band = jnp.exp2(s_band - mx)
        e_meta = jnp.exp2(s_meta - mx)
        dens.append(jnp.sum(e_band, axis=-1, keepdims=True) + jnp.sum(e_meta, axis=-1, keepdims=True)
                    + jnp.exp2(sink - mx))
        weights.append(jnp.concatenate(
            [e_meta, jnp.where(in_cur, 0.0, e_band), jnp.where(in_cur, e_band, 0.0)], axis=1).astype(BF16))
    for g in range(SWA_KV_HEADS):
        o = _dot(weights[g], kvs[g][1]) / dens[g]
        for t in range(2):
            o_ref[:, (2 * g + t) * LANES:(2 * g + t + 1) * LANES] = _merge_halves(
                o[2 * t * BLOCK:(2 * t + 1) * BLOCK], o[(2 * t + 1) * BLOCK:(2 * t + 2) * BLOCK])


def _swa(proj, proj_meta, sinks, qg, kg, tables, batch):
    m = proj.shape[0]
    nb = m // batch // BLOCK
    qw = SWA_HEADS * HEAD_DIM
    kvw = SWA_KV_HEADS * HEAD_DIM
    group = SWA_HEADS // SWA_KV_HEADS
    kcol, vcol = qw // kvw, qw // kvw + 1
    row = lambda b, n: b * nb + n
    whole = lambda shape: pl.BlockSpec(shape, lambda b, n: (0,) * len(shape))
    sink_col = jnp.broadcast_to((sinks * LOG2E).reshape(SWA_KV_HEADS, group, 1, 1),
                                (SWA_KV_HEADS, group, BLOCK, LANES))
    sink_col = sink_col.reshape(SWA_KV_HEADS, group * BLOCK, LANES)
    band, meta, step = tables
    return pl.pallas_call(
        _swa_kernel,
        grid=(batch, nb),
        in_specs=[
            pl.BlockSpec((BLOCK, qw), lambda b, n: (row(b, n), 0)),
            pl.BlockSpec((BLOCK, kvw), lambda b, n: (row(b, n), kcol)),
            pl.BlockSpec((BLOCK, kvw), lambda b, n: (row(b, jnp.maximum(n - 1, 0)), kcol)),
            pl.BlockSpec((BLOCK, kvw), lambda b, n: (row(b, n), vcol)),
            pl.BlockSpec((BLOCK, kvw), lambda b, n: (row(b, jnp.maximum(n - 1, 0)), vcol)),
            pl.BlockSpec((N_META, kvw), lambda b, n: (0, kcol)),
            pl.BlockSpec((N_META, kvw), lambda b, n: (0, vcol)),
            pl.BlockSpec((1, qw), lambda b, n: (0, 0)),
            pl.BlockSpec((1, kvw), lambda b, n: (0, 0)),
            whole(band.shape), whole(meta.shape), whole(step.shape), whole(sink_col.shape),
        ],
        out_specs=pl.BlockSpec((BLOCK, qw), lambda b, n: (row(b, n), 0)),
        out_shape=jax.ShapeDtypeStruct((m, qw), F32),
        compiler_params=_cparams(("arbitrary", "arbitrary")),
        name="swa",
    )(proj, proj, proj, proj, proj, proj_meta, proj_meta, qg, kg, band, meta, step, sink_col)


def _sb_kernel(q_ref, k_ref, v_ref, km_ref, vm_ref, o_ref,
               kbf_ref, vbf_ref, kmp_ref, vmp_ref, acc_ref, car_ref, more_ref, *, tq, hp):
    i = pl.program_id(2)

    @pl.when(i == 0)
    def _():
        kbf_ref[...] = k_ref[...].astype(BF16)
        vbf_ref[...] = v_ref[...].astype(BF16)
        kmp_ref[...] = jnp.zeros_like(kmp_ref)
        vmp_ref[...] = jnp.zeros_like(vmp_ref)
        kmp_ref[0:N_META, :] = km_ref[...].astype(BF16)
        vmp_ref[0:N_META, :] = vm_ref[...].astype(BF16)

    lanes = lambda p: slice(p * LANES, (p + 1) * LANES)
    rows = 2 * tq

    def queries():
        return [jnp.concatenate(_lane_halves((q_ref[:, lanes(p)] * SB_QSCALE).astype(BF16)), axis=0)
                for p in range(hp)]

    def note_carries(carries):
        m = functools.reduce(jnp.minimum, carries)
        while m.shape[0] > 8:
            half = m.shape[0] // 2
            m = jnp.minimum(m[:half], m[half:])
        more_ref[0] = (jnp.min(m) <= F32_UNDERFLOW_LOG2).astype(jnp.int32)

    def block_step(kv_of_pair, width, mask, first):
        qs = queries()
        kv = [kv_of_pair(p) for p in range(hp)]
        z2 = [_dot_nt(qs[p], kv[p][0]) for p in range(hp)]
        sp = [_softplus2(z) for z in z2]
        if mask is not None:
            sp = [jnp.where(mask, s, 0.0) for s in sp]
        zs = [z2[p] - sp[p] for p in range(hp)]
        later = _dot(jnp.concatenate([s.astype(BF16) for s in sp], axis=0), _suffix_matrix(width))
        carries = []
        for p in range(hp):
            carry = 0.0 if first else car_ref[p]
            a = jnp.exp2(zs[p] - (later[p * rows:(p + 1) * rows] + carry))
            if mask is not None:
                a = jnp.where(mask, a, 0.0)
            d = _dot(a.astype(BF16), kv[p][1])
            acc_ref[p] = d if first else acc_ref[p] + d
            carries.append(carry + jnp.sum(sp[p], axis=-1, keepdims=True))
            car_ref[p] = carries[-1]
        note_carries(carries)

    def kv_block(start):
        return lambda p: (kbf_ref[pl.ds(start, tq), lanes(p)], vbf_ref[pl.ds(start, tq), lanes(p)])

    below_diag = lambda: _iota((rows, tq), 1) < (_iota((rows, tq), 0) % tq)
    prev_start = pl.multiple_of(jnp.maximum(i - 1, 0) * tq, tq)

    @pl.when(i == 0)
    def _():
        block_step(kv_block(0), tq, below_diag(), True)

    @pl.when(i > 0)
    def _():
        qs = queries()
        in_cur = below_diag()
        cur_start = pl.multiple_of(i * tq, tq)
        zs, spb = [], []
        for p in range(hp):
            z_cur = _dot_nt(qs[p], kbf_ref[pl.ds(cur_start, tq), lanes(p)])
            z_prev = _dot_nt(qs[p], kbf_ref[pl.ds(prev_start, tq), lanes(p)])
            z2 = jnp.where(in_cur, z_cur, z_prev)
            sp = _softplus2(z2)
            zs.append(z2 - sp)
            spb.append(sp.astype(BF16))
        zero = jnp.zeros((rows, tq), BF16)
        prev_part = lambda x: jnp.where(in_cur, zero, x)
        cur_part = lambda x: jnp.where(in_cur, x, zero)
        suf = _dot(jnp.concatenate(spb, axis=0), _suffix_matrix(tq))
        prev_tot = _dot(jnp.concatenate([prev_part(s) for s in spb], axis=0), jnp.ones((tq, LANES), BF16))
        carries = []
        for p in range(hp):
            sf = suf[p * rows:(p + 1) * rows]
            ptw = jnp.concatenate([prev_tot[p * rows:(p + 1) * rows]] * (tq // LANES), axis=1)
            total = sf[:, 0:1] + spb[p][:, 0:1].astype(F32)
            later = sf + jnp.where(in_cur, -ptw, total - ptw)
            a = jnp.exp2(zs[p] - later).astype(BF16)
            v2 = jnp.concatenate([vbf_ref[pl.ds(prev_start, tq), lanes(p)],
                                  vbf_ref[pl.ds(cur_start, tq), lanes(p)]], axis=0)
            acc_ref[p] = _dot(jnp.concatenate([prev_part(a), cur_part(a)], axis=1), v2)
            car_ref[p] = total
            carries.append(total)
        note_carries(carries)

    @pl.when((i > 0) & (more_ref[0] > 0))
    def _():
        block_step(kv_block(prev_start), tq, below_diag(), False)

    def body(t):
        block_step(kv_block(pl.multiple_of((i - 2 - t) * tq, tq)), tq, None, False)
        return t + 1

    lax.while_loop(lambda t: (t < i - 1) & (more_ref[0] > 0), body, jnp.int32(0))

    @pl.when(more_ref[0] > 0)
    def _():
        block_step(lambda p: (kmp_ref[:, lanes(p)], vmp_ref[:, lanes(p)]), LANES,
                   _iota((rows, LANES), 1) < N_META, False)

    for p in range(hp):
        o_ref[:, lanes(p)] = _merge_halves(acc_ref[p, 0:tq], acc_ref[p, tq:rows])


def _sb(proj, proj_meta, batch, tq, hp):
    m = proj.shape[0]
    seq = m // batch
    nq = seq // tq
    w = hp * LANES
    qcol = (SWA_HEADS + 2 * SWA_KV_HEADS) * HEAD_DIM // w
    steps = SB_HEADS * HEAD_DIM // w
    kcol, vcol = qcol + steps, qcol + 2 * steps
    return pl.pallas_call(
        functools.partial(_sb_kernel, tq=tq, hp=hp),
        grid=(batch, steps, nq),
        in_specs=[
            pl.BlockSpec((tq, w), lambda b, p, i: (b * nq + i, qcol + p)),
            pl.BlockSpec((seq, w), lambda b, p, i: (b, kcol + p)),
            pl.BlockSpec((seq, w), lambda b, p, i: (b, vcol + p)),
            pl.BlockSpec((N_META, w), lambda b, p, i: (0, kcol + p)),
            pl.BlockSpec((N_META, w), lambda b, p, i: (0, vcol + p)),
        ],
        out_specs=pl.BlockSpec((tq, w), lambda b, p, i: (b * nq + i, p)),
        out_shape=jax.ShapeDtypeStruct((m, SB_HEADS * HEAD_DIM), F32),
        scratch_shapes=[pltpu.VMEM((seq, w), BF16), pltpu.VMEM((seq, w), BF16),
                        pltpu.VMEM((LANES, w), BF16), pltpu.VMEM((LANES, w), BF16),
                        pltpu.VMEM((hp, 2 * tq, LANES), F32), pltpu.VMEM((hp, 2 * tq, 1), F32),
                        pltpu.SMEM((1,), jnp.int32)],
        compiler_params=_cparams(("arbitrary", "arbitrary", "arbitrary")),
        name="stickbreak",
    )(proj, proj, proj, proj_meta, proj_meta)


def _meta_attn_kernel(sink_ref, p_ref, qg_ref, kg_ref, oa_ref, ob_ref):
    qw = SWA_HEADS * HEAD_DIM
    kvw = SWA_KV_HEADS * HEAD_DIM
    group = SWA_HEADS // SWA_KV_HEADS
    x = p_ref[...]
    xk = _pad_rows(x, LANES)
    qi = _iota((N_META, LANES), 0)
    kj = _iota((N_META, LANES), 1)

    qn = [(t * SCALE).astype(BF16) for t in _head_rmsnorm(x[:, 0:qw], qg_ref[...])]
    kn = jnp.concatenate(_head_rmsnorm(xk[:, qw:qw + kvw], kg_ref[...]), axis=1).astype(BF16)
    vb = xk[:, qw + kvw:qw + 2 * kvw].astype(BF16)
    delta_f = (qi - kj).astype(F32)
    causal = kj <= qi
    for pair in range(SWA_HEADS // 2):
        kv = (2 * pair) // group
        kdup, vdup = _dup_kv(kn, kv), _dup_kv(vb, kv)
        outs = []
        for e, qe in enumerate(_lane_halves(qn[pair])):
            h = 2 * pair + e
            outs.append(_swa_head(qe, kdup, vdup, _alibi_slope(h) * delta_f, causal, sink_ref[h]))
        oa_ref[:, pair * LANES:(pair + 1) * LANES] = _merge_halves(outs[0], outs[1])

    base = qw + 2 * kvw
    sbw = SB_HEADS * HEAD_DIM
    strict = kj < qi
    suffix = _suffix_matrix(LANES)
    zero_c = jnp.zeros((N_META, 1), F32)
    for pair in range(SB_HEADS // 2):
        sl = lambda src, part: src[:, base + part * sbw + pair * LANES: base + part * sbw + (pair + 1) * LANES]
        kb, vb2 = sl(xk, 1).astype(BF16), sl(xk, 2).astype(BF16)
        outs = [_sb_block(qe, kb, vb2, suffix, zero_c, strict)[0]
                for qe in _lane_halves((sl(x, 0) * SB_QSCALE).astype(BF16))]
        ob_ref[:, pair * LANES:(pair + 1) * LANES] = _merge_halves(outs[0], outs[1])


def _meta_attn(proj_meta, sinks, qg, kg, batch):
    pw = proj_meta.shape[1]
    qw = SWA_HEADS * HEAD_DIM
    kvw = SWA_KV_HEADS * HEAD_DIM
    sbw = SB_HEADS * HEAD_DIM
    return pl.pallas_call(
        _meta_attn_kernel,
        grid=(batch,),
        in_specs=[
            pl.BlockSpec(memory_space=pltpu.SMEM),
            pl.BlockSpec((N_META, pw), lambda b: (b, 0)),
            pl.BlockSpec((1, qw), lambda b: (0, 0)),
            pl.BlockSpec((1, kvw), lambda b: (0, 0)),
        ],
        out_specs=[pl.BlockSpec((N_META, qw), lambda b: (b, 0)),
                   pl.BlockSpec((N_META, sbw), lambda b: (b, 0))],
        out_shape=[jax.ShapeDtypeStruct((batch * N_META, qw), F32),
                   jax.ShapeDtypeStruct((batch * N_META, sbw), F32)],
        compiler_params=_cparams(("arbitrary",)),
        name="meta_attn",
    )(sinks, proj_meta, qg, kg)


def kernel(x, meta_tokens, attn_norm_g, w_in, q_norm_g, k_norm_g, attn_sinks,
           swa_out_g, sb_out_g, w_o, ffn_norm_g, w_gate, w_up, w_down):
    batch, seq, d = x.shape
    depth = w_in.shape[0]
    qw = SWA_HEADS * HEAD_DIM
    assert seq % 512 == 0 and meta_tokens.shape[0] == N_META

    tm = 512
    tmeta = N_META
    tm_in = 1024
    tn_in = 1536
    tn_out = 2048
    tf = 512
    tq = 256
    hp = 4

    h = x.reshape(batch * seq, d)
    hm = meta_tokens.astype(x.dtype)

    w_in_b, w_o_b = w_in.astype(BF16), w_o.astype(BF16)
    w_gate_b, w_up_b, w_down_b = w_gate.astype(BF16), w_up.astype(BF16), w_down.astype(BF16)

    swa_tables = _swa_tables()
    for l in range(depth):
        g_attn =attn_norm_g[l].reshape(1, d)
        qg = jnp.tile(q_norm_g[l], SWA_HEADS).reshape(1, -1)
        kg = jnp.tile(k_norm_g[l], SWA_KV_HEADS).reshape(1, -1)
        sinks = attn_sinks[l].astype(F32)
        ga, gb = swa_out_g[l].reshape(1, -1), sb_out_g[l].reshape(1, -1)
        g_ffn = ffn_norm_g[l].reshape(1, d)

        proj = _norm_matmul(h, g_attn, w_in_b, l, tm_in, tn_in)
        proj_m = _norm_matmul(hm, g_attn, w_in_b, l, tmeta, tn_in)

        out_a = _swa(proj, proj_m, sinks, qg, kg, swa_tables, batch)
        out_b = _sb(proj, proj_m, batch, tq, hp)
        out_am, out_bm = _meta_attn(proj_m, sinks, qg, kg, 1)

        h = _outproj(out_a, out_b, ga, gb, w_o_b, l, h, tm, tn_out)
        hm = _outproj(out_am, out_bm, ga, gb, w_o_b, l, hm, tmeta, tn_out)

        h = _ffn(h, g_ffn, w_gate_b, w_up_b, w_down_b, l, tm, tf)
        hm = _ffn(hm, g_ffn, w_gate_b, w_up_b, w_down_b, l, tmeta, tf)

    return h.reshape(batch, seq, d)
```

```python
import functools

import jax
import jax.numpy as jnp
from jax import lax
from jax.experimental import pallas as pl
from jax.experimental.pallas import tpu as pltpu

F32 = jnp.float32
BF16 = jnp.bfloat16

N_META = 16
BLOCK = 128
HEAD_DIM = 64
LANES = 128
SWA_HEADS = 16
SWA_KV_HEADS = 4
SB_HEADS = 16
EPS = 1e-6
NEG = -1e30
SCALE = HEAD_DIM ** -0.5
LOG2E = 1.4426950408889634
SB_QSCALE = SCALE * LOG2E
F32_UNDERFLOW_LOG2 = 160.0
VMEM_LIMIT = 56 * 1024 * 1024


def _cparams(sem):
    return pltpu.CompilerParams(dimension_semantics=sem, vmem_limit_bytes=VMEM_LIMIT)


def _dot(a, b):
    return jnp.dot(a, b, preferred_element_type=F32)


def _dot_nt(a, b):
    return lax.dot_general(a, b, (((1,), (1,)), ((), ())), preferred_element_type=F32)


def _iota(shape, dim):
    return lax.broadcasted_iota(jnp.int32, shape, dim)


def _norm_mm_kernel(x_ref, g_ref, w_ref, o_ref, xn_ref):
    @pl.when(pl.program_id(1) == 0)
    def _():
        x = x_ref[...]
        ms = jnp.mean(x * x, axis=-1, keepdims=True)
        xn_ref[...] = ((x * lax.rsqrt(ms + EPS)) * g_ref[...]).astype(BF16)

    o_ref[...] = _dot(xn_ref[...], w_ref[...])


def _norm_matmul(x, g, w, layer, tm, tn):
    m, d = x.shape
    n = w.shape[2]
    return pl.pallas_call(
        _norm_mm_kernel,
        grid=(m // tm, n // tn),
        in_specs=[
            pl.BlockSpec((tm, d), lambda i, j: (i, 0)),
            pl.BlockSpec((1, d), lambda i, j: (0, 0)),
            pl.BlockSpec((None, d, tn), lambda i, j: (layer, 0, j)),
        ],
        out_specs=pl.BlockSpec((tm, tn), lambda i, j: (i, j)),
        out_shape=jax.ShapeDtypeStruct((m, n), F32),
        scratch_shapes=[pltpu.VMEM((tm, d), BF16)],
        compiler_params=_cparams(("arbitrary", "arbitrary")),
        name="norm_inproj",
    )(x, g, w)


def _outproj_kernel(a_ref, b_ref, ga_ref, gb_ref, wa_ref, wb_ref, h_ref, o_ref, an_ref, bn_ref):
    @pl.when(pl.program_id(1) == 0)
    def _():
        for src, g, dst in ((a_ref, ga_ref, an_ref), (b_ref, gb_ref, bn_ref)):
            x = src[...]
            ms = jnp.mean(x * x, axis=-1, keepdims=True)
            dst[...] = ((x * lax.rsqrt(ms + EPS)) * g[...]).astype(BF16)

    o_ref[...] = h_ref[...] + (_dot(an_ref[...], wa_ref[...]) + _dot(bn_ref[...], wb_ref[...]))


def _outproj(a, b, ga, gb, w, layer, h, tm, tn):
    m, wdt = a.shape
    d = h.shape[1]
    return pl.pallas_call(
        _outproj_kernel,
        grid=(m // tm, d // tn),
        in_specs=[
            pl.BlockSpec((tm, wdt), lambda i, j: (i, 0)),
            pl.BlockSpec((tm, wdt), lambda i, j: (i, 0)),
            pl.BlockSpec((1, wdt), lambda i, j: (0, 0)),
            pl.BlockSpec((1, wdt), lambda i, j: (0, 0)),
            pl.BlockSpec((None, wdt, tn), lambda i, j: (layer, 0, j)),
            pl.BlockSpec((None, wdt, tn), lambda i, j: (layer, 1, j)),
            pl.BlockSpec((tm, tn), lambda i, j: (i, j)),
        ],
        out_specs=pl.BlockSpec((tm, tn), lambda i, j: (i, j)),
        out_shape=jax.ShapeDtypeStruct((m, d), F32),
        scratch_shapes=[pltpu.VMEM((tm, wdt), BF16), pltpu.VMEM((tm, wdt), BF16)],
        compiler_params=_cparams(("arbitrary", "arbitrary")),
        name="outproj",
    )(a, b, ga, gb, w, w, h)


def _ffn_kernel(x_ref, g_ref, wg_ref, wu_ref, wd_ref, o_ref, xn_ref):
    @pl.when(pl.program_id(1) == 0)
    def _():
        x = x_ref[...]
        ms = jnp.mean(x * x, axis=-1, keepdims=True)
        xn_ref[...] = ((x * lax.rsqrt(ms + EPS)) * g_ref[...]).astype(BF16)
        o_ref[...] = x

    xn = xn_ref[...]
    gate = _dot(xn, wg_ref[...])
    up = _dot(xn, wu_ref[...])
    act = (gate / (1.0 + jnp.exp(-gate))) * up
    o_ref[...] += _dot(act.astype(BF16), wd_ref[...])


def _ffn(x, g, wg, wu, wd, layer, tm, tf):
    m, d = x.shape
    f = wg.shape[2]
    return pl.pallas_call(
        _ffn_kernel,
        grid=(m // tm, f // tf),
        in_specs=[
            pl.BlockSpec((tm, d), lambda i, j: (i, 0)),
            pl.BlockSpec((1, d), lambda i, j: (0, 0)),
            pl.BlockSpec((None, d, tf), lambda i, j: (layer, 0, j)),
            pl.BlockSpec((None, d, tf), lambda i, j: (layer, 0, j)),
            pl.BlockSpec((None, tf, d), lambda i, j: (layer, j, 0)),
        ],
        out_specs=pl.BlockSpec((tm, d), lambda i, j: (i, 0)),
        out_shape=jax.ShapeDtypeStruct((m, d), F32),
        scratch_shapes=[pltpu.VMEM((tm, d), BF16)],
        compiler_params=_cparams(("arbitrary", "arbitrary")),
        name="ffn",
    )(x, g, wg, wu, wd)


def _head_rmsnorm(x, g):
    r, c = x.shape
    same_head = (_iota((LANES, LANES), 0) // HEAD_DIM) == (_iota((LANES, LANES), 1) // HEAD_DIM)
    bd = jnp.where(same_head, 1.0, 0.0).astype(BF16)
    outs = []
    for t in range(c // LANES):
        xb = x[:, t * LANES:(t + 1) * LANES]
        ss = _dot((xb * xb).astype(BF16), bd)
        outs.append((xb * lax.rsqrt(ss * (1.0 / HEAD_DIM) + EPS)) * g[:, t * LANES:(t + 1) * LANES])
    return outs


def _half_select(half):
    src = _iota((LANES, LANES), 0)
    dst = _iota((LANES, LANES), 1)
    return jnp.where(src == (dst % HEAD_DIM) + HEAD_DIM * half, 1.0, 0.0).astype(BF16)


def _lane_halves(x):
    lo_half = _iota(x.shape, 1) < HEAD_DIM
    zero = jnp.zeros_like(x)
    return jnp.where(lo_half, x, zero), jnp.where(lo_half, zero, x)


def _merge_halves(lo, hi):
    return jnp.where(_iota(lo.shape, 1) < HEAD_DIM, lo, hi)


def _alibi_slope(h):
    return 2.0 ** (-8.0 * (h + 1) / SWA_HEADS)


def _softplus2(z2):
    neg_abs = lax.bitcast_convert_type(
        lax.bitcast_convert_type(z2, jnp.uint32) | jnp.uint32(0x80000000), F32)
    return jnp.maximum(z2, 0.0) + jnp.log2(1.0 + jnp.exp2(neg_abs))


def _suffix_matrix(n):
    return jnp.where(_iota((n, n), 0) > _iota((n, n), 1), 1.0, 0.0).astype(BF16)


def _sb_block(qe, kblk, vblk, suffix, carry, mask):
    z2 = _dot_nt(qe, kblk)
    sp = _softplus2(z2)
    if mask is not None:
        sp = jnp.where(mask, sp, 0.0)
    later = _dot(sp.astype(BF16), suffix)
    a = jnp.exp2((z2 - sp) - (later + carry))
    if mask is not None:
        a = jnp.where(mask, a, 0.0)
    contrib = _dot(a.astype(BF16), vblk)
    return contrib, carry + jnp.sum(sp, axis=-1, keepdims=True)


def _swa_head(qe, kdup, vdup, bias, mask, sink):
    s = jnp.where(mask, _dot_nt(qe, kdup) - bias, NEG)
    mx = jnp.maximum(jnp.max(s, axis=-1, keepdims=True), sink)
    e = jnp.exp(s - mx)
    den = jnp.sum(e, axis=-1, keepdims=True) + jnp.exp(sink - mx)
    return _dot(e.astype(BF16), vdup) / den


def _pad_rows(x, rows):
    return jnp.concatenate([x, jnp.zeros((rows - x.shape[0], x.shape[1]), x.dtype)], axis=0)


def _dup_kv(x_bf, kv_head):
    t = kv_head // 2
    return _dot(x_bf[:, t * LANES:(t + 1) * LANES], _half_select(kv_head % 2)).astype(BF16)


def _swa_tables():
    group = SWA_HEADS // SWA_KV_HEADS
    slope = jnp.asarray([_alibi_slope(h) * LOG2E for h in range(SWA_HEADS)], F32)
    slope = jnp.repeat(slope.reshape(SWA_KV_HEADS, group, 1, 1), BLOCK, axis=2)
    qi = jnp.arange(BLOCK, dtype=F32)[:, None]
    j = jnp.arange(BLOCK, dtype=F32)[None, :]
    dist_band = jnp.where(j <= qi, qi - j, qi - j + BLOCK)
    dist_meta = jnp.where(j < N_META, N_META + qi - j, 0.0)
    shape = (SWA_KV_HEADS, group * BLOCK, BLOCK)
    return ((slope * dist_band).reshape(shape), (slope * dist_meta).reshape(shape),
            jnp.broadcast_to(slope * BLOCK, slope.shape[:3] + (BLOCK,)).reshape(shape))


def _swa_kernel(q_ref, kc_ref, vc_ref, km_ref, vm_ref, qg_ref, kg_ref,
                band_ref, meta_ref, step_ref, sink_ref, o_ref, kd_ref, vd_ref):
    n = pl.program_id(1)
    group = SWA_HEADS // SWA_KV_HEADS
    rows = group * BLOCK

    def store_rows(k_rows, v_rows, start):
        kn = jnp.concatenate(_head_rmsnorm(k_rows, kg_ref[...]), axis=1).astype(BF16)
        vb = v_rows.astype(BF16)
        for g in range(SWA_KV_HEADS):
            kd_ref[g, start:start + BLOCK] = _dup_kv(kn, g)
            vd_ref[g, start:start + BLOCK] = _dup_kv(vb, g)

    @pl.when(n == 0)
    def _():
        store_rows(_pad_rows(km_ref[...], BLOCK), _pad_rows(vm_ref[...], BLOCK), 0)
        kd_ref[:, 2 * BLOCK:3 * BLOCK] = jnp.zeros((SWA_KV_HEADS, BLOCK, LANES), BF16)
        vd_ref[:, 2 * BLOCK:3 * BLOCK] = jnp.zeros((SWA_KV_HEADS, BLOCK, LANES), BF16)

    kd_ref[:, BLOCK:2 * BLOCK] = kd_ref[:, 2 * BLOCK:3 * BLOCK]
    vd_ref[:, BLOCK:2 * BLOCK] = vd_ref[:, 2 * BLOCK:3 * BLOCK]
    store_rows(kc_ref[...], vc_ref[...], 2 * BLOCK)
    qn = [(x * (SCALE * LOG2E)).astype(BF16) for x in _head_rmsnorm(q_ref[...], qg_ref[...])]

    qi = _iota((rows, BLOCK), 0) % BLOCK
    j = _iota((rows, BLOCK), 1)
    in_cur = j <= qi
    band_ok = in_cur | (n > 0)
    meta_ok = j < N_META
    blocks_before = n.astype(F32)

    logits = [_dot_nt(jnp.concatenate(_lane_halves(qn[2 * g]) + _lane_halves(qn[2 * g + 1]), axis=0),
                      kd_ref[g]) for g in range(SWA_KV_HEADS)]
    weights, dens = [], []
    for g in range(SWA_KV_HEADS):
        sink = sink_ref[g]
        s = logits[g]
        s_band = jnp.where(in_cur, s[:, 2 * BLOCK:], s[:, BLOCK:2 * BLOCK]) - band_ref[g]
        s_band = jnp.where(band_ok, s_band, NEG)
        s_meta = jnp.where(meta_ok, (s[:, :BLOCK] - meta_ref[g]) - step_ref[g] * blocks_before, NEG)
        mx = jnp.maximum(jnp.max(jnp.maximum(s_band, s_meta), axis=-1, keepdims=True), sink)
        e_band = jnp.exp2(s_band - mx)
        e_meta = jnp.exp2(s_meta - mx)
        dens.append(jnp.sum(e_band + e_meta, axis=-1, keepdims=True) + jnp.exp2(sink - mx))
        weights.append(jnp.concatenate(
            [e_meta, jnp.where(in_cur, 0.0, e_band), jnp.where(in_cur, e_band, 0.0)], axis=1).astype(BF16))
    for g in range(SWA_KV_HEADS):
        o = _dot(weights[g], vd_ref[g]) / dens[g]
        for t in range(2):
            o_ref[:, (2 * g + t) * LANES:(2 * g + t + 1) * LANES] = _merge_halves(
                o[2 * t * BLOCK:(2 * t + 1) * BLOCK], o[(2 * t + 1) * BLOCK:(2 * t + 2) * BLOCK])


def _swa(proj, proj_meta, sinks, qg, kg, tables, batch):
    m = proj.shape[0]
    nb = m // batch // BLOCK
    qw = SWA_HEADS * HEAD_DIM
    kvw = SWA_KV_HEADS * HEAD_DIM
    group = SWA_HEADS // SWA_KV_HEADS
    kcol, vcol = qw // kvw, qw // kvw + 1
    row = lambda b, n: b * nb + n
    whole = lambda shape: pl.BlockSpec(shape, lambda b, n: (0,) * len(shape))
    sink_col = jnp.broadcast_to((sinks * LOG2E).reshape(SWA_KV_HEADS, group, 1, 1),
                                (SWA_KV_HEADS, group, BLOCK, LANES))
    sink_col = sink_col.reshape(SWA_KV_HEADS, group * BLOCK, LANES)
    band, meta, step = tables
    return pl.pallas_call(
        _swa_kernel,
        grid=(batch, nb),
        in_specs=[
            pl.BlockSpec((BLOCK, qw), lambda b, n: (row(b, n), 0)),
            pl.BlockSpec((BLOCK, kvw), lambda b, n: (row(b, n), kcol)),
            pl.BlockSpec((BLOCK, kvw), lambda b, n: (row(b, n), vcol)),
            pl.BlockSpec((N_META, kvw), lambda b, n: (0, kcol)),
            pl.BlockSpec((N_META, kvw), lambda b, n: (0, vcol)),
            pl.BlockSpec((1, qw), lambda b, n: (0, 0)),
            pl.BlockSpec((1, kvw), lambda b, n: (0, 0)),
            whole(band.shape), whole(meta.shape), whole(step.shape), whole(sink_col.shape),
        ],
        out_specs=pl.BlockSpec((BLOCK, qw), lambda b, n: (row(b, n), 0)),
        out_shape=jax.ShapeDtypeStruct((m, qw), F32),
        scratch_shapes=[pltpu.VMEM((SWA_KV_HEADS, 3 * BLOCK, LANES), BF16),
                        pltpu.VMEM((SWA_KV_HEADS, 3 * BLOCK, LANES), BF16)],
        compiler_params=_cparams(("arbitrary", "arbitrary")),
        name="swa",
    )(proj, proj, proj, proj_meta, proj_meta, qg, kg, band, meta, step, sink_col)


def _sb_kernel(q_ref, k_ref, v_ref, km_ref, vm_ref, o_ref,
               kbf_ref, vbf_ref, kmp_ref, vmp_ref, acc_ref, car_ref, more_ref, *, tq, hp):
    i = pl.program_id(2)

    @pl.when(i == 0)
    def _():
        kbf_ref[...] = k_ref[...].astype(BF16)
        vbf_ref[...] = v_ref[...].astype(BF16)
        kmp_ref[...] = jnp.zeros_like(kmp_ref)
        vmp_ref[...] = jnp.zeros_like(vmp_ref)
        kmp_ref[0:N_META, :] = km_ref[...].astype(BF16)
        vmp_ref[0:N_META, :] = vm_ref[...].astype(BF16)

    lanes = lambda p: slice(p * LANES, (p + 1) * LANES)
    rows = 2 * tq

    def queries():
        return [jnp.concatenate(_lane_halves((q_ref[:, lanes(p)] * SB_QSCALE).astype(BF16)), axis=0)
                for p in range(hp)]

    def note_carries(carries):
        m = functools.reduce(jnp.minimum, carries)
        while m.shape[0] > 8:
            half = m.shape[0] // 2
            m = jnp.minimum(m[:half], m[half:])
        more_ref[0] = (jnp.min(m) <= F32_UNDERFLOW_LOG2).astype(jnp.int32)

    def block_step(kv_of_pair, width, mask, first):
        qs = queries()
        kv = [kv_of_pair(p) for p in range(hp)]
        z2 = [_dot_nt(qs[p], kv[p][0]) for p in range(hp)]
        sp = [_softplus2(z) for z in z2]
        if mask is not None:
            sp = [jnp.where(mask, s, 0.0) for s in sp]
        zs = [z2[p] - sp[p] for p in range(hp)]
        later = _dot(jnp.concatenate([s.astype(BF16) for s in sp], axis=0), _suffix_matrix(width))
        carries = []
        for p in range(hp):
            carry = 0.0 if first else car_ref[p]
            a = jnp.exp2(zs[p] - (later[p * rows:(p + 1) * rows] + carry))
            if mask is not None:
                a = jnp.where(mask, a, 0.0)
            d = _dot(a.astype(BF16), kv[p][1])
            acc_ref[p] = d if first else acc_ref[p] + d
            carries.append(carry + jnp.sum(sp[p], axis=-1, keepdims=True))
            car_ref[p] = carries[-1]
        note_carries(carries)

    def kv_block(start):
        return lambda p: (kbf_ref[pl.ds(start, tq), lanes(p)], vbf_ref[pl.ds(start, tq), lanes(p)])

    below_diag = lambda: _iota((rows, tq), 1) < (_iota((rows, tq), 0) % tq)
    prev_start = pl.multiple_of(jnp.maximum(i - 1, 0) * tq, tq)

    @pl.when(i == 0)
    def _():
        block_step(kv_block(0), tq, below_diag(), True)

    @pl.when(i > 0)
    def _():
        qs = queries()
        in_cur = below_diag()
        cur_start = pl.multiple_of(i * tq, tq)
        zs, spb = [], []
        for p in range(hp):
            z_cur = _dot_nt(qs[p], kbf_ref[pl.ds(cur_start, tq), lanes(p)])
            z_prev = _dot_nt(qs[p], kbf_ref[pl.ds(prev_start, tq), lanes(p)])
            z2 = jnp.where(in_cur, z_cur, z_prev)
            sp = _softplus2(z2)
            zs.append(z2 - sp)
            spb.append(sp.astype(BF16))
        zero = jnp.zeros((rows, tq), BF16)
        prev_part = lambda x: jnp.where(in_cur, zero, x)
        cur_part = lambda x: jnp.where(in_cur, x, zero)
        suf = _dot(jnp.concatenate(spb, axis=0), _suffix_matrix(tq))
        prev_tot = _dot(jnp.concatenate([prev_part(s) for s in spb], axis=0), jnp.ones((tq, LANES), BF16))
        carries = []
        for p in range(hp):
            sf = suf[p * rows:(p + 1) * rows]
            ptw = jnp.concatenate([prev_tot[p * rows:(p + 1) * rows]] * (tq // LANES), axis=1)
            total = sf[:, 0:1] + spb[p][:, 0:1].astype(F32)
            later = sf + jnp.where(in_cur, -ptw, total - ptw)
            a = jnp.exp2(zs[p] - later).astype(BF16)
            v2 = jnp.concatenate([vbf_ref[pl.ds(prev_start, tq), lanes(p)],
                                  vbf_ref[pl.ds(cur_start, tq), lanes(p)]], axis=0)
            acc_ref[p] = _dot(jnp.concatenate([prev_part(a), cur_part(a)], axis=1), v2)
            car_ref[p] = total
            carries.append(total)
        note_carries(carries)

    @pl.when((i > 0) & (more_ref[0] > 0))
    def _():
        block_step(kv_block(prev_start), tq, below_diag(), False)

    def body(t):
        block_step(kv_block(pl.multiple_of((i - 2 - t) * tq, tq)), tq, None, False)
        return t + 1

    lax.while_loop(lambda t: (t < i - 1) & (more_ref[0] > 0), body, jnp.int32(0))

    @pl.when(more_ref[0] > 0)
    def _():
        block_step(lambda p: (kmp_ref[:, lanes(p)], vmp_ref[:, lanes(p)]), LANES,
                   _iota((rows, LANES), 1) < N_META, False)

    for p in range(hp):
        o_ref[:, lanes(p)] = _merge_halves(acc_ref[p, 0:tq], acc_ref[p, tq:rows])


def _sb(proj, proj_meta, batch, tq, hp):
    m = proj.shape[0]
    seq = m // batch
    nq = seq // tq
    w = hp * LANES
    qcol = (SWA_HEADS + 2 * SWA_KV_HEADS) * HEAD_DIM // w
    steps = SB_HEADS * HEAD_DIM // w
    kcol, vcol = qcol + steps, qcol + 2 * steps
    return pl.pallas_call(
        functools.partial(_sb_kernel, tq=tq, hp=hp),
        grid=(batch, steps, nq),
        in_specs=[
            pl.BlockSpec((tq, w), lambda b, p, i: (b * nq + i, qcol + p)),
            pl.BlockSpec((seq, w), lambda b, p, i: (b, kcol + p)),
            pl.BlockSpec((seq, w), lambda b, p, i: (b, vcol + p)),
            pl.BlockSpec((N_META, w), lambda b, p, i: (0, kcol + p)),
            pl.BlockSpec((N_META, w), lambda b, p, i: (0, vcol + p)),
        ],
        out_specs=pl.BlockSpec((tq, w), lambda b, p, i: (b * nq + i, p)),
        out_shape=jax.ShapeDtypeStruct((m, SB_HEADS * HEAD_DIM), F32),
        scratch_shapes=[pltpu.VMEM((seq, w), BF16), pltpu.VMEM((seq, w), BF16),
                        pltpu.VMEM((LANES, w), BF16), pltpu.VMEM((LANES, w), BF16),
                        pltpu.VMEM((hp, 2 * tq, LANES), F32), pltpu.VMEM((hp, 2 * tq, 1), F32),
                        pltpu.SMEM((1,), jnp.int32)],
        compiler_params=_cparams(("arbitrary", "arbitrary", "arbitrary")),
        name="stickbreak",
    )(proj, proj, proj, proj_meta, proj_meta)


def _meta_attn_kernel(sink_ref, p_ref, qg_ref, kg_ref, oa_ref, ob_ref):
    qw = SWA_HEADS * HEAD_DIM
    kvw = SWA_KV_HEADS * HEAD_DIM
    group = SWA_HEADS // SWA_KV_HEADS
    x = p_ref[...]
    xk = _pad_rows(x, LANES)
    qi = _iota((N_META, LANES), 0)
    kj = _iota((N_META, LANES), 1)

    qn = [(t * SCALE).astype(BF16) for t in _head_rmsnorm(x[:, 0:qw], qg_ref[...])]
    kn = jnp.concatenate(_head_rmsnorm(xk[:, qw:qw + kvw], kg_ref[...]), axis=1).astype(BF16)
    vb = xk[:, qw + kvw:qw + 2 * kvw].astype(BF16)
    delta_f = (qi - kj).astype(F32)
    causal = kj <= qi
    for pair in range(SWA_HEADS // 2):
        kv = (2 * pair) // group
        kdup, vdup = _dup_kv(kn, kv), _dup_kv(vb, kv)
        outs = []
        for e, qe in enumerate(_lane_halves(qn[pair])):
            h = 2 * pair + e
            outs.append(_swa_head(qe, kdup, vdup, _alibi_slope(h) * delta_f, causal, sink_ref[h]))
        oa_ref[:, pair * LANES:(pair + 1) * LANES] = _merge_halves(outs[0], outs[1])

    base = qw + 2 * kvw
    sbw = SB_HEADS * HEAD_DIM
    strict = kj < qi
    suffix = _suffix_matrix(LANES)
    zero_c = jnp.zeros((N_META, 1), F32)
    for pair in range(SB_HEADS // 2):
        sl = lambda src, part: src[:, base + part * sbw + pair * LANES: base + part * sbw + (pair + 1) * LANES]
        kb, vb2 = sl(xk, 1).astype(BF16), sl(xk, 2).astype(BF16)
        outs = [_sb_block(qe, kb, vb2, suffix, zero_c, strict)[0]
                for qe in _lane_halves((sl(x, 0) * SB_QSCALE).astype(BF16))]
        ob_ref[:, pair * LANES:(pair + 1) * LANES] = _merge_halves(outs[0], outs[1])


def _meta_attn(proj_meta, sinks, qg, kg, batch):
    pw = proj_meta.shape[1]
    qw = SWA_HEADS * HEAD_DIM
    kvw = SWA_KV_HEADS * HEAD_DIM
    sbw = SB_HEADS * HEAD_DIM
    return pl.pallas_call(
        _meta_attn_kernel,
        grid=(batch,),
        in_specs=[
            pl.BlockSpec(memory_space=pltpu.SMEM),
            pl.BlockSpec((N_META, pw), lambda b: (b, 0)),
            pl.BlockSpec((1, qw), lambda b: (0, 0)),
            pl.BlockSpec((1, kvw), lambda b: (0, 0)),
        ],
        out_specs=[pl.BlockSpec((N_META, qw), lambda b: (b, 0)),
                   pl.BlockSpec((N_META, sbw), lambda b: (b, 0))],
        out_shape=[jax.ShapeDtypeStruct((batch * N_META, qw), F32),
                   jax.ShapeDtypeStruct((batch * N_META, sbw), F32)],
        compiler_params=_cparams(("arbitrary",)),
        name="meta_attn",
    )(sinks, proj_meta, qg, kg)


def kernel(x, meta_tokens, attn_norm_g, w_in, q_norm_g, k_norm_g, attn_sinks,
           swa_out_g, sb_out_g, w_o, ffn_norm_g, w_gate, w_up, w_down):
    batch, seq, d = x.shape
    depth = w_in.shape[0]
    qw = SWA_HEADS * HEAD_DIM
    assert seq % 512 == 0 and meta_tokens.shape[0] == N_META

    tm = 512
    tmeta = N_META
    tm_in = 1024
    tn_in = 1536
    tn_out = 2048
    tm_ffn = 1024
    tf = 512
    tq = 256
    hp = 4

    h = x.reshape(batch * seq, d)
    hm = meta_tokens.astype(x.dtype)

    w_in_b, w_o_b = w_in.astype(BF16), w_o.astype(BF16)
    w_gate_b, w_up_b, w_down_b = w_gate.astype(BF16), w_up.astype(BF16), w_down.astype(BF16)

    swa_tables = _swa_tables()
    for l in range(depth):
        g_attn =attn_norm_g[l].reshape(1, d)
        qg = jnp.tile(q_norm_g[l], SWA_HEADS).reshape(1, -1)
        kg = jnp.tile(k_norm_g[l], SWA_KV_HEADS).reshape(1, -1)
        sinks = attn_sinks[l].astype(F32)
        ga, gb = swa_out_g[l].reshape(1, -1), sb_out_g[l].reshape(1, -1)
        g_ffn = ffn_norm_g[l].reshape(1, d)

        proj = _norm_matmul(h, g_attn, w_in_b, l, tm_in, tn_in)
        proj_m = _norm_matmul(hm, g_attn, w_in_b, l, tmeta, tn_in)

        out_a = _swa(proj, proj_m, sinks, qg, kg, swa_tables, batch)
        out_b = _sb(proj, proj_m, batch, tq, hp)
        out_am, out_bm = _meta_attn(proj_m, sinks, qg, kg, 1)

        h = _outproj(out_a, out_b, ga, gb, w_o_b, l, h, tm, tn_out)
        hm = _outproj(out_am, out_bm, ga, gb, w_o_b, l, hm, tmeta, tn_out)

        h = _ffn(h, g_ffn, w_gate_b, w_up_b, w_down_b, l, tm_ffn, tf)
        hm = _ffn(hm, g_ffn, w_gate_b, w_up_b, w_down_b, l, tmeta, tf)

    return h.reshape(batch, seq, d)
```

```python
import functools

import jax
import jax.numpy as jnp
from jax import lax
from jax.experimental import pallas as pl
from jax.experimental.pallas import tpu as pltpu

F32 = jnp.float32
BF16 = jnp.bfloat16

N_META = 16
BLOCK = 128
HEAD_DIM = 64
LANES = 128
SWA_HEADS = 16
SWA_KV_HEADS = 4
SB_HEADS = 16
EPS = 1e-6
NEG = -1e30
SCALE = HEAD_DIM ** -0.5
LOG2E = 1.4426950408889634
SB_QSCALE = SCALE * LOG2E
F32_UNDERFLOW_LOG2 = 160.0
VMEM_LIMIT = 56 * 1024 * 1024


def _cparams(sem):
    return pltpu.CompilerParams(dimension_semantics=sem, vmem_limit_bytes=VMEM_LIMIT)


def _dot(a, b):
    return jnp.dot(a, b, preferred_element_type=F32)


def _dot_nt(a, b):
    return lax.dot_general(a, b, (((1,), (1,)), ((), ())), preferred_element_type=F32)


def _iota(shape, dim):
    return lax.broadcasted_iota(jnp.int32, shape, dim)


def _norm_mm_kernel(x_ref, g_ref, w_ref, o_ref, xn_ref):
    @pl.when(pl.program_id(1) == 0)
    def _():
        x = x_ref[...]
        ms = jnp.mean(x * x, axis=-1, keepdims=True)
        xn_ref[...] = ((x * lax.rsqrt(ms + EPS)) * g_ref[...]).astype(BF16)

    o_ref[...] = _dot(xn_ref[...], w_ref[...])


def _norm_matmul(x, g, w, layer, tm, tn):
    m, d = x.shape
    n = w.shape[2]
    return pl.pallas_call(
        _norm_mm_kernel,
        grid=(m // tm, n // tn),
        in_specs=[
            pl.BlockSpec((tm, d), lambda i, j: (i, 0)),
            pl.BlockSpec((1, d), lambda i, j: (0, 0)),
            pl.BlockSpec((None, d, tn), lambda i, j: (layer, 0, j)),
        ],
        out_specs=pl.BlockSpec((tm, tn), lambda i, j: (i, j)),
        out_shape=jax.ShapeDtypeStruct((m, n), F32),
        scratch_shapes=[pltpu.VMEM((tm, d), BF16)],
        compiler_params=_cparams(("arbitrary", "arbitrary")),
        name="norm_inproj",
    )(x, g, w)


def _outproj_kernel(a_ref, b_ref, ga_ref, gb_ref, wa_ref, wb_ref, h_ref, o_ref, an_ref, bn_ref):
    @pl.when(pl.program_id(1) == 0)
    def _():
        for src, g, dst in ((a_ref, ga_ref, an_ref), (b_ref, gb_ref, bn_ref)):
            x = src[...]
            ms = jnp.mean(x * x, axis=-1, keepdims=True)
            dst[...] = ((x * lax.rsqrt(ms + EPS)) * g[...]).astype(BF16)

    o_ref[...] = h_ref[...] + (_dot(an_ref[...], wa_ref[...]) + _dot(bn_ref[...], wb_ref[...]))


def _outproj(a, b, ga, gb, w, layer, h, tm, tn):
    m, wdt = a.shape
    d = h.shape[1]
    return pl.pallas_call(
        _outproj_kernel,
        grid=(m // tm, d // tn),
        in_specs=[
            pl.BlockSpec((tm, wdt), lambda i, j: (i, 0)),
            pl.BlockSpec((tm, wdt), lambda i, j: (i, 0)),
            pl.BlockSpec((1, wdt), lambda i, j: (0, 0)),
            pl.BlockSpec((1, wdt), lambda i, j: (0, 0)),
            pl.BlockSpec((None, wdt, tn), lambda i, j: (layer, 0, j)),
            pl.BlockSpec((None, wdt, tn), lambda i, j: (layer, 1, j)),
            pl.BlockSpec((tm, tn), lambda i, j: (i, j)),
        ],
        out_specs=pl.BlockSpec((tm, tn), lambda i, j: (i, j)),
        out_shape=jax.ShapeDtypeStruct((m, d), F32),
        scratch_shapes=[pltpu.VMEM((tm, wdt), BF16), pltpu.VMEM((tm, wdt), BF16)],
        compiler_params=_cparams(("arbitrary", "arbitrary")),
        name="outproj",
    )(a, b, ga, gb, w, w, h)


def _ffn_kernel(x_ref, g_ref, wg_ref, wu_ref, wd_ref, o_ref, xn_ref):
    @pl.when(pl.program_id(1) == 0)
    def _():
        x = x_ref[...]
        ms = jnp.mean(x * x, axis=-1, keepdims=True)
        xn_ref[...] = ((x * lax.rsqrt(ms + EPS)) * g_ref[...]).astype(BF16)
        o_ref[...] = x

    xn = xn_ref[...]
    gate = _dot(xn, wg_ref[...])
    up = _dot(xn, wu_ref[...])
    act = (gate / (1.0 + jnp.exp(-gate))) * up
    o_ref[...] += _dot(act.astype(BF16), wd_ref[...])


def _ffn(x, g, wg, wu, wd, layer, tm, tf):
    m, d = x.shape
    f = wg.shape[2]
    return pl.pallas_call(
        _ffn_kernel,
        grid=(m // tm, f // tf),
        in_specs=[
            pl.BlockSpec((tm, d), lambda i, j: (i, 0)),
            pl.BlockSpec((1, d), lambda i, j: (0, 0)),
            pl.BlockSpec((None, d, tf), lambda i, j: (layer, 0, j)),
            pl.BlockSpec((None, d, tf), lambda i, j: (layer, 0, j)),
            pl.BlockSpec((None, tf, d), lambda i, j: (layer, j, 0)),
        ],
        out_specs=pl.BlockSpec((tm, d), lambda i, j: (i, 0)),
        out_shape=jax.ShapeDtypeStruct((m, d), F32),
        scratch_shapes=[pltpu.VMEM((tm, d), BF16)],
        compiler_params=_cparams(("arbitrary", "arbitrary")),
        name="ffn",
    )(x, g, wg, wu, wd)


def _head_rmsnorm(x, g):
    r, c = x.shape
    same_head = (_iota((LANES, LANES), 0) // HEAD_DIM) == (_iota((LANES, LANES), 1) // HEAD_DIM)
    bd = jnp.where(same_head, 1.0, 0.0).astype(BF16)
    outs = []
    for t in range(c // LANES):
        xb = x[:, t * LANES:(t + 1) * LANES]
        ss = _dot((xb * xb).astype(BF16), bd)
        outs.append((xb * lax.rsqrt(ss * (1.0 / HEAD_DIM) + EPS)) * g[:, t * LANES:(t + 1) * LANES])
    return outs


def _half_select(half):
    src = _iota((LANES, LANES), 0)
    dst = _iota((LANES, LANES), 1)
    return jnp.where(src == (dst % HEAD_DIM) + HEAD_DIM * half, 1.0, 0.0).astype(BF16)


def _lane_halves(x):
    lo_half = _iota(x.shape, 1) < HEAD_DIM
    zero = jnp.zeros_like(x)
    return jnp.where(lo_half, x, zero), jnp.where(lo_half, zero, x)


def _merge_halves(lo, hi):
    return jnp.where(_iota(lo.shape, 1) < HEAD_DIM, lo, hi)


def _alibi_slope(h):
    return 2.0 ** (-8.0 * (h + 1) / SWA_HEADS)


def _softplus2(z2):
    neg_abs = lax.bitcast_convert_type(
        lax.bitcast_convert_type(z2, jnp.uint32) | jnp.uint32(0x80000000), F32)
    return jnp.maximum(z2, 0.0) + jnp.log2(1.0 + jnp.exp2(neg_abs))


def _suffix_matrix(n):
    return jnp.where(_iota((n, n), 0) > _iota((n, n), 1), 1.0, 0.0).astype(BF16)


def _sb_block(qe, kblk, vblk, suffix, carry, mask):
    z2 = _dot_nt(qe, kblk)
    sp = _softplus2(z2)
    if mask is not None:
        sp = jnp.where(mask, sp, 0.0)
    later = _dot(sp.astype(BF16), suffix)
    a = jnp.exp2((z2 - sp) - (later + carry))
    if mask is not None:
        a = jnp.where(mask, a, 0.0)
    contrib = _dot(a.astype(BF16), vblk)
    return contrib, carry + jnp.sum(sp, axis=-1, keepdims=True)


def _swa_head(qe, kdup, vdup, bias, mask, sink):
    s = jnp.where(mask, _dot_nt(qe, kdup) - bias, NEG)
    mx = jnp.maximum(jnp.max(s, axis=-1, keepdims=True), sink)
    e = jnp.exp(s - mx)
    den = jnp.sum(e, axis=-1, keepdims=True) + jnp.exp(sink - mx)
    return _dot(e.astype(BF16), vdup) / den


def _pad_rows(x, rows):
    return jnp.concatenate([x, jnp.zeros((rows - x.shape[0], x.shape[1]), x.dtype)], axis=0)


def _dup_kv(x_bf, kv_head):
    t = kv_head // 2
    return _dot(x_bf[:, t * LANES:(t + 1) * LANES], _half_select(kv_head % 2)).astype(BF16)


def _swa_tables():
    group = SWA_HEADS // SWA_KV_HEADS
    slope = jnp.asarray([_alibi_slope(h) * LOG2E for h in range(SWA_HEADS)], F32)
    slope = jnp.repeat(slope.reshape(SWA_KV_HEADS, group, 1, 1), BLOCK, axis=2)
    qi = jnp.arange(BLOCK, dtype=F32)[:, None]
    j = jnp.arange(BLOCK, dtype=F32)[None, :]
    dist_band = jnp.where(j <= qi, qi - j, qi - j + BLOCK)
    dist_meta = jnp.where(j < N_META, N_META + qi - j, 0.0)
    shape = (SWA_KV_HEADS, group * BLOCK, BLOCK)
    return ((slope * dist_band).reshape(shape), (slope * dist_meta).reshape(shape),
            jnp.broadcast_to(slope * BLOCK, slope.shape[:3] + (BLOCK,)).reshape(shape))


def _swa_kernel(q_ref, kc_ref, vc_ref, km_ref, vm_ref, qg_ref, kg_ref,
                band_ref, meta_ref, step_ref, sink_ref, o_ref, kd_ref, vd_ref):
    n = pl.program_id(1)
    group = SWA_HEADS // SWA_KV_HEADS
    rows = group * BLOCK

    def store_rows(k_rows, v_rows, start):
        kn = jnp.concatenate(_head_rmsnorm(k_rows, kg_ref[...]), axis=1).astype(BF16)
        vb = v_rows.astype(BF16)
        for g in range(SWA_KV_HEADS):
            kd_ref[g, start:start + BLOCK] = _dup_kv(kn, g)
            vd_ref[g, start:start + BLOCK] = _dup_kv(vb, g)

    @pl.when(n == 0)
    def _():
        store_rows(_pad_rows(km_ref[...], BLOCK), _pad_rows(vm_ref[...], BLOCK), 0)
        kd_ref[:, 2 * BLOCK:3 * BLOCK] = jnp.zeros((SWA_KV_HEADS, BLOCK, LANES), BF16)
        vd_ref[:, 2 * BLOCK:3 * BLOCK] = jnp.zeros((SWA_KV_HEADS, BLOCK, LANES), BF16)

    kd_ref[:, BLOCK:2 * BLOCK] = kd_ref[:, 2 * BLOCK:3 * BLOCK]
    vd_ref[:, BLOCK:2 * BLOCK] = vd_ref[:, 2 * BLOCK:3 * BLOCK]
    store_rows(kc_ref[...], vc_ref[...], 2 * BLOCK)
    qn = [(x * (SCALE * LOG2E)).astype(BF16) for x in _head_rmsnorm(q_ref[...], qg_ref[...])]

    qi = _iota((rows, BLOCK), 0) % BLOCK
    j = _iota((rows, BLOCK), 1)
    in_cur = j <= qi
    band_ok = in_cur | (n > 0)
    meta_ok = j < N_META
    blocks_before = n.astype(F32)

    logits, weights, dens = {}, {}, {}

    def score(g):
        logits[g] = _dot_nt(jnp.concatenate(_lane_halves(qn[2 * g]) + _lane_halves(qn[2 * g + 1]), axis=0),
                            kd_ref[g])

    def softmax(g):
        sink = sink_ref[g]
        s = logits.pop(g)
        s_band = jnp.where(in_cur, s[:, 2 * BLOCK:], s[:, BLOCK:2 * BLOCK]) - band_ref[g]
        s_band = jnp.where(band_ok, s_band, NEG)
        s_meta = jnp.where(meta_ok, (s[:, :BLOCK] - meta_ref[g]) - step_ref[g] * blocks_before, NEG)
        mx = jnp.maximum(jnp.max(jnp.maximum(s_band, s_meta), axis=-1, keepdims=True), sink)
        e_band = jnp.exp2(s_band - mx)
        e_meta = jnp.exp2(s_meta - mx)
        dens[g] = jnp.sum(e_band + e_meta, axis=-1, keepdims=True) + jnp.exp2(sink - mx)
        weights[g] = jnp.concatenate(
            [e_meta, jnp.where(in_cur, 0.0, e_band), jnp.where(in_cur, e_band, 0.0)], axis=1).astype(BF16)

    def attend(g):
        o = _dot(weights.pop(g), vd_ref[g]) / dens.pop(g)
        for t in range(2):
            o_ref[:, (2 * g + t) * LANES:(2 * g + t + 1) * LANES] = _merge_halves(
                o[2 * t * BLOCK:(2 * t + 1) * BLOCK], o[(2 * t + 1) * BLOCK:(2 * t + 2) * BLOCK])

    for stage, g in ((score, 0), (score, 1), (softmax, 0), (score, 2), (softmax, 1), (attend, 0),
                     (score, 3), (softmax, 2), (attend, 1), (softmax, 3), (attend, 2), (attend, 3)):
        stage(g)


def _swa(proj, proj_meta, sinks, qg, kg, tables, batch):
    m = proj.shape[0]
    nb = m // batch // BLOCK
    qw = SWA_HEADS * HEAD_DIM
    kvw = SWA_KV_HEADS * HEAD_DIM
    group = SWA_HEADS // SWA_KV_HEADS
    kcol, vcol = qw // kvw, qw // kvw + 1
    row = lambda b, n: b * nb + n
    whole = lambda shape: pl.BlockSpec(shape, lambda b, n: (0,) * len(shape))
    sink_col = jnp.broadcast_to((sinks * LOG2E).reshape(SWA_KV_HEADS, group, 1, 1),
                                (SWA_KV_HEADS, group, BLOCK, LANES))
    sink_col = sink_col.reshape(SWA_KV_HEADS, group * BLOCK, LANES)
    band, meta, step = tables
    return pl.pallas_call(
        _swa_kernel,
        grid=(batch, nb),
        in_specs=[
            pl.BlockSpec((BLOCK, qw), lambda b, n: (row(b, n), 0)),
            pl.BlockSpec((BLOCK, kvw), lambda b, n: (row(b, n), kcol)),
            pl.BlockSpec((BLOCK, kvw), lambda b, n: (row(b, n), vcol)),
            pl.BlockSpec((N_META, kvw), lambda b, n: (0, kcol)),
            pl.BlockSpec((N_META, kvw), lambda b, n: (0, vcol)),
            pl.BlockSpec((1, qw), lambda b, n: (0, 0)),
            pl.BlockSpec((1, kvw), lambda b, n: (0, 0)),
            whole(band.shape), whole(meta.shape), whole(step.shape), whole(sink_col.shape),
        ],
        out_specs=pl.BlockSpec((BLOCK, qw), lambda b, n: (row(b, n), 0)),
        out_shape=jax.ShapeDtypeStruct((m, qw), F32),
        scratch_shapes=[pltpu.VMEM((SWA_KV_HEADS, 3 * BLOCK, LANES), BF16),
                        pltpu.VMEM((SWA_KV_HEADS, 3 * BLOCK, LANES), BF16)],
        compiler_params=_cparams(("arbitrary", "arbitrary")),
        name="swa",
    )(proj, proj, proj, proj_meta, proj_meta, qg, kg, band, meta, step, sink_col)


def _sb_kernel(q_ref, k_ref, v_ref, km_ref, vm_ref, o_ref,
               kbf_ref, vbf_ref, kmp_ref, vmp_ref, acc_ref, car_ref, more_ref, *, tq, hp):
    i = pl.program_id(2)

    @pl.when(i == 0)
    def _():
        kbf_ref[...] = k_ref[...].astype(BF16)
        vbf_ref[...] = v_ref[...].astype(BF16)
        kmp_ref[...] = jnp.zeros_like(kmp_ref)
        vmp_ref[...] = jnp.zeros_like(vmp_ref)
        kmp_ref[0:N_META, :] = km_ref[...].astype(BF16)
        vmp_ref[0:N_META, :] = vm_ref[...].astype(BF16)

    lanes = lambda p: slice(p * LANES, (p + 1) * LANES)
    rows = 2 * tq

    def queries():
        return [jnp.concatenate(_lane_halves((q_ref[:, lanes(p)] * SB_QSCALE).astype(BF16)), axis=0)
                for p in range(hp)]

    def note_carries(carries):
        m = functools.reduce(jnp.minimum, carries)
        while m.shape[0] > 8:
            half = m.shape[0] // 2
            m = jnp.minimum(m[:half], m[half:])
        more_ref[0] = (jnp.min(m) <= F32_UNDERFLOW_LOG2).astype(jnp.int32)

    def block_step(kv_of_pair, width, mask, first):
        qs = queries()
        kv = [kv_of_pair(p) for p in range(hp)]
        z2 = [_dot_nt(qs[p], kv[p][0]) for p in range(hp)]
        sp = [_softplus2(z) for z in z2]
        if mask is not None:
            sp = [jnp.where(mask, s, 0.0) for s in sp]
        zs = [z2[p] - sp[p] for p in range(hp)]
        later = _dot(jnp.concatenate([s.astype(BF16) for s in sp], axis=0), _suffix_matrix(width))
        carries = []
        for p in range(hp):
            carry = 0.0 if first else car_ref[p]
            a = jnp.exp2(zs[p] - (later[p * rows:(p + 1) * rows] + carry))
            if mask is not None:
                a = jnp.where(mask, a, 0.0)
            d = _dot(a.astype(BF16), kv[p][1])
            acc_ref[p] = d if first else acc_ref[p] + d
            carries.append(carry + jnp.sum(sp[p], axis=-1, keepdims=True))
            car_ref[p] = carries[-1]
        note_carries(carries)

    def kv_block(start):
        return lambda p: (kbf_ref[pl.ds(start, tq), lanes(p)], vbf_ref[pl.ds(start, tq), lanes(p)])

    below_diag = lambda: _iota((rows, tq), 1) < (_iota((rows, tq), 0) % tq)
    prev_start = pl.multiple_of(jnp.maximum(i - 1, 0) * tq, tq)

    @pl.when(i == 0)
    def _():
        block_step(kv_block(0), tq, below_diag(), True)

    @pl.when(i > 0)
    def _():
        qs = queries()
        in_cur = below_diag()
        cur_start = pl.multiple_of(i * tq, tq)
        zs, spb = [], []
        for p in range(hp):
            z_cur = _dot_nt(qs[p], kbf_ref[pl.ds(cur_start, tq), lanes(p)])
            z_prev = _dot_nt(qs[p], kbf_ref[pl.ds(prev_start, tq), lanes(p)])
            z2 = jnp.where(in_cur, z_cur, z_prev)
            sp = _softplus2(z2)
            zs.append(z2 - sp)
            spb.append(sp.astype(BF16))
        zero = jnp.zeros((rows, tq), BF16)
        prev_part = lambda x: jnp.where(in_cur, zero, x)
        cur_part = lambda x: jnp.where(in_cur, x, zero)
        suf = _dot(jnp.concatenate(spb, axis=0), _suffix_matrix(tq))
        prev_tot = _dot(jnp.concatenate([prev_part(s) for s in spb], axis=0), jnp.ones((tq, LANES), BF16))
        carries = []
        for p in range(hp):
            sf = suf[p * rows:(p + 1) * rows]
            pt = prev_tot[p * rows:(p + 1) * rows]
            total = sf[:, 0:1] + spb[p][:, 0:1].astype(F32)
            widen = lambda x: jnp.concatenate([x] * (tq // LANES), axis=1)
            later = sf + jnp.where(in_cur, widen(-pt), widen(total - pt))
            a = jnp.exp2(zs[p] - later).astype(BF16)
            v2 = jnp.concatenate([vbf_ref[pl.ds(prev_start, tq), lanes(p)],
                                  vbf_ref[pl.ds(cur_start, tq), lanes(p)]], axis=0)
            acc_ref[p] = _dot(jnp.concatenate([prev_part(a), cur_part(a)], axis=1), v2)
            car_ref[p] = total
            carries.append(total)
        note_carries(carries)

    @pl.when((i > 0) & (more_ref[0] > 0))
    def _():
        block_step(kv_block(prev_start), tq, below_diag(), False)

    def body(t):
        block_step(kv_block(pl.multiple_of((i - 2 - t) * tq, tq)), tq, None, False)
        return t + 1

    lax.while_loop(lambda t: (t < i - 1) & (more_ref[0] > 0), body, jnp.int32(0))

    @pl.when(more_ref[0] > 0)
    def _():
        block_step(lambda p: (kmp_ref[:, lanes(p)], vmp_ref[:, lanes(p)]), LANES,
                   _iota((rows, LANES), 1) < N_META, False)

    for p in range(hp):
        o_ref[:, lanes(p)] = _merge_halves(acc_ref[p, 0:tq], acc_ref[p, tq:rows])


def _sb(proj, proj_meta, batch, tq, hp):
    m = proj.shape[0]
    seq = m // batch
    nq = seq // tq
    w = hp * LANES
    qcol = (SWA_HEADS + 2 * SWA_KV_HEADS) * HEAD_DIM // w
    steps = SB_HEADS * HEAD_DIM // w
    kcol, vcol = qcol + steps, qcol + 2 * steps
    return pl.pallas_call(
        functools.partial(_sb_kernel, tq=tq, hp=hp),
        grid=(batch, steps, nq),
        in_specs=[
            pl.BlockSpec((tq, w), lambda b, p, i: (b * nq + i, qcol + p)),
            pl.BlockSpec((seq, w), lambda b, p, i: (b, kcol + p)),
            pl.BlockSpec((seq, w), lambda b, p, i: (b, vcol + p)),
            pl.BlockSpec((N_META, w), lambda b, p, i: (0, kcol + p)),
            pl.BlockSpec((N_META, w), lambda b, p, i: (0, vcol + p)),
        ],
        out_specs=pl.BlockSpec((tq, w), lambda b, p, i: (b * nq + i, p)),
        out_shape=jax.ShapeDtypeStruct((m, SB_HEADS * HEAD_DIM), F32),
        scratch_shapes=[pltpu.VMEM((seq, w), BF16), pltpu.VMEM((seq, w), BF16),
                        pltpu.VMEM((LANES, w), BF16), pltpu.VMEM((LANES, w), BF16),
                        pltpu.VMEM((hp, 2 * tq, LANES), F32), pltpu.VMEM((hp, 2 * tq, 1), F32),
                        pltpu.SMEM((1,), jnp.int32)],
        compiler_params=_cparams(("arbitrary", "arbitrary", "arbitrary")),
        name="stickbreak",
    )(proj, proj, proj, proj_meta, proj_meta)


def _meta_attn_kernel(sink_ref, p_ref, qg_ref, kg_ref, oa_ref, ob_ref):
    qw = SWA_HEADS * HEAD_DIM
    kvw = SWA_KV_HEADS * HEAD_DIM
    group = SWA_HEADS // SWA_KV_HEADS
    x = p_ref[...]
    xk = _pad_rows(x, LANES)
    qi = _iota((N_META, LANES), 0)
    kj = _iota((N_META, LANES), 1)

    qn = [(t * SCALE).astype(BF16) for t in _head_rmsnorm(x[:, 0:qw], qg_ref[...])]
    kn = jnp.concatenate(_head_rmsnorm(xk[:, qw:qw + kvw], kg_ref[...]), axis=1).astype(BF16)
    vb = xk[:, qw + kvw:qw + 2 * kvw].astype(BF16)
    delta_f = (qi - kj).astype(F32)
    causal = kj <= qi
    for pair in range(SWA_HEADS // 2):
        kv = (2 * pair) // group
        kdup, vdup = _dup_kv(kn, kv), _dup_kv(vb, kv)
        outs = []
        for e, qe in enumerate(_lane_halves(qn[pair])):
            h = 2 * pair + e
            outs.append(_swa_head(qe, kdup, vdup, _alibi_slope(h) * delta_f, causal, sink_ref[h]))
        oa_ref[:, pair * LANES:(pair + 1) * LANES] = _merge_halves(outs[0], outs[1])

    base = qw + 2 * kvw
    sbw = SB_HEADS * HEAD_DIM
    strict = kj < qi
    suffix = _suffix_matrix(LANES)
    zero_c = jnp.zeros((N_META, 1), F32)
    for pair in range(SB_HEADS // 2):
        sl = lambda src, part: src[:, base + part * sbw + pair * LANES: base + part * sbw + (pair + 1) * LANES]
        kb, vb2 = sl(xk, 1).astype(BF16), sl(xk, 2).astype(BF16)
        outs = [_sb_block(qe, kb, vb2, suffix, zero_c, strict)[0]
                for qe in _lane_halves((sl(x, 0) * SB_QSCALE).astype(BF16))]
        ob_ref[:, pair * LANES:(pair + 1) * LANES] = _merge_halves(outs[0], outs[1])


def _meta_attn(proj_meta, sinks, qg, kg, batch):
    pw = proj_meta.shape[1]
    qw = SWA_HEADS * HEAD_DIM
    kvw = SWA_KV_HEADS * HEAD_DIM
    sbw = SB_HEADS * HEAD_DIM
    return pl.pallas_call(
        _meta_attn_kernel,
        grid=(batch,),
        in_specs=[
            pl.BlockSpec(memory_space=pltpu.SMEM),
            pl.BlockSpec((N_META, pw), lambda b: (b, 0)),
            pl.BlockSpec((1, qw), lambda b: (0, 0)),
            pl.BlockSpec((1, kvw), lambda b: (0, 0)),
        ],
        out_specs=[pl.BlockSpec((N_META, qw), lambda b: (b, 0)),
                   pl.BlockSpec((N_META, sbw), lambda b: (b, 0))],
        out_shape=[jax.ShapeDtypeStruct((batch * N_META, qw), F32),
                   jax.ShapeDtypeStruct((batch * N_META, sbw), F32)],
        compiler_params=_cparams(("arbitrary",)),
        name="meta_attn",
    )(sinks, proj_meta, qg, kg)


def kernel(x, meta_tokens, attn_norm_g, w_in, q_norm_g, k_norm_g, attn_sinks,
           swa_out_g, sb_out_g, w_o, ffn_norm_g, w_gate, w_up, w_down):
    batch, seq, d = x.shape
    depth = w_in.shape[0]
    qw = SWA_HEADS * HEAD_DIM
    assert seq % 512 == 0 and meta_tokens.shape[0] == N_META

    tm = 512
    tmeta = N_META
    tm_in = 1024
    tn_in = 1536
    tn_out = 2048
    tm_ffn = 1024
    tf = 512
    tq = 256
    hp = 4

    h = x.reshape(batch * seq, d)
    hm = meta_tokens.astype(x.dtype)

    w_in_b, w_o_b = w_in.astype(BF16), w_o.astype(BF16)
    w_gate_b, w_up_b, w_down_b = w_gate.astype(BF16), w_up.astype(BF16), w_down.astype(BF16)

    swa_tables = _swa_tables()
    for l in range(depth):
        g_attn =attn_norm_g[l].reshape(1, d)
        qg = jnp.tile(q_norm_g[l], SWA_HEADS).reshape(1, -1)
        kg = jnp.tile(k_norm_g[l], SWA_KV_HEADS).reshape(1, -1)
        sinks = attn_sinks[l].astype(F32)
        ga, gb = swa_out_g[l].reshape(1, -1), sb_out_g[l].reshape(1, -1)
        g_ffn = ffn_norm_g[l].reshape(1, d)

        proj = _norm_matmul(h, g_attn, w_in_b, l, tm_in, tn_in)
        proj_m = _norm_matmul(hm, g_attn, w_in_b, l, tmeta, tn_in)

        out_a = _swa(proj, proj_m, sinks, qg, kg, swa_tables, batch)
        out_b = _sb(proj, proj_m, batch, tq, hp)
        out_am, out_bm = _meta_attn(proj_m, sinks, qg, kg, 1)

        h = _outproj(out_a, out_b, ga, gb, w_o_b, l, h, tm, tn_out)
        hm = _outproj(out_am, out_bm, ga, gb, w_o_b, l, hm, tmeta, tn_out)

        h = _ffn(h, g_ffn, w_gate_b, w_up_b, w_down_b, l, tm_ffn, tf)
        hm = _ffn(hm, g_ffn, w_gate_b, w_up_b, w_down_b, l, tmeta, tf)

    return h.reshape(batch, seq, d)
```

```python
import functools

import jax
import jax.numpy as jnp
from jax import lax
from jax.experimental import pallas as pl
from jax.experimental.pallas import tpu as pltpu

F32 = jnp.float32
BF16 = jnp.bfloat16

N_META = 16
BLOCK = 128
HEAD_DIM = 64
LANES = 128
SWA_HEADS = 16
SWA_KV_HEADS = 4
SB_HEADS = 16
EPS = 1e-6
NEG = -1e30
SCALE = HEAD_DIM ** -0.5
LOG2E = 1.4426950408889634
SB_QSCALE = SCALE * LOG2E
F32_UNDERFLOW_LOG2 = 160.0
VMEM_LIMIT = 56 * 1024 * 1024


def _cparams(sem):
    return pltpu.CompilerParams(dimension_semantics=sem, vmem_limit_bytes=VMEM_LIMIT)


def _dot(a, b):
    return jnp.dot(a, b, preferred_element_type=F32)


def _dot_nt(a, b):
    return lax.dot_general(a, b, (((1,), (1,)), ((), ())), preferred_element_type=F32)


def _iota(shape, dim):
    return lax.broadcasted_iota(jnp.int32, shape, dim)


def _norm_mm_kernel(x_ref, g_ref, w_ref, o_ref, ob_ref, xn_ref):
    @pl.when(pl.program_id(1) == 0)
    def _():
        x = x_ref[...]
        ms = jnp.mean(x * x, axis=-1, keepdims=True)
        xn_ref[...] = ((x * lax.rsqrt(ms + EPS)) * g_ref[...]).astype(BF16)

    y = _dot(xn_ref[...], w_ref[...])
    o_ref[...] = y
    ob_ref[...] = y.astype(BF16)


def _norm_matmul(x, g, w, layer, tm, tn):
    m, d = x.shape
    n = w.shape[2]
    return pl.pallas_call(
        _norm_mm_kernel,
        grid=(m // tm, n // tn),
        in_specs=[
            pl.BlockSpec((tm, d), lambda i, j: (i, 0)),
            pl.BlockSpec((1, d), lambda i, j: (0, 0)),
            pl.BlockSpec((None, d, tn), lambda i, j: (layer, 0, j)),
        ],
        out_specs=[pl.BlockSpec((tm, tn), lambda i, j: (i, j)),
                   pl.BlockSpec((tm, tn), lambda i, j: (i, jnp.maximum(j - 1, 0)))],
        out_shape=[jax.ShapeDtypeStruct((m, n), F32), jax.ShapeDtypeStruct((m, n - tn), BF16)],
        scratch_shapes=[pltpu.VMEM((tm, d), BF16)],
        compiler_params=_cparams(("arbitrary", "arbitrary")),
        name="norm_inproj",
    )(x, g, w)


def _outproj_kernel(a_ref, b_ref, ga_ref, gb_ref, wa_ref, wb_ref, h_ref, o_ref, an_ref, bn_ref):
    @pl.when(pl.program_id(1) == 0)
    def _():
        for src, g, dst in ((a_ref, ga_ref, an_ref), (b_ref, gb_ref, bn_ref)):
            x = src[...]
            ms = jnp.mean(x * x, axis=-1, keepdims=True)
            dst[...] = ((x * lax.rsqrt(ms + EPS)) * g[...]).astype(BF16)

    o_ref[...] = h_ref[...] + (_dot(an_ref[...], wa_ref[...]) + _dot(bn_ref[...], wb_ref[...]))


def _outproj(a, b, ga, gb, w, layer, h, tm, tn):
    m, wdt = a.shape
    d = h.shape[1]
    return pl.pallas_call(
        _outproj_kernel,
        grid=(m // tm, d // tn),
        in_specs=[
            pl.BlockSpec((tm, wdt), lambda i, j: (i, 0)),
            pl.BlockSpec((tm, wdt), lambda i, j: (i, 0)),
            pl.BlockSpec((1, wdt), lambda i, j: (0, 0)),
            pl.BlockSpec((1, wdt), lambda i, j: (0, 0)),
            pl.BlockSpec((None, wdt, tn), lambda i, j: (layer, 0, j)),
            pl.BlockSpec((None, wdt, tn), lambda i, j: (layer, 1, j)),
            pl.BlockSpec((tm, tn), lambda i, j: (i, j)),
        ],
        out_specs=pl.BlockSpec((tm, tn), lambda i, j: (i, j)),
        out_shape=jax.ShapeDtypeStruct((m, d), F32),
        scratch_shapes=[pltpu.VMEM((tm, wdt), BF16), pltpu.VMEM((tm, wdt), BF16)],
        compiler_params=_cparams(("arbitrary", "arbitrary")),
        name="outproj",
    )(a, b, ga, gb, w, w, h)


def _ffn_kernel(x_ref, g_ref, wg_ref, wu_ref, wd_ref, o_ref, xn_ref):
    @pl.when(pl.program_id(1) == 0)
    def _():
        x = x_ref[...]
        ms = jnp.mean(x * x, axis=-1, keepdims=True)
        xn_ref[...] = ((x * lax.rsqrt(ms + EPS)) * g_ref[...]).astype(BF16)
        o_ref[...] = x

    xn = xn_ref[...]
    gate = _dot(xn, wg_ref[...])
    up = _dot(xn, wu_ref[...])
    act = (gate / (1.0 + jnp.exp(-gate))) * up
    o_ref[...] += _dot(act.astype(BF16), wd_ref[...])


def _ffn(x, g, wg, wu, wd, layer, tm, tf):
    m, d = x.shape
    f = wg.shape[2]
    return pl.pallas_call(
        _ffn_kernel,
        grid=(m // tm, f // tf),
        in_specs=[
            pl.BlockSpec((tm, d), lambda i, j: (i, 0)),
            pl.BlockSpec((1, d), lambda i, j: (0, 0)),
            pl.BlockSpec((None, d, tf), lambda i, j: (layer, 0, j)),
            pl.BlockSpec((None, d, tf), lambda i, j: (layer, 0, j)),
            pl.BlockSpec((None, tf, d), lambda i, j: (layer, j, 0)),
        ],
        out_specs=pl.BlockSpec((tm, d), lambda i, j: (i, 0)),
        out_shape=jax.ShapeDtypeStruct((m, d), F32),
        scratch_shapes=[pltpu.VMEM((tm, d), BF16)],
        compiler_params=_cparams(("arbitrary", "arbitrary")),
        name="ffn",
    )(x, g, wg, wu, wd)


def _head_rmsnorm(x, g):
    r, c = x.shape
    same_head = (_iota((LANES, LANES), 0) // HEAD_DIM) == (_iota((LANES, LANES), 1) // HEAD_DIM)
    bd = jnp.where(same_head, 1.0 / HEAD_DIM, 0.0).astype(BF16)
    outs = []
    for t in range(c // LANES):
        xb = x[:, t * LANES:(t + 1) * LANES]
        ms = _dot((xb * xb).astype(BF16), bd)
        outs.append((xb * lax.rsqrt(ms + EPS)) * g[:, t * LANES:(t + 1) * LANES])
    return outs


def _half_select(half):
    src = _iota((LANES, LANES), 0)
    dst = _iota((LANES, LANES), 1)
    return jnp.where(src == (dst % HEAD_DIM) + HEAD_DIM * half, 1.0, 0.0).astype(BF16)


def _lane_halves(x):
    lo_half = _iota(x.shape, 1) < HEAD_DIM
    zero = jnp.zeros_like(x)
    return jnp.where(lo_half, x, zero), jnp.where(lo_half, zero, x)


def _merge_halves(lo, hi):
    return jnp.where(_iota(lo.shape, 1) < HEAD_DIM, lo, hi)


def _alibi_slope(h):
    return 2.0 ** (-8.0 * (h + 1) / SWA_HEADS)


def _softplus2(z2):
    neg_abs = lax.bitcast_convert_type(
        lax.bitcast_convert_type(z2, jnp.uint32) | jnp.uint32(0x80000000), F32)
    return jnp.maximum(z2, 0.0) + jnp.log2(1.0 + jnp.exp2(neg_abs))


def _suffix_matrix(n):
    return jnp.where(_iota((n, n), 0) > _iota((n, n), 1), 1.0, 0.0).astype(BF16)


def _sb_block(qe, kblk, vblk, suffix, carry, mask):
    z2 = _dot_nt(qe, kblk)
    sp = _softplus2(z2)
    if mask is not None:
        sp = jnp.where(mask, sp, 0.0)
    later = _dot(sp.astype(BF16), suffix)
    a = jnp.exp2((z2 - sp) - (later + carry))
    if mask is not None:
        a = jnp.where(mask, a, 0.0)
    contrib = _dot(a.astype(BF16), vblk)
    return contrib, carry + jnp.sum(sp, axis=-1, keepdims=True)


def _swa_head(qe, kdup, vdup, bias, mask, sink):
    s = jnp.where(mask, _dot_nt(qe, kdup) - bias, NEG)
    mx = jnp.maximum(jnp.max(s, axis=-1, keepdims=True), sink)
    e = jnp.exp(s - mx)
    den = jnp.sum(e, axis=-1, keepdims=True) + jnp.exp(sink - mx)
    return _dot(e.astype(BF16), vdup) / den


def _pad_rows(x, rows):
    return jnp.concatenate([x, jnp.zeros((rows - x.shape[0], x.shape[1]), x.dtype)], axis=0)


def _dup_kv(x_bf, kv_head):
    t = kv_head // 2
    return _dot(x_bf[:, t * LANES:(t + 1) * LANES], _half_select(kv_head % 2)).astype(BF16)


def _swa_tables():
    group = SWA_HEADS // SWA_KV_HEADS
    slope = jnp.asarray([_alibi_slope(h) * LOG2E for h in range(SWA_HEADS)], F32)
    slope = jnp.repeat(slope.reshape(SWA_KV_HEADS, group, 1, 1), BLOCK, axis=2)
    qi = jnp.arange(BLOCK, dtype=F32)[:, None]
    j = jnp.arange(BLOCK, dtype=F32)[None, :]
    dist_band = jnp.where(j <= qi, qi - j, qi - j + BLOCK)
    dist_meta = jnp.where(j < N_META, N_META + qi - j, 0.0)
    shape = (SWA_KV_HEADS, group * BLOCK, BLOCK)
    return ((slope * dist_band).reshape(shape), (slope * dist_meta).reshape(shape),
            jnp.broadcast_to(slope * BLOCK, slope.shape[:3] + (BLOCK,)).reshape(shape))


def _swa_kernel(q_ref, kc_ref, vc_ref, km_ref, vm_ref, qg_ref, kg_ref,
                band_ref, meta_ref, step_ref, sink_ref, o_ref, kd_ref, vd_ref):
    n = pl.program_id(1)
    group = SWA_HEADS // SWA_KV_HEADS
    rows = group * BLOCK

    def store_rows(k_rows, v_rows, start):
        kn = jnp.concatenate(_head_rmsnorm(k_rows, kg_ref[...]), axis=1).astype(BF16)
        vb = v_rows.astype(BF16)
        for g in range(SWA_KV_HEADS):
            kd_ref[g, start:start + BLOCK] = _dup_kv(kn, g)
            vd_ref[g, start:start + BLOCK] = _dup_kv(vb, g)

    @pl.when(n == 0)
    def _():
        store_rows(_pad_rows(km_ref[...], BLOCK), _pad_rows(vm_ref[...], BLOCK), 0)
        kd_ref[:, 2 * BLOCK:3 * BLOCK] = jnp.zeros((SWA_KV_HEADS, BLOCK, LANES), BF16)
        vd_ref[:, 2 * BLOCK:3 * BLOCK] = jnp.zeros((SWA_KV_HEADS, BLOCK, LANES), BF16)

    kd_ref[:, BLOCK:2 * BLOCK] = kd_ref[:, 2 * BLOCK:3 * BLOCK]
    vd_ref[:, BLOCK:2 * BLOCK] = vd_ref[:, 2 * BLOCK:3 * BLOCK]
    store_rows(kc_ref[...], vc_ref[...], 2 * BLOCK)
    qn = [x.astype(BF16) for x in _head_rmsnorm(q_ref[...], qg_ref[...])]

    qi = _iota((rows, BLOCK), 0) % BLOCK
    j = _iota((rows, BLOCK), 1)
    in_cur = j <= qi
    band_ok = in_cur | (n > 0)
    meta_ok = j < N_META
    blocks_before = n.astype(F32)

    logits, weights, dens = {}, {}, {}

    def score(g):
        logits[g] = _dot_nt(jnp.concatenate(_lane_halves(qn[2 * g]) + _lane_halves(qn[2 * g + 1]), axis=0),
                            kd_ref[g])

    def softmax(g):
        sink = sink_ref[g]
        s = logits.pop(g)
        s_band = jnp.where(in_cur, s[:, 2 * BLOCK:], s[:, BLOCK:2 * BLOCK]) - band_ref[g]
        s_band = jnp.where(band_ok, s_band, NEG)
        s_meta = jnp.where(meta_ok, (s[:, :BLOCK] - meta_ref[g]) - step_ref[g] * blocks_before, NEG)
        mx = jnp.maximum(jnp.max(jnp.maximum(s_band, s_meta), axis=-1, keepdims=True), sink)
        e_band = jnp.exp2(s_band - mx)
        e_meta = jnp.exp2(s_meta - mx)
        dens[g] = jnp.sum(e_band + e_meta, axis=-1, keepdims=True) + jnp.exp2(sink - mx)
        weights[g] = jnp.concatenate(
            [e_meta, jnp.where(in_cur, 0.0, e_band), jnp.where(in_cur, e_band, 0.0)], axis=1).astype(BF16)

    def attend(g):
        o = _dot(weights.pop(g), vd_ref[g]) / dens.pop(g)
        for t in range(2):
            o_ref[:, (2 * g + t) * LANES:(2 * g + t + 1) * LANES] = _merge_halves(
                o[2 * t * BLOCK:(2 * t + 1) * BLOCK], o[(2 * t + 1) * BLOCK:(2 * t + 2) * BLOCK])

    for stage, g in ((score, 0), (score, 1), (softmax, 0), (score, 2), (softmax, 1), (attend, 0),
                     (score, 3), (softmax, 2), (attend, 1), (softmax, 3), (attend, 2), (attend, 3)):
        stage(g)


def _swa(proj, proj_meta, sinks, qg, kg, tables, batch):
    m = proj.shape[0]
    nb = m // batch // BLOCK
    qw = SWA_HEADS * HEAD_DIM
    kvw = SWA_KV_HEADS * HEAD_DIM
    group = SWA_HEADS // SWA_KV_HEADS
    kcol, vcol = qw // kvw, qw // kvw + 1
    row = lambda b, n: b * nb + n
    whole = lambda shape: pl.BlockSpec(shape, lambda b, n: (0,) * len(shape))
    sink_col = jnp.broadcast_to((sinks * LOG2E).reshape(SWA_KV_HEADS, group, 1, 1),
                                (SWA_KV_HEADS, group, BLOCK, LANES))
    sink_col = sink_col.reshape(SWA_KV_HEADS, group * BLOCK, LANES)
    band, meta, step = tables
    return pl.pallas_call(
        _swa_kernel,
        grid=(batch, nb),
        in_specs=[
            pl.BlockSpec((BLOCK, qw), lambda b, n: (row(b, n), 0)),
            pl.BlockSpec((BLOCK, kvw), lambda b, n: (row(b, n), kcol)),
            pl.BlockSpec((BLOCK, kvw), lambda b, n: (row(b, n), vcol)),
            pl.BlockSpec((N_META, kvw), lambda b, n: (0, kcol)),
            pl.BlockSpec((N_META, kvw), lambda b, n: (0, vcol)),
            pl.BlockSpec((1, qw), lambda b, n: (0, 0)),
            pl.BlockSpec((1, kvw), lambda b, n: (0, 0)),
            whole(band.shape), whole(meta.shape), whole(step.shape), whole(sink_col.shape),
        ],
        out_specs=pl.BlockSpec((BLOCK, qw), lambda b, n: (row(b, n), 0)),
        out_shape=jax.ShapeDtypeStruct((m, qw), F32),
        scratch_shapes=[pltpu.VMEM((SWA_KV_HEADS, 3 * BLOCK, LANES), BF16),
                        pltpu.VMEM((SWA_KV_HEADS, 3 * BLOCK, LANES), BF16)],
        compiler_params=_cparams(("arbitrary", "arbitrary")),
        name="swa",
    )(proj, proj, proj, proj_meta, proj_meta, qg, kg, band, meta, step, sink_col)


def _sb_kernel(*refs, tq, hp, n_q):
    q_refs = refs[:n_q]
    kbf_ref, vbf_ref, km_ref, vm_ref, o_ref, kmp_ref, vmp_ref, acc_ref, car_ref, more_ref = refs[n_q:]
    per_q = hp // n_q
    i = pl.program_id(2)

    @pl.when(i == 0)
    def _():
        kmp_ref[...] = jnp.zeros_like(kmp_ref)
        vmp_ref[...] = jnp.zeros_like(vmp_ref)
        kmp_ref[0:N_META, :] = km_ref[...].astype(BF16)
        vmp_ref[0:N_META, :] = vm_ref[...].astype(BF16)

    lanes = lambda p: slice(p * LANES, (p + 1) * LANES)
    rows = 2 * tq

    def queries():
        return [jnp.concatenate(_lane_halves(
                    (q_refs[p // per_q][:, lanes(p % per_q)] * SB_QSCALE).astype(BF16)), axis=0)
                for p in range(hp)]

    def note_carries(carries):
        m = functools.reduce(jnp.minimum, carries)
        while m.shape[0] > 8:
            half = m.shape[0] // 2
            m = jnp.minimum(m[:half], m[half:])
        more_ref[0] = (jnp.min(m) <= F32_UNDERFLOW_LOG2).astype(jnp.int32)

    def block_step(kv_of_pair, width, mask, first):
        qs = queries()
        kv = [kv_of_pair(p) for p in range(hp)]
        z2 = [_dot_nt(qs[p], kv[p][0]) for p in range(hp)]
        sp = [_softplus2(z) for z in z2]
        if mask is not None:
            sp = [jnp.where(mask, s, 0.0) for s in sp]
        zs = [z2[p] - sp[p] for p in range(hp)]
        later = _dot(jnp.concatenate([s.astype(BF16) for s in sp], axis=0), _suffix_matrix(width))
        carries = []
        for p in range(hp):
            carry = 0.0 if first else car_ref[p]
            a = jnp.exp2(zs[p] - (later[p * rows:(p + 1) * rows] + carry))
            if mask is not None:
                a = jnp.where(mask, a, 0.0)
            d = _dot(a.astype(BF16), kv[p][1])
            acc_ref[p] = d if first else acc_ref[p] + d
            carries.append(carry + jnp.sum(sp[p], axis=-1, keepdims=True))
            car_ref[p] = carries[-1]
        note_carries(carries)

    def kv_block(start):
        return lambda p: (kbf_ref[pl.ds(start, tq), lanes(p)], vbf_ref[pl.ds(start, tq), lanes(p)])

    below_diag = lambda: _iota((rows, tq), 1) < (_iota((rows, tq), 0) % tq)
    prev_start = pl.multiple_of(jnp.maximum(i - 1, 0) * tq, tq)

    @pl.when(i == 0)
    def _():
        block_step(kv_block(0), tq, below_diag(), True)

    @pl.when(i > 0)
    def _():
        qs = queries()
        in_cur = below_diag()
        cur_start = pl.multiple_of(i * tq, tq)
        zs, spb = [], []
        for p in range(hp):
            z_cur = _dot_nt(qs[p], kbf_ref[pl.ds(cur_start, tq), lanes(p)])
            z_prev = _dot_nt(qs[p], kbf_ref[pl.ds(prev_start, tq), lanes(p)])
            z2 = jnp.where(in_cur, z_cur, z_prev)
            sp = _softplus2(z2)
            zs.append(z2 - sp)
            spb.append(sp.astype(BF16))
        zero = jnp.zeros((rows, tq), BF16)
        prev_part = lambda x: jnp.where(in_cur, zero, x)
        cur_part = lambda x: jnp.where(in_cur, x, zero)
        suf = _dot(jnp.concatenate(spb, axis=0), _suffix_matrix(tq))
        prev_tot = _dot(jnp.concatenate([prev_part(s) for s in spb], axis=0), jnp.ones((tq, LANES), BF16))
        carries = []
        for p in range(hp):
            sf = suf[p * rows:(p + 1) * rows]
            pt = prev_tot[p * rows:(p + 1) * rows]
            total = sf[:, 0:1] + spb[p][:, 0:1].astype(F32)
            widen = lambda x: jnp.concatenate([x] * (tq // LANES), axis=1)
            later = sf + jnp.where(in_cur, widen(-pt), widen(total - pt))
            a = jnp.exp2(zs[p] - later).astype(BF16)
            v2 = jnp.concatenate([vbf_ref[pl.ds(prev_start, tq), lanes(p)],
                                  vbf_ref[pl.ds(cur_start, tq), lanes(p)]], axis=0)
            acc_ref[p] = _dot(jnp.concatenate([prev_part(a), cur_part(a)], axis=1), v2)
            car_ref[p] = total
            carries.append(total)
        note_carries(carries)

    @pl.when(more_ref[0] > 0)
    def _():
        @pl.when(i > 0)
        def _():
            block_step(kv_block(prev_start), tq, below_diag(), False)

        def body(t):
            block_step(kv_block(pl.multiple_of((i - 2 - t) * tq, tq)), tq, None, False)
            return t + 1

        lax.while_loop(lambda t: (t < i - 1) & (more_ref[0] > 0), body, jnp.int32(0))

        @pl.when(more_ref[0] > 0)
        def _():
            block_step(lambda p: (kmp_ref[:, lanes(p)], vmp_ref[:, lanes(p)]), LANES,
                       _iota((rows, LANES), 1) < N_META, False)

    for p in range(hp):
        o_ref[:, lanes(p)] = _merge_halves(acc_ref[p, 0:tq], acc_ref[p, tq:rows])


def _sb(proj, proj_b, proj_meta_b, batch, tq, hp):
    m = proj.shape[0]
    seq = m // batch
    nq = seq // tq
    w = hp * LANES
    steps = SB_HEADS * HEAD_DIM // w
    q_start = (SWA_HEADS + 2 * SWA_KV_HEADS) * HEAD_DIM
    qw = min(w, 4 * LANES)
    n_q = w // qw
    assert q_start % qw == 0
    q_spec = lambda t: pl.BlockSpec((tq, qw), lambda b, p, i: (b * nq + i, q_start // qw + p * n_q + t))
    return pl.pallas_call(
        functools.partial(_sb_kernel, tq=tq, hp=hp, n_q=n_q),
        grid=(batch, steps, nq),
        in_specs=[q_spec(t) for t in range(n_q)] + [
            pl.BlockSpec((seq, w), lambda b, p, i: (b, steps + p)),
            pl.BlockSpec((seq, w), lambda b, p, i: (b, 2 * steps + p)),
            pl.BlockSpec((N_META, w), lambda b, p, i: (0, steps + p)),
            pl.BlockSpec((N_META, w), lambda b, p, i: (0, 2 * steps + p)),
        ],
        out_specs=pl.BlockSpec((tq, w), lambda b, p, i: (b * nq + i, p)),
        out_shape=jax.ShapeDtypeStruct((m, SB_HEADS * HEAD_DIM), F32),
        scratch_shapes=[pltpu.VMEM((LANES, w), BF16), pltpu.VMEM((LANES, w), BF16),
                        pltpu.VMEM((hp, 2 * tq, LANES), F32), pltpu.VMEM((hp, 2 * tq, 1), F32),
                        pltpu.SMEM((1,), jnp.int32)],
        compiler_params=_cparams(("arbitrary", "arbitrary", "arbitrary")),
        name="stickbreak",
    )(*([proj] * n_q), proj_b, proj_b, proj_meta_b, proj_meta_b)


def _meta_attn_kernel(sink_ref, p_ref, qg_ref, kg_ref, oa_ref, ob_ref):
    qw = SWA_HEADS * HEAD_DIM
    kvw = SWA_KV_HEADS * HEAD_DIM
    group = SWA_HEADS // SWA_KV_HEADS
    x = p_ref[...]
    xk = _pad_rows(x, LANES)
    qi = _iota((N_META, LANES), 0)
    kj = _iota((N_META, LANES), 1)

    qn = [(t * SCALE).astype(BF16) for t in _head_rmsnorm(x[:, 0:qw], qg_ref[...])]
    kn = jnp.concatenate(_head_rmsnorm(xk[:, qw:qw + kvw], kg_ref[...]), axis=1).astype(BF16)
    vb = xk[:, qw + kvw:qw + 2 * kvw].astype(BF16)
    delta_f = (qi - kj).astype(F32)
    causal = kj <= qi
    for pair in range(SWA_HEADS // 2):
        kv = (2 * pair) // group
        kdup, vdup = _dup_kv(kn, kv), _dup_kv(vb, kv)
        outs = []
        for e, qe in enumerate(_lane_halves(qn[pair])):
            h = 2 * pair + e
            outs.append(_swa_head(qe, kdup, vdup, _alibi_slope(h) * delta_f, causal, sink_ref[h]))
        oa_ref[:, pair * LANES:(pair + 1) * LANES] = _merge_halves(outs[0], outs[1])

    base = qw + 2 * kvw
    sbw = SB_HEADS * HEAD_DIM
    strict = kj < qi
    suffix = _suffix_matrix(LANES)
    zero_c = jnp.zeros((N_META, 1), F32)
    for pair in range(SB_HEADS // 2):
        sl = lambda src, part: src[:, base + part * sbw + pair * LANES: base + part * sbw + (pair + 1) * LANES]
        kb, vb2 = sl(xk, 1).astype(BF16), sl(xk, 2).astype(BF16)
        outs = [_sb_block(qe, kb, vb2, suffix, zero_c, strict)[0]
                for qe in _lane_halves((sl(x, 0) * SB_QSCALE).astype(BF16))]
        ob_ref[:, pair * LANES:(pair + 1) * LANES] = _merge_halves(outs[0], outs[1])


def _meta_attn(proj_meta, sinks, qg, kg, batch):
    pw = proj_meta.shape[1]
    qw = SWA_HEADS * HEAD_DIM
    kvw = SWA_KV_HEADS * HEAD_DIM
    sbw = SB_HEADS * HEAD_DIM
    return pl.pallas_call(
        _meta_attn_kernel,
        grid=(batch,),
        in_specs=[
            pl.BlockSpec(memory_space=pltpu.SMEM),
            pl.BlockSpec((N_META, pw), lambda b: (b, 0)),
            pl.BlockSpec((1, qw), lambda b: (0, 0)),
            pl.BlockSpec((1, kvw), lambda b: (0, 0)),
        ],
        out_specs=[pl.BlockSpec((N_META, qw), lambda b: (b, 0)),
                   pl.BlockSpec((N_META, sbw), lambda b: (b, 0))],
        out_shape=[jax.ShapeDtypeStruct((batch * N_META, qw), F32),
                   jax.ShapeDtypeStruct((batch * N_META, sbw), F32)],
        compiler_params=_cparams(("arbitrary",)),
        name="meta_attn",
    )(sinks, proj_meta, qg, kg)


def kernel(x, meta_tokens, attn_norm_g, w_in, q_norm_g, k_norm_g, attn_sinks,
           swa_out_g, sb_out_g, w_o, ffn_norm_g, w_gate, w_up, w_down):
    batch, seq, d = x.shape
    depth = w_in.shape[0]
    assert (batch * seq) % 1024 == 0 and seq % 256 == 0 and meta_tokens.shape[0] == N_META

    tm = 512
    tmeta = N_META
    tm_in = 1024
    tn_in = 1536
    assert tn_in == (SWA_HEADS + 2 * SWA_KV_HEADS) * HEAD_DIM
    tn_out = 2048
    tm_ffn = 1024
    tf = 512
    tq = 256
    hp = 8

    h = x.reshape(batch * seq, d)
    hm = meta_tokens.astype(x.dtype)

    w_in_b, w_o_b = w_in.astype(BF16), w_o.astype(BF16)
    w_gate_b, w_up_b, w_down_b = w_gate.astype(BF16), w_up.astype(BF16), w_down.astype(BF16)

    swa_tables = _swa_tables()
    for l in range(depth):
        g_attn = attn_norm_g[l].reshape(1, d)
        qg = jnp.tile(q_norm_g[l], SWA_HEADS).reshape(1, -1)
        kg = jnp.tile(k_norm_g[l], SWA_KV_HEADS).reshape(1, -1)
        sinks = attn_sinks[l].astype(F32)
        ga, gb = swa_out_g[l].reshape(1, -1), sb_out_g[l].reshape(1, -1)
        g_ffn = ffn_norm_g[l].reshape(1, d)

        proj, proj_b = _norm_matmul(h, g_attn, w_in_b, l, tm_in, tn_in)
        proj_m, proj_mb = _norm_matmul(hm, g_attn, w_in_b, l, tmeta, tn_in)

        out_a = _swa(proj, proj_m, sinks, qg * (SCALE * LOG2E), kg, swa_tables, batch)
        out_b = _sb(proj, proj_b, proj_mb, batch, tq, hp)
        out_am, out_bm = _meta_attn(proj_m, sinks, qg, kg, 1)

        h = _outproj(out_a, out_b, ga, gb, w_o_b, l, h, tm, tn_out)
        hm = _outproj(out_am, out_bm, ga, gb, w_o_b, l, hm, tmeta, tn_out)

        h = _ffn(h, g_ffn, w_gate_b, w_up_b, w_down_b, l, tm_ffn, tf)
        hm = _ffn(hm, g_ffn, w_gate_b, w_up_b, w_down_b, l, tmeta, tf)

    return h.reshape(batch, seq, d)
```

```python
import functools

import jax
import jax.numpy as jnp
from jax import lax
from jax.experimental import pallas as pl
from jax.experimental.pallas import tpu as pltpu

F32 = jnp.float32
BF16 = jnp.bfloat16

N_META = 16
BLOCK = 128
HEAD_DIM = 64
LANES = 128
SWA_HEADS = 16
SWA_KV_HEADS = 4
SB_HEADS = 16
EPS = 1e-6
NEG = -1e30
SCALE = HEAD_DIM ** -0.5
LOG2E = 1.4426950408889634
SB_QSCALE = SCALE * LOG2E
F32_UNDERFLOW_LOG2 = 160.0
VMEM_LIMIT = 56 * 1024 * 1024


def _cparams(sem):
    return pltpu.CompilerParams(dimension_semantics=sem, vmem_limit_bytes=VMEM_LIMIT)


def _dot(a, b):
    return jnp.dot(a, b, preferred_element_type=F32)


def _dot_nt(a, b):
    return lax.dot_general(a, b, (((1,), (1,)), ((), ())), preferred_element_type=F32)


def _iota(shape, dim):
    return lax.broadcasted_iota(jnp.int32, shape, dim)


def _norm_mm_kernel(x_ref, g_ref, w_ref, o_ref, ob_ref, xn_ref):
    @pl.when(pl.program_id(1) == 0)
    def _():
        x = x_ref[...]
        ms = jnp.mean(x * x, axis=-1, keepdims=True)
        xn_ref[...] = ((x * lax.rsqrt(ms + EPS)) * g_ref[...]).astype(BF16)

    y = _dot(xn_ref[...], w_ref[...])
    ob_ref[...] = y.astype(BF16)

    @pl.when(pl.program_id(1) == 0)
    def _():
        o_ref[...] = y


def _norm_matmul(x, g, w, layer, tm, tn):
    m, d = x.shape
    n = w.shape[2]
    return pl.pallas_call(
        _norm_mm_kernel,
        grid=(m // tm, n // tn),
        in_specs=[
            pl.BlockSpec((tm, d), lambda i, j: (i, 0)),
            pl.BlockSpec((1, d), lambda i, j: (0, 0)),
            pl.BlockSpec((None, d, tn), lambda i, j: (layer, 0, j)),
        ],
        out_specs=[pl.BlockSpec((tm, tn), lambda i, j: (i, 0), pipeline_mode=pl.Buffered(1)),
                   pl.BlockSpec((tm, tn), lambda i, j: (i, jnp.maximum(j - 1, 0)))],
        out_shape=[jax.ShapeDtypeStruct((m, tn), F32), jax.ShapeDtypeStruct((m, n - tn), BF16)],
        scratch_shapes=[pltpu.VMEM((tm, d), BF16)],
        compiler_params=_cparams(("arbitrary", "arbitrary")),
        name="norm_inproj",
    )(x, g, w)


def _outproj_kernel(a_ref, b_ref, ga_ref, gb_ref, wa_ref, wb_ref, h_ref, o_ref, an_ref, bn_ref):
    @pl.when(pl.program_id(1) == 0)
    def _():
        for src, g, dst in ((a_ref, ga_ref, an_ref), (b_ref, gb_ref, bn_ref)):
            x = src[...]
            ms = jnp.mean(x * x, axis=-1, keepdims=True)
            dst[...] = ((x * lax.rsqrt(ms + EPS)) * g[...]).astype(BF16)

    o_ref[...] = h_ref[...] + (_dot(an_ref[...], wa_ref[...]) + _dot(bn_ref[...], wb_ref[...]))


def _outproj(a, b, ga, gb, w, layer, h, tm, tn):
    m, wdt = a.shape
    d = h.shape[1]
    return pl.pallas_call(
        _outproj_kernel,
        grid=(m // tm, d // tn),
        in_specs=[
            pl.BlockSpec((tm, wdt), lambda i, j: (i, 0)),
            pl.BlockSpec((tm, wdt), lambda i, j: (i, 0)),
            pl.BlockSpec((1, wdt), lambda i, j: (0, 0)),
            pl.BlockSpec((1, wdt), lambda i, j: (0, 0)),
            pl.BlockSpec((None, wdt, tn), lambda i, j: (layer, 0, j)),
            pl.BlockSpec((None, wdt, tn), lambda i, j: (layer, 1, j)),
            pl.BlockSpec((tm, tn), lambda i, j: (i, j)),
        ],
        out_specs=pl.BlockSpec((tm, tn), lambda i, j: (i, j)),
        out_shape=jax.ShapeDtypeStruct((m, d), F32),
        scratch_shapes=[pltpu.VMEM((tm, wdt), BF16), pltpu.VMEM((tm, wdt), BF16)],
        compiler_params=_cparams(("arbitrary", "arbitrary")),
        name="outproj",
    )(a, b, ga, gb, w, w, h)


def _ffn_kernel(x_ref, g_ref, wg_ref, wu_ref, wd_ref, o_ref, xn_ref):
    @pl.when(pl.program_id(1) == 0)
    def _():
        x = x_ref[...]
        ms = jnp.mean(x * x, axis=-1, keepdims=True)
        xn_ref[...] = ((x * lax.rsqrt(ms + EPS)) * g_ref[...]).astype(BF16)
        o_ref[...] = x

    xn = xn_ref[...]
    gate = _dot(xn, wg_ref[...])
    up = _dot(xn, wu_ref[...])
    act = (gate / (1.0 + jnp.exp(-gate))) * up
    o_ref[...] += _dot(act.astype(BF16), wd_ref[...])


def _ffn(x, g, wg, wu, wd, layer, tm, tf):
    m, d = x.shape
    f = wg.shape[2]
    return pl.pallas_call(
        _ffn_kernel,
        grid=(m // tm, f // tf),
        in_specs=[
            pl.BlockSpec((tm, d), lambda i, j: (i, 0)),
            pl.BlockSpec((1, d), lambda i, j: (0, 0)),
            pl.BlockSpec((None, d, tf), lambda i, j: (layer, 0, j)),
            pl.BlockSpec((None, d, tf), lambda i, j: (layer, 0, j)),
            pl.BlockSpec((None, tf, d), lambda i, j: (layer, j, 0)),
        ],
        out_specs=pl.BlockSpec((tm, d), lambda i, j: (i, 0)),
        out_shape=jax.ShapeDtypeStruct((m, d), F32),
        scratch_shapes=[pltpu.VMEM((tm, d), BF16)],
        compiler_params=_cparams(("arbitrary", "arbitrary")),
        name="ffn",
    )(x, g, wg, wu, wd)


def _head_rmsnorm(x, g):
    r, c = x.shape
    same_head = (_iota((LANES, LANES), 0) // HEAD_DIM) == (_iota((LANES, LANES), 1) // HEAD_DIM)
    bd = jnp.where(same_head, 1.0 / HEAD_DIM, 0.0).astype(BF16)
    outs = []
    for t in range(c // LANES):
        xb = x[:, t * LANES:(t + 1) * LANES]
        ms = _dot((xb * xb).astype(BF16), bd)
        outs.append((xb * lax.rsqrt(ms + EPS)) * g[:, t * LANES:(t + 1) * LANES])
    return outs


def _half_select(half):
    src = _iota((LANES, LANES), 0)
    dst = _iota((LANES, LANES), 1)
    return jnp.where(src == (dst % HEAD_DIM) + HEAD_DIM * half, 1.0, 0.0).astype(BF16)


def _lane_halves(x):
    lo_half = _iota(x.shape, 1) < HEAD_DIM
    zero = jnp.zeros_like(x)
    return jnp.where(lo_half, x, zero), jnp.where(lo_half, zero, x)


def _merge_halves(lo, hi):
    return jnp.where(_iota(lo.shape, 1) < HEAD_DIM, lo, hi)


def _alibi_slope(h):
    return 2.0 ** (-8.0 * (h + 1) / SWA_HEADS)


def _softplus2(z2):
    neg_abs = lax.bitcast_convert_type(
        lax.bitcast_convert_type(z2, jnp.uint32) | jnp.uint32(0x80000000), F32)
    return jnp.maximum(z2, 0.0) + jnp.log2(1.0 + jnp.exp2(neg_abs))


def _suffix_matrix(n):
    return jnp.where(_iota((n, n), 0) > _iota((n, n), 1), 1.0, 0.0).astype(BF16)


def _sb_block(qe, kblk, vblk, suffix, carry, mask):
    z2 = _dot_nt(qe, kblk)
    sp = _softplus2(z2)
    if mask is not None:
        sp = jnp.where(mask, sp, 0.0)
    later = _dot(sp.astype(BF16), suffix)
    a = jnp.exp2((z2 - sp) - (later + carry))
    if mask is not None:
        a = jnp.where(mask, a, 0.0)
    contrib = _dot(a.astype(BF16), vblk)
    return contrib, carry + jnp.sum(sp, axis=-1, keepdims=True)


def _swa_head(qe, kdup, vdup, bias, mask, sink):
    s = jnp.where(mask, _dot_nt(qe, kdup) - bias, NEG)
    mx = jnp.maximum(jnp.max(s, axis=-1, keepdims=True), sink)
    e = jnp.exp(s - mx)
    den = jnp.sum(e, axis=-1, keepdims=True) + jnp.exp(sink - mx)
    return _dot(e.astype(BF16), vdup) / den


def _pad_rows(x, rows):
    return jnp.concatenate([x, jnp.zeros((rows - x.shape[0], x.shape[1]), x.dtype)], axis=0)


def _dup_kv(x_bf, kv_head):
    t = kv_head // 2
    return _dot(x_bf[:, t * LANES:(t + 1) * LANES], _half_select(kv_head % 2)).astype(BF16)


def _swa_tables():
    group = SWA_HEADS // SWA_KV_HEADS
    slope = jnp.asarray([_alibi_slope(h) * LOG2E for h in range(SWA_HEADS)], F32)
    slope = jnp.repeat(slope.reshape(SWA_KV_HEADS, group, 1, 1), BLOCK, axis=2)
    qi = jnp.arange(BLOCK, dtype=F32)[:, None]
    j = jnp.arange(BLOCK, dtype=F32)[None, :]
    dist_band = jnp.where(j <= qi, qi - j, qi - j + BLOCK)
    dist_meta = jnp.where(j < N_META, N_META + qi - j, 0.0)
    shape = (SWA_KV_HEADS, group * BLOCK, BLOCK)
    return ((slope * dist_band).reshape(shape), (slope * dist_meta).reshape(shape),
            jnp.broadcast_to(slope * BLOCK, slope.shape[:3] + (BLOCK,)).reshape(shape))


def _swa_kernel(q_ref, kc_ref, vc_ref, km_ref, vm_ref, qg_ref, kg_ref,
                band_ref, meta_ref, step_ref, sink_ref, o_ref, kd_ref, vd_ref):
    n = pl.program_id(1)
    group = SWA_HEADS // SWA_KV_HEADS
    rows = group * BLOCK

    def store_rows(k_rows, v_rows, start):
        kn = jnp.concatenate(_head_rmsnorm(k_rows, kg_ref[...]), axis=1).astype(BF16)
        vb = v_rows.astype(BF16)
        for g in range(SWA_KV_HEADS):
            kd_ref[g, start:start + BLOCK] = _dup_kv(kn, g)
            vd_ref[g, start:start + BLOCK] = _dup_kv(vb, g)

    @pl.when(n == 0)
    def _():
        store_rows(_pad_rows(km_ref[...], BLOCK), _pad_rows(vm_ref[...], BLOCK), 0)
        kd_ref[:, 2 * BLOCK:3 * BLOCK] = jnp.zeros((SWA_KV_HEADS, BLOCK, LANES), BF16)
        vd_ref[:, 2 * BLOCK:3 * BLOCK] = jnp.zeros((SWA_KV_HEADS, BLOCK, LANES), BF16)

    kd_ref[:, BLOCK:2 * BLOCK] = kd_ref[:, 2 * BLOCK:3 * BLOCK]
    vd_ref[:, BLOCK:2 * BLOCK] = vd_ref[:, 2 * BLOCK:3 * BLOCK]
    store_rows(kc_ref[...], vc_ref[...], 2 * BLOCK)
    qn = [x.astype(BF16) for x in _head_rmsnorm(q_ref[...], qg_ref[...])]

    qi = _iota((rows, BLOCK), 0) % BLOCK
    j = _iota((rows, BLOCK), 1)
    in_cur = j <= qi
    band_ok = in_cur | (n > 0)
    meta_ok = j < N_META
    blocks_before = n.astype(F32)

    logits, weights, dens = {}, {}, {}

    def score(g):
        logits[g] = _dot_nt(jnp.concatenate(_lane_halves(qn[2 * g]) + _lane_halves(qn[2 * g + 1]), axis=0),
                            kd_ref[g])

    def softmax(g):
        sink = sink_ref[g]
        s = logits.pop(g)
        s_band = jnp.where(in_cur, s[:, 2 * BLOCK:], s[:, BLOCK:2 * BLOCK]) - band_ref[g]
        s_band = jnp.where(band_ok, s_band, NEG)
        s_meta = jnp.where(meta_ok, (s[:, :BLOCK] - meta_ref[g]) - step_ref[g] * blocks_before, NEG)
        mx = jnp.maximum(jnp.max(jnp.maximum(s_band, s_meta), axis=-1, keepdims=True), sink)
        e_band = jnp.exp2(s_band - mx)
        e_meta = jnp.exp2(s_meta - mx)
        dens[g] = jnp.sum(e_band + e_meta, axis=-1, keepdims=True) + jnp.exp2(sink - mx)
        weights[g] = jnp.concatenate(
            [e_meta, jnp.where(in_cur, 0.0, e_band), jnp.where(in_cur, e_band, 0.0)], axis=1).astype(BF16)

    def attend(g):
        o = _dot(weights.pop(g), vd_ref[g]) / dens.pop(g)
        for t in range(2):
            o_ref[:, (2 * g + t) * LANES:(2 * g + t + 1) * LANES] = _merge_halves(
                o[2 * t * BLOCK:(2 * t + 1) * BLOCK], o[(2 * t + 1) * BLOCK:(2 * t + 2) * BLOCK])

    for stage, g in ((score, 0), (score, 1), (softmax, 0), (score, 2), (softmax, 1), (attend, 0),
                     (score, 3), (softmax, 2), (attend, 1), (softmax, 3), (attend, 2), (attend, 3)):
        stage(g)


def _swa(proj, proj_meta, sinks, qg, kg, tables, batch):
    m = proj.shape[0]
    nb = m // batch // BLOCK
    qw = SWA_HEADS * HEAD_DIM
    kvw = SWA_KV_HEADS * HEAD_DIM
    group = SWA_HEADS // SWA_KV_HEADS
    kcol, vcol = qw // kvw, qw // kvw + 1
    row = lambda b, n: b * nb + n
    whole = lambda shape: pl.BlockSpec(shape, lambda b, n: (0,) * len(shape))
    sink_col = jnp.broadcast_to((sinks * LOG2E).reshape(SWA_KV_HEADS, group, 1, 1),
                                (SWA_KV_HEADS, group, BLOCK, LANES))
    sink_col = sink_col.reshape(SWA_KV_HEADS, group * BLOCK, LANES)
    band, meta, step = tables
    return pl.pallas_call(
        _swa_kernel,
        grid=(batch, nb),
        in_specs=[
            pl.BlockSpec((BLOCK, qw), lambda b, n: (row(b, n), 0)),
            pl.BlockSpec((BLOCK, kvw), lambda b, n: (row(b, n), kcol)),
            pl.BlockSpec((BLOCK, kvw), lambda b, n: (row(b, n), vcol)),
            pl.BlockSpec((N_META, kvw), lambda b, n: (0, kcol)),
            pl.BlockSpec((N_META, kvw), lambda b, n: (0, vcol)),
            pl.BlockSpec((1, qw), lambda b, n: (0, 0)),
            pl.BlockSpec((1, kvw), lambda b, n: (0, 0)),
            whole(band.shape), whole(meta.shape), whole(step.shape), whole(sink_col.shape),
        ],
        out_specs=pl.BlockSpec((BLOCK, qw), lambda b, n: (row(b, n), 0)),
        out_shape=jax.ShapeDtypeStruct((m, qw), F32),
        scratch_shapes=[pltpu.VMEM((SWA_KV_HEADS, 3 * BLOCK, LANES), BF16),
                        pltpu.VMEM((SWA_KV_HEADS, 3 * BLOCK, LANES), BF16)],
        compiler_params=_cparams(("arbitrary", "arbitrary")),
        name="swa",
    )(proj, proj, proj, proj_meta, proj_meta, qg, kg, band, meta, step, sink_col)


def _sb_kernel(q_ref, kbf_ref, vbf_ref, km_ref, vm_ref, o_ref,
               kmp_ref, vmp_ref, acc_ref, car_ref, more_ref, *, tq, hp):
    i = pl.program_id(2)

    @pl.when(i == 0)
    def _():
        kmp_ref[...] = jnp.zeros_like(kmp_ref)
        vmp_ref[...] = jnp.zeros_like(vmp_ref)
        kmp_ref[0:N_META, :] = km_ref[...].astype(BF16)
        vmp_ref[0:N_META, :] = vm_ref[...].astype(BF16)

    lanes = lambda p: slice(p * LANES, (p + 1) * LANES)
    rows = 2 * tq

    def queries():
        return [jnp.concatenate(_lane_halves(q_ref[:, lanes(p)]), axis=0) for p in range(hp)]

    def note_carries(carries):
        m = functools.reduce(jnp.minimum, carries)
        while m.shape[0] > 8:
            half = m.shape[0] // 2
            m = jnp.minimum(m[:half], m[half:])
        more_ref[0] = (jnp.min(m) <= F32_UNDERFLOW_LOG2).astype(jnp.int32)

    def block_step(kv_of_pair, width, mask, first):
        qs = queries()
        kv = [kv_of_pair(p) for p in range(hp)]
        z2 = [_dot_nt(qs[p], kv[p][0]) for p in range(hp)]
        sp = [_softplus2(z) for z in z2]
        if mask is not None:
            sp = [jnp.where(mask, s, 0.0) for s in sp]
        zs = [z2[p] - sp[p] for p in range(hp)]
        later = _dot(jnp.concatenate([s.astype(BF16) for s in sp], axis=0), _suffix_matrix(width))
        carries = []
        for p in range(hp):
            carry = 0.0 if first else car_ref[p]
            a = jnp.exp2(zs[p] - (later[p * rows:(p + 1) * rows] + carry))
            if mask is not None:
                a = jnp.where(mask, a, 0.0)
            d = _dot(a.astype(BF16), kv[p][1])
            acc_ref[p] = d if first else acc_ref[p] + d
            carries.append(carry + jnp.sum(sp[p], axis=-1, keepdims=True))
            car_ref[p] = carries[-1]
        note_carries(carries)

    def kv_block(start):
        return lambda p: (kbf_ref[pl.ds(start, tq), lanes(p)], vbf_ref[pl.ds(start, tq), lanes(p)])

    below_diag = lambda: _iota((rows, tq), 1) < (_iota((rows, tq), 0) % tq)
    prev_start = pl.multiple_of(jnp.maximum(i - 1, 0) * tq, tq)

    @pl.when(i == 0)
    def _():
        block_step(kv_block(0), tq, below_diag(), True)

    @pl.when(i > 0)
    def _():
        qs = queries()
        in_cur = below_diag()
        cur_start = pl.multiple_of(i * tq, tq)
        zs, spb = [], []
        for p in range(hp):
            z_cur = _dot_nt(qs[p], kbf_ref[pl.ds(cur_start, tq), lanes(p)])
            z_prev = _dot_nt(qs[p], kbf_ref[pl.ds(prev_start, tq), lanes(p)])
            z2 = jnp.where(in_cur, z_cur, z_prev)
            sp = _softplus2(z2)
            zs.append(z2 - sp)
            spb.append(sp.astype(BF16))
        zero = jnp.zeros((rows, tq), BF16)
        prev_part = lambda x: jnp.where(in_cur, zero, x)
        cur_part = lambda x: jnp.where(in_cur, x, zero)
        suf = _dot(jnp.concatenate(spb, axis=0), _suffix_matrix(tq))
        prev_tot = _dot(jnp.concatenate([prev_part(s) for s in spb], axis=0), jnp.ones((tq, LANES), BF16))
        carries = []
        for p in range(hp):
            sf = suf[p * rows:(p + 1) * rows]
            pt = prev_tot[p * rows:(p + 1) * rows]
            total = sf[:, 0:1] + spb[p][:, 0:1].astype(F32)
            widen = lambda x: jnp.concatenate([x] * (tq // LANES), axis=1)
            later = sf + jnp.where(in_cur, widen(-pt), widen(total - pt))
            a = jnp.exp2(zs[p] - later).astype(BF16)
            v2 = jnp.concatenate([vbf_ref[pl.ds(prev_start, tq), lanes(p)],
                                  vbf_ref[pl.ds(cur_start, tq), lanes(p)]], axis=0)
            acc_ref[p] = _dot(jnp.concatenate([prev_part(a), cur_part(a)], axis=1), v2)
            car_ref[p] = total
            carries.append(total)
        note_carries(carries)

    @pl.when(more_ref[0] > 0)
    def _():
        @pl.when(i > 0)
        def _():
            block_step(kv_block(prev_start), tq, below_diag(), False)

        def body(t):
            block_step(kv_block(pl.multiple_of((i - 2 - t) * tq, tq)), tq, None, False)
            return t + 1

        lax.while_loop(lambda t: (t < i - 1) & (more_ref[0] > 0), body, jnp.int32(0))

        @pl.when(more_ref[0] > 0)
        def _():
            block_step(lambda p: (kmp_ref[:, lanes(p)], vmp_ref[:, lanes(p)]), LANES,
                       _iota((rows, LANES), 1) < N_META, False)

    for p in range(hp):
        o_ref[:, lanes(p)] = _merge_halves(acc_ref[p, 0:tq], acc_ref[p, tq:rows])


def _sb(proj_b, proj_meta_b, batch, tq, hp):
    m = proj_b.shape[0]
    seq = m // batch
    nq = seq // tq
    w = hp * LANES
    steps = SB_HEADS * HEAD_DIM // w
    return pl.pallas_call(
        functools.partial(_sb_kernel, tq=tq, hp=hp),
        grid=(batch, steps, nq),
        in_specs=[
            pl.BlockSpec((tq, w), lambda b, p, i: (b * nq + i, p)),
            pl.BlockSpec((seq, w), lambda b, p, i: (b, steps + p)),
            pl.BlockSpec((seq, w), lambda b, p, i: (b, 2 * steps + p)),
            pl.BlockSpec((N_META, w), lambda b, p, i: (0, steps + p)),
            pl.BlockSpec((N_META, w), lambda b, p, i: (0, 2 * steps + p)),
        ],
        out_specs=pl.BlockSpec((tq, w), lambda b, p, i: (b * nq + i, p)),
        out_shape=jax.ShapeDtypeStruct((m, SB_HEADS * HEAD_DIM), F32),
        scratch_shapes=[pltpu.VMEM((LANES, w), BF16), pltpu.VMEM((LANES, w), BF16),
                        pltpu.VMEM((hp, 2 * tq, LANES), F32), pltpu.VMEM((hp, 2 * tq, 1), F32),
                        pltpu.SMEM((1,), jnp.int32)],
        compiler_params=_cparams(("arbitrary", "arbitrary", "arbitrary")),
        name="stickbreak",
    )(proj_b, proj_b, proj_b, proj_meta_b, proj_meta_b)


def _meta_attn_kernel(sink_ref, p_ref, pb_ref, qg_ref, kg_ref, oa_ref, ob_ref):
    qw = SWA_HEADS * HEAD_DIM
    kvw = SWA_KV_HEADS * HEAD_DIM
    group = SWA_HEADS // SWA_KV_HEADS
    x = p_ref[...]
    xk = _pad_rows(x, LANES)
    qi = _iota((N_META, LANES), 0)
    kj = _iota((N_META, LANES), 1)

    qn = [(t * SCALE).astype(BF16) for t in _head_rmsnorm(x[:, 0:qw], qg_ref[...])]
    kn = jnp.concatenate(_head_rmsnorm(xk[:, qw:qw + kvw], kg_ref[...]), axis=1).astype(BF16)
    vb = xk[:, qw + kvw:qw + 2 * kvw].astype(BF16)
    delta_f = (qi - kj).astype(F32)
    causal = kj <= qi
    for pair in range(SWA_HEADS // 2):
        kv = (2 * pair) // group
        kdup, vdup = _dup_kv(kn, kv), _dup_kv(vb, kv)
        outs = []
        for e, qe in enumerate(_lane_halves(qn[pair])):
            h = 2 * pair + e
            outs.append(_swa_head(qe, kdup, vdup, _alibi_slope(h) * delta_f, causal, sink_ref[h]))
        oa_ref[:, pair * LANES:(pair + 1) * LANES] = _merge_halves(outs[0], outs[1])

    sbw = SB_HEADS * HEAD_DIM
    xb = pb_ref[...]
    xbk = _pad_rows(xb, LANES)
    strict = kj < qi
    suffix = _suffix_matrix(LANES)
    zero_c = jnp.zeros((N_META, 1), F32)
    for pair in range(SB_HEADS // 2):
        sl = lambda src, part: src[:, part * sbw + pair * LANES: part * sbw + (pair + 1) * LANES]
        outs = [_sb_block(qe, sl(xbk, 1), sl(xbk, 2), suffix, zero_c, strict)[0]
                for qe in _lane_halves(sl(xb, 0))]
        ob_ref[:, pair * LANES:(pair + 1) * LANES] = _merge_halves(outs[0], outs[1])


def _meta_attn(proj_meta, proj_meta_b, sinks, qg, kg, batch):
    pw, pbw = proj_meta.shape[1], proj_meta_b.shape[1]
    qw = SWA_HEADS * HEAD_DIM
    kvw = SWA_KV_HEADS * HEAD_DIM
    sbw = SB_HEADS * HEAD_DIM
    return pl.pallas_call(
        _meta_attn_kernel,
        grid=(batch,),
        in_specs=[
            pl.BlockSpec(memory_space=pltpu.SMEM),
            pl.BlockSpec((N_META, pw), lambda b: (b, 0)),
            pl.BlockSpec((N_META, pbw), lambda b: (b, 0)),
            pl.BlockSpec((1, qw), lambda b: (0, 0)),
            pl.BlockSpec((1, kvw), lambda b: (0, 0)),
        ],
        out_specs=[pl.BlockSpec((N_META, qw), lambda b: (b, 0)),
                   pl.BlockSpec((N_META, sbw), lambda b: (b, 0))],
        out_shape=[jax.ShapeDtypeStruct((batch * N_META, qw), F32),
                   jax.ShapeDtypeStruct((batch * N_META, sbw), F32)],
        compiler_params=_cparams(("arbitrary",)),
        name="meta_attn",
    )(sinks, proj_meta, proj_meta_b, qg, kg)


def kernel(x, meta_tokens, attn_norm_g, w_in, q_norm_g, k_norm_g, attn_sinks,
           swa_out_g, sb_out_g, w_o, ffn_norm_g, w_gate, w_up, w_down):
    batch, seq, d = x.shape
    depth = w_in.shape[0]
    assert (batch * seq) % 1024 == 0 and seq % 256 == 0 and meta_tokens.shape[0] == N_META

    tm = 512
    tmeta = N_META
    tm_in = 1024
    tn_in = 1536
    assert tn_in == (SWA_HEADS + 2 * SWA_KV_HEADS) * HEAD_DIM
    tn_out = 2048
    tm_ffn = 1024
    tf = 512
    tq = 256
    hp = 8

    h = x.reshape(batch * seq, d)
    hm = meta_tokens.astype(x.dtype)

    q_start = (SWA_HEADS + 2 * SWA_KV_HEADS) * HEAD_DIM
    col_scale = jnp.ones((w_in.shape[2],), F32).at[q_start:q_start + SB_HEADS * HEAD_DIM].set(SB_QSCALE)
    w_in_b, w_o_b = (w_in * col_scale).astype(BF16), w_o.astype(BF16)
    w_gate_b, w_up_b, w_down_b = w_gate.astype(BF16), w_up.astype(BF16), w_down.astype(BF16)

    swa_tables = _swa_tables()
    for l in range(depth):
        g_attn = attn_norm_g[l].reshape(1, d)
        qg = jnp.tile(q_norm_g[l], SWA_HEADS).reshape(1, -1)
        kg = jnp.tile(k_norm_g[l], SWA_KV_HEADS).reshape(1, -1)
        sinks = attn_sinks[l].astype(F32)
        ga, gb = swa_out_g[l].reshape(1, -1), sb_out_g[l].reshape(1, -1)
        g_ffn = ffn_norm_g[l].reshape(1, d)

        proj, proj_b = _norm_matmul(h, g_attn, w_in_b, l, tm_in, tn_in)
        proj_m, proj_mb = _norm_matmul(hm, g_attn, w_in_b, l, tmeta, tn_in)

        out_a = _swa(proj, proj_m, sinks, qg * (SCALE * LOG2E), kg, swa_tables, batch)
        out_b = _sb(proj_b, proj_mb, batch, tq, hp)
        out_am, out_bm = _meta_attn(proj_m, proj_mb, sinks, qg, kg, 1)

        h = _outproj(out_a, out_b, ga, gb, w_o_b, l, h, tm, tn_out)
        hm = _outproj(out_am, out_bm, ga, gb, w_o_b, l, hm, tmeta, tn_out)

        h = _ffn(h, g_ffn, w_gate_b, w_up_b, w_down_b, l, tm_ffn, tf)
        hm = _ffn(hm, g_ffn, w_gate_b, w_up_b, w_down_b, l, tmeta, tf)

    return h.reshape(batch, seq, d)
```

```python
import functools

import jax
import jax.numpy as jnp
from jax import lax
from jax.experimental import pallas as pl
from jax.experimental.pallas import tpu as pltpu

F32 = jnp.float32
BF16 = jnp.bfloat16

N_META = 16
BLOCK = 128
HEAD_DIM = 64
LANES = 128
SWA_HEADS = 16
SWA_KV_HEADS = 4
SB_HEADS = 16
EPS = 1e-6
NEG = -1e30
SCALE = HEAD_DIM ** -0.5
LOG2E = 1.4426950408889634
SB_QSCALE = SCALE * LOG2E
F32_UNDERFLOW_LOG2 = 160.0
VMEM_LIMIT = 56 * 1024 * 1024


def _cparams(sem):
    return pltpu.CompilerParams(dimension_semantics=sem, vmem_limit_bytes=VMEM_LIMIT)


def _dot(a, b):
    return jnp.dot(a, b, preferred_element_type=F32)


def _dot_nt(a, b):
    return lax.dot_general(a, b, (((1,), (1,)), ((), ())), preferred_element_type=F32)


def _iota(shape, dim):
    return lax.broadcasted_iota(jnp.int32, shape, dim)


def _norm_mm_kernel(x_ref, g_ref, w_ref, o_ref, ob_ref, xn_ref):
    @pl.when(pl.program_id(1) == 0)
    def _():
        x = x_ref[...]
        ms = jnp.mean(x * x, axis=-1, keepdims=True)
        xn_ref[...] = ((x * lax.rsqrt(ms + EPS)) * g_ref[...]).astype(BF16)

    y = _dot(xn_ref[...], w_ref[...])
    o_ref[...] = y
    ob_ref[...] = y.astype(BF16)


def _norm_matmul(x, g, w, layer, tm, tn):
    m, d = x.shape
    n = w.shape[2]
    return pl.pallas_call(
        _norm_mm_kernel,
        grid=(m // tm, n // tn),
        in_specs=[
            pl.BlockSpec((tm, d), lambda i, j: (i, 0)),
            pl.BlockSpec((1, d), lambda i, j: (0, 0)),
            pl.BlockSpec((None, d, tn), lambda i, j: (layer, 0, j)),
        ],
        out_specs=[pl.BlockSpec((tm, tn), lambda i, j: (i, j)),
                   pl.BlockSpec((tm, tn), lambda i, j: (i, jnp.maximum(j - 1, 0)))],
        out_shape=[jax.ShapeDtypeStruct((m, n), F32), jax.ShapeDtypeStruct((m, n - tn), BF16)],
        scratch_shapes=[pltpu.VMEM((tm, d), BF16)],
        compiler_params=_cparams(("arbitrary", "arbitrary")),
        name="norm_inproj",
    )(x, g, w)


def _outproj_kernel(a_ref, b_ref, ga_ref, gb_ref, wa_ref, wb_ref, h_ref, o_ref, an_ref, bn_ref):
    @pl.when(pl.program_id(1) == 0)
    def _():
        for src, g, dst in ((a_ref, ga_ref, an_ref), (b_ref, gb_ref, bn_ref)):
            x = src[...]
            ms = jnp.mean(x * x, axis=-1, keepdims=True)
            dst[...] = ((x * lax.rsqrt(ms + EPS)) * g[...]).astype(BF16)

    o_ref[...] = h_ref[...] + (_dot(an_ref[...], wa_ref[...]) + _dot(bn_ref[...], wb_ref[...]))


def _outproj(a, b, ga, gb, w, layer, h, tm, tn):
    m, wdt = a.shape
    d = h.shape[1]
    return pl.pallas_call(
        _outproj_kernel,
        grid=(m // tm, d // tn),
        in_specs=[
            pl.BlockSpec((tm, wdt), lambda i, j: (i, 0)),
            pl.BlockSpec((tm, wdt), lambda i, j: (i, 0)),
            pl.BlockSpec((1, wdt), lambda i, j: (0, 0)),
            pl.BlockSpec((1, wdt), lambda i, j: (0, 0)),
            pl.BlockSpec((None, wdt, tn), lambda i, j: (layer, 0, j)),
            pl.BlockSpec((None, wdt, tn), lambda i, j: (layer, 1, j)),
            pl.BlockSpec((tm, tn), lambda i, j: (i, j)),
        ],
        out_specs=pl.BlockSpec((tm, tn), lambda i, j: (i, j)),
        out_shape=jax.ShapeDtypeStruct((m, d), F32),
        scratch_shapes=[pltpu.VMEM((tm, wdt), BF16), pltpu.VMEM((tm, wdt), BF16)],
        compiler_params=_cparams(("arbitrary", "arbitrary")),
        name="outproj",
    )(a, b, ga, gb, w, w, h)


def _ffn_kernel(x_ref, g_ref, wg_ref, wu_ref, wd_ref, o_ref, xn_ref):
    @pl.when(pl.program_id(1) == 0)
    def _():
        x = x_ref[...]
        ms = jnp.mean(x * x, axis=-1, keepdims=True)
        xn_ref[...] = ((x * lax.rsqrt(ms + EPS)) * g_ref[...]).astype(BF16)
        o_ref[...] = x

    xn = xn_ref[...]
    gate = _dot(xn, wg_ref[...])
    up = _dot(xn, wu_ref[...])
    act = (gate / (1.0 + jnp.exp(-gate))) * up
    o_ref[...] += _dot(act.astype(BF16), wd_ref[...])


def _ffn(x, g, wg, wu, wd, layer, tm, tf):
    m, d = x.shape
    f = wg.shape[2]
    return pl.pallas_call(
        _ffn_kernel,
        grid=(m // tm, f // tf),
        in_specs=[
            pl.BlockSpec((tm, d), lambda i, j: (i, 0)),
            pl.BlockSpec((1, d), lambda i, j: (0, 0)),
            pl.BlockSpec((None, d, tf), lambda i, j: (layer, 0, j)),
            pl.BlockSpec((None, d, tf), lambda i, j: (layer, 0, j)),
            pl.BlockSpec((None, tf, d), lambda i, j: (layer, j, 0)),
        ],
        out_specs=pl.BlockSpec((tm, d), lambda i, j: (i, 0)),
        out_shape=jax.ShapeDtypeStruct((m, d), F32),
        scratch_shapes=[pltpu.VMEM((tm, d), BF16)],
        compiler_params=_cparams(("arbitrary", "arbitrary")),
        name="ffn",
    )(x, g, wg, wu, wd)


def _head_rmsnorm(x, g):
    r, c = x.shape
    same_head = (_iota((LANES, LANES), 0) // HEAD_DIM) == (_iota((LANES, LANES), 1) // HEAD_DIM)
    bd = jnp.where(same_head, 1.0 / HEAD_DIM, 0.0).astype(BF16)
    outs = []
    for t in range(c // LANES):
        xb = x[:, t * LANES:(t + 1) * LANES]
        ms = _dot((xb * xb).astype(BF16), bd)
        outs.append((xb * lax.rsqrt(ms + EPS)) * g[:, t * LANES:(t + 1) * LANES])
    return outs


def _half_select(half):
    src = _iota((LANES, LANES), 0)
    dst = _iota((LANES, LANES), 1)
    return jnp.where(src == (dst % HEAD_DIM) + HEAD_DIM * half, 1.0, 0.0).astype(BF16)


def _lane_halves(x):
    lo_half = _iota(x.shape, 1) < HEAD_DIM
    zero = jnp.zeros_like(x)
    return jnp.where(lo_half, x, zero), jnp.where(lo_half, zero, x)


def _merge_halves(lo, hi):
    return jnp.where(_iota(lo.shape, 1) < HEAD_DIM, lo, hi)


def _alibi_slope(h):
    return 2.0 ** (-8.0 * (h + 1) / SWA_HEADS)


def _softplus2(z2):
    neg_abs = lax.bitcast_convert_type(
        lax.bitcast_convert_type(z2, jnp.uint32) | jnp.uint32(0x80000000), F32)
    return jnp.maximum(z2, 0.0) + jnp.log2(1.0 + jnp.exp2(neg_abs))


def _suffix_matrix(n):
    return jnp.where(_iota((n, n), 0) > _iota((n, n), 1), 1.0, 0.0).astype(BF16)


def _sb_block(qe, kblk, vblk, suffix, carry, mask):
    z2 = _dot_nt(qe, kblk)
    sp = _softplus2(z2)
    if mask is not None:
        sp = jnp.where(mask, sp, 0.0)
    later = _dot(sp.astype(BF16), suffix)
    a = jnp.exp2((z2 - sp) - (later + carry))
    if mask is not None:
        a = jnp.where(mask, a, 0.0)
    contrib = _dot(a.astype(BF16), vblk)
    return contrib, carry + jnp.sum(sp, axis=-1, keepdims=True)


def _swa_head(qe, kdup, vdup, bias, mask, sink):
    s = jnp.where(mask, _dot_nt(qe, kdup) - bias, NEG)
    mx = jnp.maximum(jnp.max(s, axis=-1, keepdims=True), sink)
    e = jnp.exp(s - mx)
    den = jnp.sum(e, axis=-1, keepdims=True) + jnp.exp(sink - mx)
    return _dot(e.astype(BF16), vdup) / den


def _pad_rows(x, rows):
    return jnp.concatenate([x, jnp.zeros((rows - x.shape[0], x.shape[1]), x.dtype)], axis=0)


def _dup_kv(x_bf, kv_head):
    t = kv_head // 2
    return _dot(x_bf[:, t * LANES:(t + 1) * LANES], _half_select(kv_head % 2)).astype(BF16)


def _swa_tables():
    group = SWA_HEADS // SWA_KV_HEADS
    slope = jnp.asarray([_alibi_slope(h) * LOG2E for h in range(SWA_HEADS)], F32)
    slope = jnp.repeat(slope.reshape(SWA_KV_HEADS, group, 1, 1), BLOCK, axis=2)
    qi = jnp.arange(BLOCK, dtype=F32)[:, None]
    j = jnp.arange(BLOCK, dtype=F32)[None, :]
    dist_band = jnp.where(j <= qi, qi - j, qi - j + BLOCK)
    dist_meta = jnp.where(j < N_META, N_META + qi - j, 0.0)
    shape = (SWA_KV_HEADS, group * BLOCK, BLOCK)
    return ((slope * dist_band).reshape(shape), (slope * dist_meta).reshape(shape),
            jnp.broadcast_to(slope * BLOCK, slope.shape[:3] + (BLOCK,)).reshape(shape))


def _swa_kernel(q_ref, kc_ref, vc_ref, km_ref, vm_ref, qg_ref, kg_ref,
                band_ref, meta_ref, step_ref, sink_ref, o_ref, kd_ref, vd_ref, *, nsub):
    n = pl.program_id(1)
    group = SWA_HEADS // SWA_KV_HEADS
    rows = group * BLOCK

    def store_rows(k_rows, v_rows, start):
        kn = jnp.concatenate(_head_rmsnorm(k_rows, kg_ref[...]), axis=1).astype(BF16)
        vb = v_rows.astype(BF16)
        for g in range(SWA_KV_HEADS):
            kd_ref[g, start:start + BLOCK] = _dup_kv(kn, g)
            vd_ref[g, start:start + BLOCK] = _dup_kv(vb, g)

    @pl.when(n == 0)
    def _():
        store_rows(_pad_rows(km_ref[...], BLOCK), _pad_rows(vm_ref[...], BLOCK), 0)
        kd_ref[:, 2 * BLOCK:3 * BLOCK] = jnp.zeros((SWA_KV_HEADS, BLOCK, LANES), BF16)
        vd_ref[:, 2 * BLOCK:3 * BLOCK] = jnp.zeros((SWA_KV_HEADS, BLOCK, LANES), BF16)

    qi = _iota((rows, BLOCK), 0) % BLOCK
    j = _iota((rows, BLOCK), 1)
    in_cur = j <= qi
    meta_ok = j < N_META

    def one_block(sub):
        blk = nsub * n + sub
        rs = slice(sub * BLOCK, (sub + 1) * BLOCK)
        kd_ref[:, BLOCK:2 * BLOCK] = kd_ref[:, 2 * BLOCK:3 * BLOCK]
        vd_ref[:, BLOCK:2 * BLOCK] = vd_ref[:, 2 * BLOCK:3 * BLOCK]
        store_rows(kc_ref[rs, :], vc_ref[rs, :], 2 * BLOCK)
        qn = [x.astype(BF16) for x in _head_rmsnorm(q_ref[rs, :], qg_ref[...])]
        band_ok = in_cur | (blk > 0)
        blocks_before = blk.astype(F32)
        logits, weights, dens = {}, {}, {}

        def score(g):
            logits[g] = _dot_nt(
                jnp.concatenate(_lane_halves(qn[2 * g]) + _lane_halves(qn[2 * g + 1]), axis=0), kd_ref[g])

        def softmax(g):
            sink = sink_ref[g]
            s = logits.pop(g)
            s_band = jnp.where(in_cur, s[:, 2 * BLOCK:], s[:, BLOCK:2 * BLOCK]) - band_ref[g]
            s_band = jnp.where(band_ok, s_band, NEG)
            s_meta = jnp.where(meta_ok, (s[:, :BLOCK] - meta_ref[g]) - step_ref[g] * blocks_before, NEG)
            mx = jnp.maximum(jnp.max(jnp.maximum(s_band, s_meta), axis=-1, keepdims=True), sink)
            e_band = jnp.exp2(s_band - mx)
            e_meta = jnp.exp2(s_meta - mx)
            dens[g] = jnp.sum(e_band + e_meta, axis=-1, keepdims=True) + jnp.exp2(sink - mx)
            weights[g] = jnp.concatenate(
                [e_meta, jnp.where(in_cur, 0.0, e_band), jnp.where(in_cur, e_band, 0.0)], axis=1).astype(BF16)

        def attend(g):
            o = _dot(weights.pop(g), vd_ref[g]) / dens.pop(g)
            for t in range(2):
                o_ref[rs, (2 * g + t) * LANES:(2 * g + t + 1) * LANES] = _merge_halves(
                    o[2 * t * BLOCK:(2 * t + 1) * BLOCK], o[(2 * t + 1) * BLOCK:(2 * t + 2) * BLOCK])

        for stage, g in ((score, 0), (score, 1), (softmax, 0), (score, 2), (softmax, 1), (attend, 0),
                         (score, 3), (softmax, 2), (attend, 1), (softmax, 3), (attend, 2), (attend, 3)):
            stage(g)

    for sub in range(nsub):
        one_block(sub)


def _swa(proj, proj_meta, sinks, qg, kg, tables, batch, nsub):
    m = proj.shape[0]
    tb = nsub * BLOCK
    nb = m // batch // tb
    qw = SWA_HEADS * HEAD_DIM
    kvw = SWA_KV_HEADS * HEAD_DIM
    group = SWA_HEADS // SWA_KV_HEADS
    kcol, vcol = qw // kvw, qw // kvw + 1
    row = lambda b, n: b * nb + n
    whole = lambda shape: pl.BlockSpec(shape, lambda b, n: (0,) * len(shape))
    sink_col = jnp.broadcast_to((sinks * LOG2E).reshape(SWA_KV_HEADS, group, 1, 1),
                                (SWA_KV_HEADS, group, BLOCK, LANES))
    sink_col = sink_col.reshape(SWA_KV_HEADS, group * BLOCK, LANES)
    band, meta, step = tables
    return pl.pallas_call(
        functools.partial(_swa_kernel, nsub=nsub),
        grid=(batch, nb),
        in_specs=[
            pl.BlockSpec((tb, qw), lambda b, n: (row(b, n), 0)),
            pl.BlockSpec((tb, kvw), lambda b, n: (row(b, n), kcol)),
            pl.BlockSpec((tb, kvw), lambda b, n: (row(b, n), vcol)),
            pl.BlockSpec((N_META, kvw), lambda b, n: (0, kcol)),
            pl.BlockSpec((N_META, kvw), lambda b, n: (0, vcol)),
            pl.BlockSpec((1, qw), lambda b, n: (0, 0)),
            pl.BlockSpec((1, kvw), lambda b, n: (0, 0)),
            whole(band.shape), whole(meta.shape), whole(step.shape), whole(sink_col.shape),
        ],
        out_specs=pl.BlockSpec((tb, qw), lambda b, n: (row(b, n), 0)),
        out_shape=jax.ShapeDtypeStruct((m, qw), F32),
        scratch_shapes=[pltpu.VMEM((SWA_KV_HEADS, 3 * BLOCK, LANES), BF16),
                        pltpu.VMEM((SWA_KV_HEADS, 3 * BLOCK, LANES), BF16)],
        compiler_params=_cparams(("arbitrary", "arbitrary")),
        name="swa",
    )(proj, proj, proj, proj_meta, proj_meta, qg, kg, band, meta, step, sink_col)


def _sb_kernel(*refs, tq, hp, n_q):
    q_refs = refs[:n_q]
    kbf_ref, vbf_ref, km_ref, vm_ref, o_ref, kmp_ref, vmp_ref, acc_ref, car_ref, more_ref = refs[n_q:]
    per_q = hp // n_q
    i = pl.program_id(2)

    @pl.when(i == 0)
    def _():
        kmp_ref[...] = jnp.zeros_like(kmp_ref)
        vmp_ref[...] = jnp.zeros_like(vmp_ref)
        kmp_ref[0:N_META, :] = km_ref[...].astype(BF16)
        vmp_ref[0:N_META, :] = vm_ref[...].astype(BF16)

    lanes = lambda p: slice(p * LANES, (p + 1) * LANES)
    rows = 2 * tq

    def queries():
        return [jnp.concatenate(_lane_halves(
                    (q_refs[p // per_q][:, lanes(p % per_q)] * SB_QSCALE).astype(BF16)), axis=0)
                for p in range(hp)]

    def note_carries(carries):
        m = functools.reduce(jnp.minimum, carries)
        while m.shape[0] > 8:
            half = m.shape[0] // 2
            m = jnp.minimum(m[:half], m[half:])
        more_ref[0] = (jnp.min(m) <= F32_UNDERFLOW_LOG2).astype(jnp.int32)

    def block_step(kv_of_pair, width, mask, first):
        qs = queries()
        kv = [kv_of_pair(p) for p in range(hp)]
        z2 = [_dot_nt(qs[p], kv[p][0]) for p in range(hp)]
        sp = [_softplus2(z) for z in z2]
        if mask is not None:
            sp = [jnp.where(mask, s, 0.0) for s in sp]
        zs = [z2[p] - sp[p] for p in range(hp)]
        later = _dot(jnp.concatenate([s.astype(BF16) for s in sp], axis=0), _suffix_matrix(width))
        carries = []
        for p in range(hp):
            carry = 0.0 if first else car_ref[p]
            a = jnp.exp2(zs[p] - (later[p * rows:(p + 1) * rows] + carry))
            if mask is not None:
                a = jnp.where(mask, a, 0.0)
            d = _dot(a.astype(BF16), kv[p][1])
            acc_ref[p] = d if first else acc_ref[p] + d
            carries.append(carry + jnp.sum(sp[p], axis=-1, keepdims=True))
            car_ref[p] = carries[-1]
        note_carries(carries)

    def kv_block(start):
        return lambda p: (kbf_ref[pl.ds(start, tq), lanes(p)], vbf_ref[pl.ds(start, tq), lanes(p)])

    below_diag = lambda: _iota((rows, tq), 1) < (_iota((rows, tq), 0) % tq)
    prev_start = pl.multiple_of(jnp.maximum(i - 1, 0) * tq, tq)

    @pl.when(i == 0)
    def _():
        block_step(kv_block(0), tq, below_diag(), True)

    @pl.when(i > 0)
    def _():
        qs = queries()
        in_cur = below_diag()
        cur_start = pl.multiple_of(i * tq, tq)
        zs, spb = [], []
        for p in range(hp):
            z_cur = _dot_nt(qs[p], kbf_ref[pl.ds(cur_start, tq), lanes(p)])
            z_prev = _dot_nt(qs[p], kbf_ref[pl.ds(prev_start, tq), lanes(p)])
            z2 = jnp.where(in_cur, z_cur, z_prev)
            sp = _softplus2(z2)
            zs.append(z2 - sp)
            spb.append(sp.astype(BF16))
        zero = jnp.zeros((rows, tq), BF16)
        prev_part = lambda x: jnp.where(in_cur, zero, x)
        cur_part = lambda x: jnp.where(in_cur, x, zero)
        suf = _dot(jnp.concatenate(spb, axis=0), _suffix_matrix(tq))
        prev_tot = _dot(jnp.concatenate([prev_part(s) for s in spb], axis=0), jnp.ones((tq, LANES), BF16))
        carries = []
        for p in range(hp):
            sf = suf[p * rows:(p + 1) * rows]
            pt = prev_tot[p * rows:(p + 1) * rows]
            total = sf[:, 0:1] + spb[p][:, 0:1].astype(F32)
            widen = lambda x: jnp.concatenate([x] * (tq // LANES), axis=1)
            later = sf + jnp.where(in_cur, widen(-pt), widen(total - pt))
            a = jnp.exp2(zs[p] - later).astype(BF16)
            v2 = jnp.concatenate([vbf_ref[pl.ds(prev_start, tq), lanes(p)],
                                  vbf_ref[pl.ds(cur_start, tq), lanes(p)]], axis=0)
            acc_ref[p] = _dot(jnp.concatenate([prev_part(a), cur_part(a)], axis=1), v2)
            car_ref[p] = total
            carries.append(total)
        note_carries(carries)

    @pl.when(more_ref[0] > 0)
    def _():
        @pl.when(i > 0)
        def _():
            block_step(kv_block(prev_start), tq, below_diag(), False)

        def body(t):
            block_step(kv_block(pl.multiple_of((i - 2 - t) * tq, tq)), tq, None, False)
            return t + 1

        lax.while_loop(lambda t: (t < i - 1) & (more_ref[0] > 0), body, jnp.int32(0))

        @pl.when(more_ref[0] > 0)
        def _():
            block_step(lambda p: (kmp_ref[:, lanes(p)], vmp_ref[:, lanes(p)]), LANES,
                       _iota((rows, LANES), 1) < N_META, False)

    for p in range(hp):
        o_ref[:, lanes(p)] = _merge_halves(acc_ref[p, 0:tq], acc_ref[p, tq:rows])


def _sb(proj, proj_b, proj_meta_b, batch, tq, hp):
    m = proj.shape[0]
    seq = m // batch
    nq = seq // tq
    w = hp * LANES
    steps = SB_HEADS * HEAD_DIM // w
    q_start = (SWA_HEADS + 2 * SWA_KV_HEADS) * HEAD_DIM
    qw = min(w, 4 * LANES)
    n_q = w // qw
    assert q_start % qw == 0
    q_spec = lambda t: pl.BlockSpec((tq, qw), lambda b, p, i: (b * nq + i, q_start // qw + p * n_q + t))
    return pl.pallas_call(
        functools.partial(_sb_kernel, tq=tq, hp=hp, n_q=n_q),
        grid=(batch, steps, nq),
        in_specs=[q_spec(t) for t in range(n_q)] + [
            pl.BlockSpec((seq, w), lambda b, p, i: (b, steps + p)),
            pl.BlockSpec((seq, w), lambda b, p, i: (b, 2 * steps + p)),
            pl.BlockSpec((N_META, w), lambda b, p, i: (0, steps + p)),
            pl.BlockSpec((N_META, w), lambda b, p, i: (0, 2 * steps + p)),
        ],
        out_specs=pl.BlockSpec((tq, w), lambda b, p, i: (b * nq + i, p)),
        out_shape=jax.ShapeDtypeStruct((m, SB_HEADS * HEAD_DIM), F32),
        scratch_shapes=[pltpu.VMEM((LANES, w), BF16), pltpu.VMEM((LANES, w), BF16),
                        pltpu.VMEM((hp, 2 * tq, LANES), F32), pltpu.VMEM((hp, 2 * tq, 1), F32),
                        pltpu.SMEM((1,), jnp.int32)],
        compiler_params=_cparams(("arbitrary", "arbitrary", "arbitrary")),
        name="stickbreak",
    )(*([proj] * n_q), proj_b, proj_b, proj_meta_b, proj_meta_b)


def _meta_attn_kernel(sink_ref, p_ref, qg_ref, kg_ref, oa_ref, ob_ref):
    qw = SWA_HEADS * HEAD_DIM
    kvw = SWA_KV_HEADS * HEAD_DIM
    group = SWA_HEADS // SWA_KV_HEADS
    x = p_ref[...]
    xk = _pad_rows(x, LANES)
    qi = _iota((N_META, LANES), 0)
    kj = _iota((N_META, LANES), 1)

    qn = [(t * SCALE).astype(BF16) for t in _head_rmsnorm(x[:, 0:qw], qg_ref[...])]
    kn = jnp.concatenate(_head_rmsnorm(xk[:, qw:qw + kvw], kg_ref[...]), axis=1).astype(BF16)
    vb = xk[:, qw + kvw:qw + 2 * kvw].astype(BF16)
    delta_f = (qi - kj).astype(F32)
    causal = kj <= qi
    for pair in range(SWA_HEADS // 2):
        kv = (2 * pair) // group
        kdup, vdup = _dup_kv(kn, kv), _dup_kv(vb, kv)
        outs = []
        for e, qe in enumerate(_lane_halves(qn[pair])):
            h = 2 * pair + e
            outs.append(_swa_head(qe, kdup, vdup, _alibi_slope(h) * delta_f, causal, sink_ref[h]))
        oa_ref[:, pair * LANES:(pair + 1) * LANES] = _merge_halves(outs[0], outs[1])

    base = qw + 2 * kvw
    sbw = SB_HEADS * HEAD_DIM
    strict = kj < qi
    suffix = _suffix_matrix(LANES)
    zero_c = jnp.zeros((N_META, 1), F32)
    for pair in range(SB_HEADS // 2):
        sl = lambda src, part: src[:, base + part * sbw + pair * LANES: base + part * sbw + (pair + 1) * LANES]
        kb, vb2 = sl(xk, 1).astype(BF16), sl(xk, 2).astype(BF16)
        outs = [_sb_block(qe, kb, vb2, suffix, zero_c, strict)[0]
                for qe in _lane_halves((sl(x, 0) * SB_QSCALE).astype(BF16))]
        ob_ref[:, pair * LANES:(pair + 1) * LANES] = _merge_halves(outs[0], outs[1])


def _meta_attn(proj_meta, sinks, qg, kg, batch):
    pw = proj_meta.shape[1]
    qw = SWA_HEADS * HEAD_DIM
    kvw = SWA_KV_HEADS * HEAD_DIM
    sbw = SB_HEADS * HEAD_DIM
    return pl.pallas_call(
        _meta_attn_kernel,
        grid=(batch,),
        in_specs=[
            pl.BlockSpec(memory_space=pltpu.SMEM),
            pl.BlockSpec((N_META, pw), lambda b: (b, 0)),
            pl.BlockSpec((1, qw), lambda b: (0, 0)),
            pl.BlockSpec((1, kvw), lambda b: (0, 0)),
        ],
        out_specs=[pl.BlockSpec((N_META, qw), lambda b: (b, 0)),
                   pl.BlockSpec((N_META, sbw), lambda b: (b, 0))],
        out_shape=[jax.ShapeDtypeStruct((batch * N_META, qw), F32),
                   jax.ShapeDtypeStruct((batch * N_META, sbw), F32)],
        compiler_params=_cparams(("arbitrary",)),
        name="meta_attn",
    )(sinks, proj_meta, qg, kg)


def kernel(x, meta_tokens, attn_norm_g, w_in, q_norm_g, k_norm_g, attn_sinks,
           swa_out_g, sb_out_g, w_o, ffn_norm_g, w_gate, w_up, w_down):
    batch, seq, d = x.shape
    depth = w_in.shape[0]
    assert (batch * seq) % 1024 == 0 and seq % 256 == 0 and meta_tokens.shape[0] == N_META

    tm = 512
    tmeta = N_META
    tm_in = 1024
    tn_in = 1536
    assert tn_in == (SWA_HEADS + 2 * SWA_KV_HEADS) * HEAD_DIM
    tn_out = 2048
    tm_ffn = 1024
    tf = 512
    swa_sub = 2
    tq = 256
    hp = 8

    h = x.reshape(batch * seq, d)
    hm = meta_tokens.astype(x.dtype)

    w_in_b, w_o_b = w_in.astype(BF16), w_o.astype(BF16)
    w_gate_b, w_up_b, w_down_b = w_gate.astype(BF16), w_up.astype(BF16), w_down.astype(BF16)

    swa_tables = _swa_tables()
    for l in range(depth):
        g_attn = attn_norm_g[l].reshape(1, d)
        qg = jnp.tile(q_norm_g[l], SWA_HEADS).reshape(1, -1)
        kg = jnp.tile(k_norm_g[l], SWA_KV_HEADS).reshape(1, -1)
        sinks = attn_sinks[l].astype(F32)
        ga, gb = swa_out_g[l].reshape(1, -1), sb_out_g[l].reshape(1, -1)
        g_ffn = ffn_norm_g[l].reshape(1, d)

        proj, proj_b = _norm_matmul(h, g_attn, w_in_b, l, tm_in, tn_in)
        proj_m, proj_mb = _norm_matmul(hm, g_attn, w_in_b, l, tmeta, tn_in)

        out_a = _swa(proj, proj_m, sinks, qg * (SCALE * LOG2E), kg, swa_tables, batch, swa_sub)
        out_b = _sb(proj, proj_b, proj_mb, batch, tq, hp)
        out_am, out_bm = _meta_attn(proj_m, sinks, qg, kg, 1)

        h = _outproj(out_a, out_b, ga, gb, w_o_b, l, h, tm, tn_out)
        hm = _outproj(out_am, out_bm, ga, gb, w_o_b, l, hm, tmeta, tn_out)

        h = _ffn(h, g_ffn, w_gate_b, w_up_b, w_down_b, l, tm_ffn, tf)
        hm = _ffn(hm, g_ffn, w_gate_b, w_up_b, w_down_b, l, tmeta, tf)

    return h.reshape(batch, seq, d)
```

```python
import functools

import jax
import jax.numpy as jnp
from jax import lax
from jax.experimental import pallas as pl
from jax.experimental.pallas import tpu as pltpu

F32 = jnp.float32
BF16 = jnp.bfloat16

N_META = 16
BLOCK = 128
HEAD_DIM = 64
LANES = 128
SWA_HEADS = 16
SWA_KV_HEADS = 4
SB_HEADS = 16
EPS = 1e-6
NEG = -1e30
SCALE = HEAD_DIM ** -0.5
LOG2E = 1.4426950408889634
SB_QSCALE = SCALE * LOG2E
F32_UNDERFLOW_LOG2 = 160.0
VMEM_LIMIT = 56 * 1024 * 1024


def _cparams(sem):
    return pltpu.CompilerParams(dimension_semantics=sem, vmem_limit_bytes=VMEM_LIMIT)


def _dot(a, b):
    return jnp.dot(a, b, preferred_element_type=F32)


def _dot_nt(a, b):
    return lax.dot_general(a, b, (((1,), (1,)), ((), ())), preferred_element_type=F32)


def _iota(shape, dim):
    return lax.broadcasted_iota(jnp.int32, shape, dim)


def _norm_mm_kernel(x_ref, g_ref, w_ref, o_ref, ob_ref, xn_ref):
    @pl.when(pl.program_id(1) == 0)
    def _():
        x = x_ref[...]
        ms = jnp.mean(x * x, axis=-1, keepdims=True)
        xn_ref[...] = ((x * lax.rsqrt(ms + EPS)) * g_ref[...]).astype(BF16)

    y = _dot(xn_ref[...], w_ref[...])
    o_ref[...] = y
    ob_ref[...] = y.astype(BF16)


def _norm_matmul(x, g, w, layer, tm, tn):
    m, d = x.shape
    n = w.shape[2]
    return pl.pallas_call(
        _norm_mm_kernel,
        grid=(m // tm, n // tn),
        in_specs=[
            pl.BlockSpec((tm, d), lambda i, j: (i, 0)),
            pl.BlockSpec((1, d), lambda i, j: (0, 0)),
            pl.BlockSpec((None, d, tn), lambda i, j: (layer, 0, j)),
        ],
        out_specs=[pl.BlockSpec((tm, tn), lambda i, j: (i, j)),
                   pl.BlockSpec((tm, tn), lambda i, j: (i, jnp.maximum(j - 1, 0)))],
        out_shape=[jax.ShapeDtypeStruct((m, n), F32), jax.ShapeDtypeStruct((m, n - tn), BF16)],
        scratch_shapes=[pltpu.VMEM((tm, d), BF16)],
        compiler_params=_cparams(("arbitrary", "arbitrary")),
        name="norm_inproj",
    )(x, g, w)


def _outproj_kernel(a_ref, b_ref, ga_ref, gb_ref, wa_ref, wb_ref, h_ref, o_ref, an_ref, bn_ref):
    @pl.when(pl.program_id(1) == 0)
    def _():
        for src, g, dst in ((a_ref, ga_ref, an_ref), (b_ref, gb_ref, bn_ref)):
            x = src[...]
            ms = jnp.mean(x * x, axis=-1, keepdims=True)
            dst[...] = ((x * lax.rsqrt(ms + EPS)) * g[...]).astype(BF16)

    o_ref[...] = h_ref[...] + (_dot(an_ref[...], wa_ref[...]) + _dot(bn_ref[...], wb_ref[...]))


def _outproj(a, b, ga, gb, w, layer, h, tm, tn):
    m, wdt = a.shape
    d = h.shape[1]
    return pl.pallas_call(
        _outproj_kernel,
        grid=(m // tm, d // tn),
        in_specs=[
            pl.BlockSpec((tm, wdt), lambda i, j: (i, 0)),
            pl.BlockSpec((tm, wdt), lambda i, j: (i, 0)),
            pl.BlockSpec((1, wdt), lambda i, j: (0, 0)),
            pl.BlockSpec((1, wdt), lambda i, j: (0, 0)),
            pl.BlockSpec((None, wdt, tn), lambda i, j: (layer, 0, j)),
            pl.BlockSpec((None, wdt, tn), lambda i, j: (layer, 1, j)),
            pl.BlockSpec((tm, tn), lambda i, j: (i, j)),
        ],
        out_specs=pl.BlockSpec((tm, tn), lambda i, j: (i, j)),
        out_shape=jax.ShapeDtypeStruct((m, d), F32),
        scratch_shapes=[pltpu.VMEM((tm, wdt), BF16), pltpu.VMEM((tm, wdt), BF16)],
        compiler_params=_cparams(("arbitrary", "arbitrary")),
        name="outproj",
    )(a, b, ga, gb, w, w, h)


def _ffn_kernel(x_ref, g_ref, wg_ref, wu_ref, wd_ref, o_ref, xn_ref):
    @pl.when(pl.program_id(1) == 0)
    def _():
        x = x_ref[...]
        ms = jnp.mean(x * x, axis=-1, keepdims=True)
        xn_ref[...] = ((x * lax.rsqrt(ms + EPS)) * g_ref[...]).astype(BF16)
        o_ref[...] = x

    xn = xn_ref[...]
    gate = _dot(xn, wg_ref[...])
    up = _dot(xn, wu_ref[...])
    act = (gate / (1.0 + jnp.exp(-gate))) * up
    o_ref[...] += _dot(act.astype(BF16), wd_ref[...])


def _ffn(x, g, wg, wu, wd, layer, tm, tf):
    m, d = x.shape
    f = wg.shape[2]
    return pl.pallas_call(
        _ffn_kernel,
        grid=(m // tm, f // tf),
        in_specs=[
            pl.BlockSpec((tm, d), lambda i, j: (i, 0)),
            pl.BlockSpec((1, d), lambda i, j: (0, 0)),
            pl.BlockSpec((None, d, tf), lambda i, j: (layer, 0, j)),
            pl.BlockSpec((None, d, tf), lambda i, j: (layer, 0, j)),
            pl.BlockSpec((None, tf, d), lambda i, j: (layer, j, 0)),
        ],
        out_specs=pl.BlockSpec((tm, d), lambda i, j: (i, 0)),
        out_shape=jax.ShapeDtypeStruct((m, d), F32),
        scratch_shapes=[pltpu.VMEM((tm, d), BF16)],
        compiler_params=_cparams(("arbitrary", "arbitrary")),
        name="ffn",
    )(x, g, wg, wu, wd)


def _head_rmsnorm(x, g):
    r, c = x.shape
    same_head = (_iota((LANES, LANES), 0) // HEAD_DIM) == (_iota((LANES, LANES), 1) // HEAD_DIM)
    bd = jnp.where(same_head, 1.0 / HEAD_DIM, 0.0).astype(BF16)
    outs = []
    for t in range(c // LANES):
        xb = x[:, t * LANES:(t + 1) * LANES]
        ms = _dot((xb * xb).astype(BF16), bd)
        outs.append((xb * lax.rsqrt(ms + EPS)) * g[:, t * LANES:(t + 1) * LANES])
    return outs


def _half_select(half):
    src = _iota((LANES, LANES), 0)
    dst = _iota((LANES, LANES), 1)
    return jnp.where(src == (dst % HEAD_DIM) + HEAD_DIM * half, 1.0, 0.0).astype(BF16)


def _lane_halves(x):
    lo_half = _iota(x.shape, 1) < HEAD_DIM
    zero = jnp.zeros_like(x)
    return jnp.where(lo_half, x, zero), jnp.where(lo_half, zero, x)


def _merge_halves(lo, hi):
    return jnp.where(_iota(lo.shape, 1) < HEAD_DIM, lo, hi)


def _alibi_slope(h):
    return 2.0 ** (-8.0 * (h + 1) / SWA_HEADS)


def _softplus2(z2):
    neg_abs = lax.bitcast_convert_type(
        lax.bitcast_convert_type(z2, jnp.uint32) | jnp.uint32(0x80000000), F32)
    return jnp.maximum(z2, 0.0) + jnp.log2(1.0 + jnp.exp2(neg_abs))


def _suffix_matrix(n):
    return jnp.where(_iota((n, n), 0) > _iota((n, n), 1), 1.0, 0.0).astype(BF16)


def _sb_block(qe, kblk, vblk, suffix, carry, mask):
    z2 = _dot_nt(qe, kblk)
    sp = _softplus2(z2)
    if mask is not None:
        sp = jnp.where(mask, sp, 0.0)
    later = _dot(sp.astype(BF16), suffix)
    a = jnp.exp2((z2 - sp) - (later + carry))
    if mask is not None:
        a = jnp.where(mask, a, 0.0)
    contrib = _dot(a.astype(BF16), vblk)
    return contrib, carry + jnp.sum(sp, axis=-1, keepdims=True)


def _swa_head(qe, kdup, vdup, bias, mask, sink):
    s = jnp.where(mask, _dot_nt(qe, kdup) - bias, NEG)
    mx = jnp.maximum(jnp.max(s, axis=-1, keepdims=True), sink)
    e = jnp.exp(s - mx)
    den = jnp.sum(e, axis=-1, keepdims=True) + jnp.exp(sink - mx)
    return _dot(e.astype(BF16), vdup) / den


def _pad_rows(x, rows):
    return jnp.concatenate([x, jnp.zeros((rows - x.shape[0], x.shape[1]), x.dtype)], axis=0)


def _dup_kv(x_bf, kv_head):
    t = kv_head // 2
    return _dot(x_bf[:, t * LANES:(t + 1) * LANES], _half_select(kv_head % 2)).astype(BF16)


def _swa_tables():
    group = SWA_HEADS // SWA_KV_HEADS
    slope = jnp.asarray([_alibi_slope(h) * LOG2E for h in range(SWA_HEADS)], F32)
    slope = jnp.repeat(slope.reshape(SWA_KV_HEADS, group, 1, 1), BLOCK, axis=2)
    qi = jnp.arange(BLOCK, dtype=F32)[:, None]
    j = jnp.arange(BLOCK, dtype=F32)[None, :]
    dist_band = jnp.where(j <= qi, qi - j, qi - j + BLOCK)
    dist_meta = jnp.where(j < N_META, N_META + qi - j, 0.0)
    shape = (SWA_KV_HEADS, group * BLOCK, BLOCK)
    return ((slope * dist_band).reshape(shape), (slope * dist_meta).reshape(shape),
            jnp.broadcast_to(slope * BLOCK, slope.shape[:3] + (BLOCK,)).reshape(shape))


def _swa_kernel(q_ref, kc_ref, vc_ref, km_ref, vm_ref, qg_ref, kg_ref,
                band_ref, meta_ref, step_ref, sink_ref, o_ref, kd_ref, vd_ref, *, nsub):
    n = pl.program_id(1)
    group = SWA_HEADS // SWA_KV_HEADS
    rows = group * BLOCK

    def store_rows(k_rows, v_rows, start):
        kn = jnp.concatenate(_head_rmsnorm(k_rows, kg_ref[...]), axis=1).astype(BF16)
        vb = v_rows.astype(BF16)
        for g in range(SWA_KV_HEADS):
            kd_ref[g, start:start + BLOCK] = _dup_kv(kn, g)
            vd_ref[g, start:start + BLOCK] = _dup_kv(vb, g)

    @pl.when(n == 0)
    def _():
        store_rows(_pad_rows(km_ref[...], BLOCK), _pad_rows(vm_ref[...], BLOCK), 0)
        kd_ref[:, 2 * BLOCK:3 * BLOCK] = jnp.zeros((SWA_KV_HEADS, BLOCK, LANES), BF16)
        vd_ref[:, 2 * BLOCK:3 * BLOCK] = jnp.zeros((SWA_KV_HEADS, BLOCK, LANES), BF16)

    qi = _iota((rows, BLOCK), 0) % BLOCK
    j = _iota((rows, BLOCK), 1)
    in_cur = j <= qi
    meta_ok = j < N_META

    def one_block(sub):
        blk = nsub * n + sub
        rs = slice(sub * BLOCK, (sub + 1) * BLOCK)
        kd_ref[:, BLOCK:2 * BLOCK] = kd_ref[:, 2 * BLOCK:3 * BLOCK]
        vd_ref[:, BLOCK:2 * BLOCK] = vd_ref[:, 2 * BLOCK:3 * BLOCK]
        store_rows(kc_ref[rs, :], vc_ref[rs, :], 2 * BLOCK)
        qn = [x.astype(BF16) for x in _head_rmsnorm(q_ref[rs, :], qg_ref[...])]
        band_ok = in_cur | (blk > 0)
        blocks_before = blk.astype(F32)
        logits, weights, dens = {}, {}, {}

        def score(g):
            logits[g] = _dot_nt(
                jnp.concatenate(_lane_halves(qn[2 * g]) + _lane_halves(qn[2 * g + 1]), axis=0), kd_ref[g])

        def softmax(g):
            sink = sink_ref[g]
            s = logits.pop(g)
            s_band = jnp.where(in_cur, s[:, 2 * BLOCK:], s[:, BLOCK:2 * BLOCK]) - band_ref[g]
            s_band = jnp.where(band_ok, s_band, NEG)
            s_meta = jnp.where(meta_ok, (s[:, :BLOCK] - meta_ref[g]) - step_ref[g] * blocks_before, NEG)
            mx = jnp.maximum(jnp.max(jnp.maximum(s_band, s_meta), axis=-1, keepdims=True), sink)
            e_band = jnp.exp2(s_band - mx)
            e_meta = jnp.exp2(s_meta - mx)
            dens[g] = jnp.sum(e_band + e_meta, axis=-1, keepdims=True) + jnp.exp2(sink - mx)
            weights[g] = jnp.concatenate(
                [e_meta, jnp.where(in_cur, 0.0, e_band), jnp.where(in_cur, e_band, 0.0)], axis=1).astype(BF16)

        def attend(g):
            o = _dot(weights.pop(g), vd_ref[g]) / dens.pop(g)
            for t in range(2):
                o_ref[rs, (2 * g + t) * LANES:(2 * g + t + 1) * LANES] = _merge_halves(
                    o[2 * t * BLOCK:(2 * t + 1) * BLOCK], o[(2 * t + 1) * BLOCK:(2 * t + 2) * BLOCK])

        for stage, g in ((score, 0), (score, 1), (softmax, 0), (score, 2), (softmax, 1), (attend, 0),
                         (score, 3), (softmax, 2), (attend, 1), (softmax, 3), (attend, 2), (attend, 3)):
            stage(g)

    for sub in range(nsub):
        one_block(sub)


def _swa(proj, proj_meta, sinks, qg, kg, tables, batch, nsub):
    m = proj.shape[0]
    tb = nsub * BLOCK
    nb = m // batch // tb
    qw = SWA_HEADS * HEAD_DIM
    kvw = SWA_KV_HEADS * HEAD_DIM
    group = SWA_HEADS // SWA_KV_HEADS
    kcol, vcol = qw // kvw, qw // kvw + 1
    row = lambda b, n: b * nb + n
    whole = lambda shape: pl.BlockSpec(shape, lambda b, n: (0,) * len(shape))
    sink_col = jnp.broadcast_to((sinks * LOG2E).reshape(SWA_KV_HEADS, group, 1, 1),
                                (SWA_KV_HEADS, group, BLOCK, LANES))
    sink_col = sink_col.reshape(SWA_KV_HEADS, group * BLOCK, LANES)
    band, meta, step = tables
    return pl.pallas_call(
        functools.partial(_swa_kernel, nsub=nsub),
        grid=(batch, nb),
        in_specs=[
            pl.BlockSpec((tb, qw), lambda b, n: (row(b, n), 0)),
            pl.BlockSpec((tb, kvw), lambda b, n: (row(b, n), kcol)),
            pl.BlockSpec((tb, kvw), lambda b, n: (row(b, n), vcol)),
            pl.BlockSpec((N_META, kvw), lambda b, n: (0, kcol)),
            pl.BlockSpec((N_META, kvw), lambda b, n: (0, vcol)),
            pl.BlockSpec((1, qw), lambda b, n: (0, 0)),
            pl.BlockSpec((1, kvw), lambda b, n: (0, 0)),
            whole(band.shape), whole(meta.shape), whole(step.shape), whole(sink_col.shape),
        ],
        out_specs=pl.BlockSpec((tb, qw), lambda b, n: (row(b, n), 0)),
        out_shape=jax.ShapeDtypeStruct((m, qw), F32),
        scratch_shapes=[pltpu.VMEM((SWA_KV_HEADS, 3 * BLOCK, LANES), BF16),
                        pltpu.VMEM((SWA_KV_HEADS, 3 * BLOCK, LANES), BF16)],
        compiler_params=_cparams(("arbitrary", "arbitrary")),
        name="swa",
    )(proj, proj, proj, proj_meta, proj_meta, qg, kg, band, meta, step, sink_col)


def _sb_kernel(*refs, tq, hp, n_q):
    q_refs = refs[:n_q]
    kbf_ref, vbf_ref, km_ref, vm_ref, o_ref, kmp_ref, vmp_ref, acc_ref, car_ref, more_ref = refs[n_q:]
    per_q = hp // n_q
    i = pl.program_id(2)

    @pl.when(i == 0)
    def _():
        kmp_ref[...] = jnp.zeros_like(kmp_ref)
        vmp_ref[...] = jnp.zeros_like(vmp_ref)
        kmp_ref[0:N_META, :] = km_ref[...].astype(BF16)
        vmp_ref[0:N_META, :] = vm_ref[...].astype(BF16)

    lanes = lambda p: slice(p * LANES, (p + 1) * LANES)
    rows = 2 * tq

    def queries():
        return [jnp.concatenate(_lane_halves(
                    (q_refs[p // per_q][:, lanes(p % per_q)] * SB_QSCALE).astype(BF16)), axis=0)
                for p in range(hp)]

    def note_carries(carries):
        m = functools.reduce(jnp.minimum, carries)
        while m.shape[0] > 8:
            half = m.shape[0] // 2
            m = jnp.minimum(m[:half], m[half:])
        more_ref[0] = (jnp.min(m) <= F32_UNDERFLOW_LOG2).astype(jnp.int32)

    def block_step(kv_of_pair, width, mask, first):
        qs = queries()
        kv = [kv_of_pair(p) for p in range(hp)]
        z2 = [_dot_nt(qs[p], kv[p][0]) for p in range(hp)]
        sp = [_softplus2(z) for z in z2]
        if mask is not None:
            sp = [jnp.where(mask, s, 0.0) for s in sp]
        zs = [z2[p] - sp[p] for p in range(hp)]
        later = _dot(jnp.concatenate([s.astype(BF16) for s in sp], axis=0), _suffix_matrix(width))
        carries = []
        for p in range(hp):
            carry = 0.0 if first else car_ref[p]
            a = jnp.exp2(zs[p] - (later[p * rows:(p + 1) * rows] + carry))
            if mask is not None:
                a = jnp.where(mask, a, 0.0)
            d = _dot(a.astype(BF16), kv[p][1])
            acc_ref[p] = d if first else acc_ref[p] + d
            carries.append(carry + jnp.sum(sp[p], axis=-1, keepdims=True))
            car_ref[p] = carries[-1]
        note_carries(carries)

    def kv_block(start):
        return lambda p: (kbf_ref[pl.ds(start, tq), lanes(p)], vbf_ref[pl.ds(start, tq), lanes(p)])

    below_diag = lambda: _iota((rows, tq), 1) < (_iota((rows, tq), 0) % tq)
    prev_start = pl.multiple_of(jnp.maximum(i - 1, 0) * tq, tq)

    @pl.when(i == 0)
    def _():
        block_step(kv_block(0), tq, below_diag(), True)

    @pl.when(i > 0)
    def _():
        qs = queries()
        in_cur = below_diag()
        cur_start = pl.multiple_of(i * tq, tq)
        zs, spb = [], []
        for p in range(hp):
            z_cur = _dot_nt(qs[p], kbf_ref[pl.ds(cur_start, tq), lanes(p)])
            z_prev = _dot_nt(qs[p], kbf_ref[pl.ds(prev_start, tq), lanes(p)])
            z2 = jnp.where(in_cur, z_cur, z_prev)
            sp = _softplus2(z2)
            zs.append(z2 - sp)
            spb.append(sp.astype(BF16))
        zero = jnp.zeros((rows, tq), BF16)
        prev_part = lambda x: jnp.where(in_cur, zero, x)
        cur_part = lambda x: jnp.where(in_cur, x, zero)
        suf = _dot(jnp.concatenate(spb, axis=0), _suffix_matrix(tq))
        prev_tot = _dot(jnp.concatenate([prev_part(s) for s in spb], axis=0), jnp.ones((tq, LANES), BF16))
        carries = []
        for p in range(hp):
            sf = suf[p * rows:(p + 1) * rows]
            pt = prev_tot[p * rows:(p + 1) * rows]
            total = sf[:, 0:1] + spb[p][:, 0:1].astype(F32)
            widen = lambda x: jnp.concatenate([x] * (tq // LANES), axis=1)
            later = sf + jnp.where(in_cur, widen(-pt), widen(total - pt))
            a = jnp.exp2(zs[p] - later).astype(BF16)
            v2 = jnp.concatenate([vbf_ref[pl.ds(prev_start, tq), lanes(p)],
                                  vbf_ref[pl.ds(cur_start, tq), lanes(p)]], axis=0)
            acc_ref[p] = _dot(jnp.concatenate([prev_part(a), cur_part(a)], axis=1), v2)
            car_ref[p] = total
            carries.append(total)
        note_carries(carries)

    @pl.when(more_ref[0] > 0)
    def _():
        @pl.when(i > 0)
        def _():
            block_step(kv_block(prev_start), tq, below_diag(), False)

        def body(t):
            block_step(kv_block(pl.multiple_of((i - 2 - t) * tq, tq)), tq, None, False)
            return t + 1

        lax.while_loop(lambda t: (t < i - 1) & (more_ref[0] > 0), body, jnp.int32(0))

        @pl.when(more_ref[0] > 0)
        def _():
            block_step(lambda p: (kmp_ref[:, lanes(p)], vmp_ref[:, lanes(p)]), LANES,
                       _iota((rows, LANES), 1) < N_META, False)

    for p in range(hp):
        o_ref[:, lanes(p)] = _merge_halves(acc_ref[p, 0:tq], acc_ref[p, tq:rows])


def _sb(proj, proj_b, proj_meta_b, batch, tq, hp):
    m = proj.shape[0]
    seq = m // batch
    nq = seq // tq
    w = hp * LANES
    steps = SB_HEADS * HEAD_DIM // w
    q_start = (SWA_HEADS + 2 * SWA_KV_HEADS) * HEAD_DIM
    qw = min(w, 4 * LANES)
    n_q = w // qw
    assert q_start % qw == 0
    q_spec = lambda t: pl.BlockSpec((tq, qw), lambda b, p, i: (b * nq + i, q_start // qw + p * n_q + t))
    return pl.pallas_call(
        functools.partial(_sb_kernel, tq=tq, hp=hp, n_q=n_q),
        grid=(batch, steps, nq),
        in_specs=[q_spec(t) for t in range(n_q)] + [
            pl.BlockSpec((seq, w), lambda b, p, i: (b, steps + p)),
            pl.BlockSpec((seq, w), lambda b, p, i: (b, 2 * steps + p)),
            pl.BlockSpec((N_META, w), lambda b, p, i: (0, steps + p)),
            pl.BlockSpec((N_META, w), lambda b, p, i: (0, 2 * steps + p)),
        ],
        out_specs=pl.BlockSpec((tq, w), lambda b, p, i: (b * nq + i, p)),
        out_shape=jax.ShapeDtypeStruct((m, SB_HEADS * HEAD_DIM), F32),
        scratch_shapes=[pltpu.VMEM((LANES, w), BF16), pltpu.VMEM((LANES, w), BF16),
                        pltpu.VMEM((hp, 2 * tq, LANES), F32), pltpu.VMEM((hp, 2 * tq, 1), F32),
                        pltpu.SMEM((1,), jnp.int32)],
        compiler_params=_cparams(("arbitrary", "arbitrary", "arbitrary")),
        name="stickbreak",
    )(*([proj] * n_q), proj_b, proj_b, proj_meta_b, proj_meta_b)


def _meta_attn_kernel(sink_ref, p_ref, qg_ref, kg_ref, oa_ref, ob_ref):
    qw = SWA_HEADS * HEAD_DIM
    kvw = SWA_KV_HEADS * HEAD_DIM
    group = SWA_HEADS // SWA_KV_HEADS
    x = p_ref[...]
    xk = _pad_rows(x, LANES)
    qi = _iota((N_META, LANES), 0)
    kj = _iota((N_META, LANES), 1)

    qn = [(t * SCALE).astype(BF16) for t in _head_rmsnorm(x[:, 0:qw], qg_ref[...])]
    kn = jnp.concatenate(_head_rmsnorm(xk[:, qw:qw + kvw], kg_ref[...]), axis=1).astype(BF16)
    vb = xk[:, qw + kvw:qw + 2 * kvw].astype(BF16)
    delta_f = (qi - kj).astype(F32)
    causal = kj <= qi
    for pair in range(SWA_HEADS // 2):
        kv = (2 * pair) // group
        kdup, vdup = _dup_kv(kn, kv), _dup_kv(vb, kv)
        outs = []
        for e, qe in enumerate(_lane_halves(qn[pair])):
            h = 2 * pair + e
            outs.append(_swa_head(qe, kdup, vdup, _alibi_slope(h) * delta_f, causal, sink_ref[h]))
        oa_ref[:, pair * LANES:(pair + 1) * LANES] = _merge_halves(outs[0], outs[1])

    base = qw + 2 * kvw
    sbw = SB_HEADS * HEAD_DIM
    strict = kj < qi
    suffix = _suffix_matrix(LANES)
    zero_c = jnp.zeros((N_META, 1), F32)
    for pair in range(SB_HEADS // 2):
        sl = lambda src, part: src[:, base + part * sbw + pair * LANES: base + part * sbw + (pair + 1) * LANES]
        kb, vb2 = sl(xk, 1).astype(BF16), sl(xk, 2).astype(BF16)
        outs = [_sb_block(qe, kb, vb2, suffix, zero_c, strict)[0]
                for qe in _lane_halves((sl(x, 0) * SB_QSCALE).astype(BF16))]
        ob_ref[:, pair * LANES:(pair + 1) * LANES] = _merge_halves(outs[0], outs[1])


def _meta_attn(proj_meta, sinks, qg, kg, batch):
    pw = proj_meta.shape[1]
    qw = SWA_HEADS * HEAD_DIM
    kvw = SWA_KV_HEADS * HEAD_DIM
    sbw = SB_HEADS * HEAD_DIM
    return pl.pallas_call(
        _meta_attn_kernel,
        grid=(batch,),
        in_specs=[
            pl.BlockSpec(memory_space=pltpu.SMEM),
            pl.BlockSpec((N_META, pw), lambda b: (b, 0)),
            pl.BlockSpec((1, qw), lambda b: (0, 0)),
            pl.BlockSpec((1, kvw), lambda b: (0, 0)),
        ],
        out_specs=[pl.BlockSpec((N_META, qw), lambda b: (b, 0)),
                   pl.BlockSpec((N_META, sbw), lambda b: (b, 0))],
        out_shape=[jax.ShapeDtypeStruct((batch * N_META, qw), F32),
                   jax.ShapeDtypeStruct((batch * N_META, sbw), F32)],
        compiler_params=_cparams(("arbitrary",)),
        name="meta_attn",
    )(sinks, proj_meta, qg, kg)


def kernel(x, meta_tokens, attn_norm_g, w_in, q_norm_g, k_norm_g, attn_sinks,
           swa_out_g, sb_out_g, w_o, ffn_norm_g, w_gate, w_up, w_down):
    batch, seq, d = x.shape
    depth = w_in.shape[0]
    assert (batch * seq) % 1024 == 0 and seq % 256 == 0 and meta_tokens.shape[0] == N_META

    tm = 512
    tmeta = N_META
    tm_in = 1024
    tn_in = 1536
    assert tn_in == (SWA_HEADS + 2 * SWA_KV_HEADS) * HEAD_DIM
    tn_out = 2048
    tm_ffn = 1024
    tf = 512
    swa_sub = 4
    tq = 256
    hp = 8

    h = x.reshape(batch * seq, d)
    hm = meta_tokens.astype(x.dtype)

    w_in_b, w_o_b = w_in.astype(BF16), w_o.astype(BF16)
    w_gate_b, w_up_b, w_down_b = w_gate.astype(BF16), w_up.astype(BF16), w_down.astype(BF16)

    swa_tables = _swa_tables()
    for l in range(depth):
        g_attn = attn_norm_g[l].reshape(1, d)
        qg = jnp.tile(q_norm_g[l], SWA_HEADS).reshape(1, -1)
        kg = jnp.tile(k_norm_g[l], SWA_KV_HEADS).reshape(1, -1)
        sinks = attn_sinks[l].astype(F32)
        ga, gb = swa_out_g[l].reshape(1, -1), sb_out_g[l].reshape(1, -1)
        g_ffn = ffn_norm_g[l].reshape(1, d)

        proj, proj_b = _norm_matmul(h, g_attn, w_in_b, l, tm_in, tn_in)
        proj_m, proj_mb = _norm_matmul(hm, g_attn, w_in_b, l, tmeta, tn_in)

        out_a = _swa(proj, proj_m, sinks, qg * (SCALE * LOG2E), kg, swa_tables, batch, swa_sub)
        out_b = _sb(proj, proj_b, proj_mb, batch, tq, hp)
        out_am, out_bm = _meta_attn(proj_m, sinks, qg, kg, 1)

        h = _outproj(out_a, out_b, ga, gb, w_o_b, l, h, tm, tn_out)
        hm = _outproj(out_am, out_bm, ga, gb, w_o_b, l, hm, tmeta, tn_out)

        h = _ffn(h, g_ffn, w_gate_b, w_up_b, w_down_b, l, tm_ffn, tf)
        hm = _ffn(hm, g_ffn, w_gate_b, w_up_b, w_down_b, l, tmeta, tf)

    return h.reshape(batch, seq, d)
```

```python
import functools

import jax
import jax.numpy as jnp
from jax import lax
from jax.experimental import pallas as pl
from jax.experimental.pallas import tpu as pltpu

F32 = jnp.float32
BF16 = jnp.bfloat16

N_META = 16
BLOCK = 128
HEAD_DIM = 64
LANES = 128
SWA_HEADS = 16
SWA_KV_HEADS = 4
SB_HEADS = 16
EPS = 1e-6
NEG = -1e30
SCALE = HEAD_DIM ** -0.5
LOG2E = 1.4426950408889634
SB_QSCALE = SCALE * LOG2E
F32_UNDERFLOW_LOG2 = 160.0
VMEM_LIMIT = 56 * 1024 * 1024


def _cparams(sem):
    return pltpu.CompilerParams(dimension_semantics=sem, vmem_limit_bytes=VMEM_LIMIT)


def _dot(a, b):
    return jnp.dot(a, b, preferred_element_type=F32)


def _dot_nt(a, b):
    return lax.dot_general(a, b, (((1,), (1,)), ((), ())), preferred_element_type=F32)


def _iota(shape, dim):
    return lax.broadcasted_iota(jnp.int32, shape, dim)


def _norm_mm_kernel(x_ref, g_ref, w_ref, o_ref, ob_ref, xn_ref):
    def column_tile(xn):
        y = _dot(xn, w_ref[...])
        o_ref[...] = y
        ob_ref[...] = y.astype(BF16)

    @pl.when(pl.program_id(1) == 0)
    def _():
        x = x_ref[...]
        ms = jnp.mean(x * x, axis=-1, keepdims=True)
        xn = ((x * lax.rsqrt(ms + EPS)) * g_ref[...]).astype(BF16)
        xn_ref[...] = xn
        column_tile(xn)

    @pl.when(pl.program_id(1) > 0)
    def _():
        column_tile(xn_ref[...])


def _norm_matmul(x, g, w, layer, tm, tn):
    m, d = x.shape
    n = w.shape[2]
    return pl.pallas_call(
        _norm_mm_kernel,
        grid=(m // tm, n // tn),
        in_specs=[
            pl.BlockSpec((tm, d), lambda i, j: (i, 0)),
            pl.BlockSpec((1, d), lambda i, j: (0, 0)),
            pl.BlockSpec((None, d, tn), lambda i, j: (layer, 0, j)),
        ],
        out_specs=[pl.BlockSpec((tm, tn), lambda i, j: (i, j)),
                   pl.BlockSpec((tm, tn), lambda i, j: (i, jnp.maximum(j - 1, 0)))],
        out_shape=[jax.ShapeDtypeStruct((m, n), F32), jax.ShapeDtypeStruct((m, n - tn), BF16)],
        scratch_shapes=[pltpu.VMEM((tm, d), BF16)],
        compiler_params=_cparams(("arbitrary", "arbitrary")),
        name="norm_inproj",
    )(x, g, w)


def _outproj_kernel(a_ref, b_ref, ga_ref, gb_ref, wa_ref, wb_ref, h_ref, o_ref):
    def normed(src, g):
        x = src[...]
        ms = jnp.mean(x * x, axis=-1, keepdims=True)
        return ((x * lax.rsqrt(ms + EPS)) * g[...]).astype(BF16)

    o_ref[...] = h_ref[...] + (_dot(normed(a_ref, ga_ref), wa_ref[...])
                               + _dot(normed(b_ref, gb_ref), wb_ref[...]))


def _outproj(a, b, ga, gb, w, layer, h, tm, tn):
    m, wdt = a.shape
    d = h.shape[1]
    assert tn == d
    return pl.pallas_call(
        _outproj_kernel,
        grid=(m // tm, d // tn),
        in_specs=[
            pl.BlockSpec((tm, wdt), lambda i, j: (i, 0)),
            pl.BlockSpec((tm, wdt), lambda i, j: (i, 0)),
            pl.BlockSpec((1, wdt), lambda i, j: (0, 0)),
            pl.BlockSpec((1, wdt), lambda i, j: (0, 0)),
            pl.BlockSpec((None, wdt, tn), lambda i, j: (layer, 0, j)),
            pl.BlockSpec((None, wdt, tn), lambda i, j: (layer, 1, j)),
            pl.BlockSpec((tm, tn), lambda i, j: (i, j)),
        ],
        out_specs=pl.BlockSpec((tm, tn), lambda i, j: (i, j)),
        out_shape=jax.ShapeDtypeStruct((m, d), F32),
        compiler_params=_cparams(("arbitrary", "arbitrary")),
        name="outproj",
    )(a, b, ga, gb, w, w, h)


def _ffn_kernel(x_ref, g_ref, wg_ref, wu_ref, wd_ref, o_ref, xn_ref):
    def hidden_tile(xn):
        gate = _dot(xn, wg_ref[...])
        up = _dot(xn, wu_ref[...])
        act = (gate / (1.0 + jnp.exp(-gate))) * up
        return _dot(act.astype(BF16), wd_ref[...])

    @pl.when(pl.program_id(1) == 0)
    def _():
        x = x_ref[...]
        ms = jnp.mean(x * x, axis=-1, keepdims=True)
        xn = ((x * lax.rsqrt(ms + EPS)) * g_ref[...]).astype(BF16)
        xn_ref[...] = xn
        o_ref[...] = x + hidden_tile(xn)

    @pl.when(pl.program_id(1) > 0)
    def _():
        o_ref[...] += hidden_tile(xn_ref[...])


def _ffn(x, g, wg, wu, wd, layer, tm, tf):
    m, d = x.shape
    f = wg.shape[2]
    return pl.pallas_call(
        _ffn_kernel,
        grid=(m // tm, f // tf),
        in_specs=[
            pl.BlockSpec((tm, d), lambda i, j: (i, 0)),
            pl.BlockSpec((1, d), lambda i, j: (0, 0)),
            pl.BlockSpec((None, d, tf), lambda i, j: (layer, 0, j)),
            pl.BlockSpec((None, d, tf), lambda i, j: (layer, 0, j)),
            pl.BlockSpec((None, tf, d), lambda i, j: (layer, j, 0)),
        ],
        out_specs=pl.BlockSpec((tm, d), lambda i, j: (i, 0)),
        out_shape=jax.ShapeDtypeStruct((m, d), F32),
        scratch_shapes=[pltpu.VMEM((tm, d), BF16)],
        compiler_params=_cparams(("arbitrary", "arbitrary")),
        name="ffn",
    )(x, g, wg, wu, wd)


def _head_rmsnorm(x, g):
    r, c = x.shape
    same_head = (_iota((LANES, LANES), 0) // HEAD_DIM) == (_iota((LANES, LANES), 1) // HEAD_DIM)
    bd = jnp.where(same_head, 1.0 / HEAD_DIM, 0.0).astype(BF16)
    outs = []
    for t in range(c // LANES):
        xb = x[:, t * LANES:(t + 1) * LANES]
        ms = _dot((xb * xb).astype(BF16), bd)
        outs.append((xb * lax.rsqrt(ms + EPS)) * g[:, t * LANES:(t + 1) * LANES])
    return outs


def _half_select(half):
    src = _iota((LANES, LANES), 0)
    dst = _iota((LANES, LANES), 1)
    return jnp.where(src == (dst % HEAD_DIM) + HEAD_DIM * half, 1.0, 0.0).astype(BF16)


def _lane_halves(x):
    lo_half = _iota(x.shape, 1) < HEAD_DIM
    zero = jnp.zeros_like(x)
    return jnp.where(lo_half, x, zero), jnp.where(lo_half, zero, x)


def _merge_halves(lo, hi):
    return jnp.where(_iota(lo.shape, 1) < HEAD_DIM, lo, hi)


def _alibi_slope(h):
    return 2.0 ** (-8.0 * (h + 1) / SWA_HEADS)


def _softplus2(z2):
    neg_abs = lax.bitcast_convert_type(
        lax.bitcast_convert_type(z2, jnp.uint32) | jnp.uint32(0x80000000), F32)
    return jnp.maximum(z2, 0.0) + jnp.log2(1.0 + jnp.exp2(neg_abs))


def _suffix_matrix(n):
    return jnp.where(_iota((n, n), 0) > _iota((n, n), 1), 1.0, 0.0).astype(BF16)


def _sb_block(qe, kblk, vblk, suffix, carry, mask):
    z2 = _dot_nt(qe, kblk)
    sp = _softplus2(z2)
    if mask is not None:
        sp = jnp.where(mask, sp, 0.0)
    later = _dot(sp.astype(BF16), suffix)
    a = jnp.exp2((z2 - sp) - (later + carry))
    if mask is not None:
        a = jnp.where(mask, a, 0.0)
    contrib = _dot(a.astype(BF16), vblk)
    return contrib, carry + jnp.sum(sp, axis=-1, keepdims=True)


def _swa_head(qe, kdup, vdup, bias, mask, sink):
    s = jnp.where(mask, _dot_nt(qe, kdup) - bias, NEG)
    mx = jnp.maximum(jnp.max(s, axis=-1, keepdims=True), sink)
    e = jnp.exp(s - mx)
    den = jnp.sum(e, axis=-1, keepdims=True) + jnp.exp(sink - mx)
    return _dot(e.astype(BF16), vdup) / den


def _pad_rows(x, rows):
    return jnp.concatenate([x, jnp.zeros((rows - x.shape[0], x.shape[1]), x.dtype)], axis=0)


def _dup_kv(x_bf, kv_head):
    t = kv_head // 2
    return _dot(x_bf[:, t * LANES:(t + 1) * LANES], _half_select(kv_head % 2)).astype(BF16)


def _swa_tables():
    group = SWA_HEADS // SWA_KV_HEADS
    slope = jnp.asarray([_alibi_slope(h) * LOG2E for h in range(SWA_HEADS)], F32)
    slope = jnp.repeat(slope.reshape(SWA_KV_HEADS, group, 1, 1), BLOCK, axis=2)
    qi = jnp.arange(BLOCK, dtype=F32)[:, None]
    j = jnp.arange(BLOCK, dtype=F32)[None, :]
    dist_band = jnp.where(j <= qi, qi - j, qi - j + BLOCK)
    dist_meta = jnp.where(j < N_META, N_META + qi - j, 0.0)
    shape = (SWA_KV_HEADS, group * BLOCK, BLOCK)
    return ((slope * dist_band).reshape(shape), (slope * dist_meta).reshape(shape),
            jnp.broadcast_to(slope * BLOCK, slope.shape[:3] + (BLOCK,)).reshape(shape))


def _swa_kernel(q_ref, kc_ref, vc_ref, km_ref, vm_ref, qg_ref, kg_ref,
                band_ref, meta_ref, step_ref, sink_ref, o_ref, kd_ref, vd_ref, *, nsub):
    n = pl.program_id(1)
    group = SWA_HEADS // SWA_KV_HEADS
    rows = group * BLOCK

    def store_rows(k_rows, v_rows, start):
        kn = jnp.concatenate(_head_rmsnorm(k_rows, kg_ref[...]), axis=1).astype(BF16)
        vb = v_rows.astype(BF16)
        for g in range(SWA_KV_HEADS):
            kd_ref[g, start:start + BLOCK] = _dup_kv(kn, g)
            vd_ref[g, start:start + BLOCK] = _dup_kv(vb, g)

    @pl.when(n == 0)
    def _():
        store_rows(_pad_rows(km_ref[...], BLOCK), _pad_rows(vm_ref[...], BLOCK), 0)
        kd_ref[:, 2 * BLOCK:3 * BLOCK] = jnp.zeros((SWA_KV_HEADS, BLOCK, LANES), BF16)
        vd_ref[:, 2 * BLOCK:3 * BLOCK] = jnp.zeros((SWA_KV_HEADS, BLOCK, LANES), BF16)

    qi = _iota((rows, BLOCK), 0) % BLOCK
    j = _iota((rows, BLOCK), 1)
    in_cur = j <= qi
    meta_ok = j < N_META

    def one_block(sub):
        blk = nsub * n + sub
        rs = slice(sub * BLOCK, (sub + 1) * BLOCK)
        kd_ref[:, BLOCK:2 * BLOCK] = kd_ref[:, 2 * BLOCK:3 * BLOCK]
        vd_ref[:, BLOCK:2 * BLOCK] = vd_ref[:, 2 * BLOCK:3 * BLOCK]
        store_rows(kc_ref[rs, :], vc_ref[rs, :], 2 * BLOCK)
        qn = [x.astype(BF16) for x in _head_rmsnorm(q_ref[rs, :], qg_ref[...])]
        band_ok = in_cur | (blk > 0)
        blocks_before = blk.astype(F32)
        logits, weights, dens = {}, {}, {}

        def score(g):
            logits[g] = _dot_nt(
                jnp.concatenate(_lane_halves(qn[2 * g]) + _lane_halves(qn[2 * g + 1]), axis=0), kd_ref[g])

        def softmax(g):
            sink = sink_ref[g]
            s = logits.pop(g)
            s_band = jnp.where(in_cur, s[:, 2 * BLOCK:], s[:, BLOCK:2 * BLOCK]) - band_ref[g]
            s_band = jnp.where(band_ok, s_band, NEG)
            s_meta = jnp.where(meta_ok, (s[:, :BLOCK] - meta_ref[g]) - step_ref[g] * blocks_before, NEG)
            mx = jnp.maximum(jnp.max(jnp.maximum(s_band, s_meta), axis=-1, keepdims=True), sink)
            e_band = jnp.exp2(s_band - mx)
            e_meta = jnp.exp2(s_meta - mx)
            dens[g] = jnp.sum(e_band + e_meta, axis=-1, keepdims=True) + jnp.exp2(sink - mx)
            weights[g] = jnp.concatenate(
                [e_meta, jnp.where(in_cur, 0.0, e_band), jnp.where(in_cur, e_band, 0.0)], axis=1).astype(BF16)

        def attend(g):
            o = _dot(weights.pop(g), vd_ref[g]) / dens.pop(g)
            for t in range(2):
                o_ref[rs, (2 * g + t) * LANES:(2 * g + t + 1) * LANES] = _merge_halves(
                    o[2 * t * BLOCK:(2 * t + 1) * BLOCK], o[(2 * t + 1) * BLOCK:(2 * t + 2) * BLOCK])

        for stage, g in ((score, 0), (score, 1), (softmax, 0), (score, 2), (softmax, 1), (attend, 0),
                         (score, 3), (softmax, 2), (attend, 1), (softmax, 3), (attend, 2), (attend, 3)):
            stage(g)

    for sub in range(nsub):
        one_block(sub)


def _swa(proj, proj_meta, sinks, qg, kg, tables, batch, nsub):
    m = proj.shape[0]
    tb = nsub * BLOCK
    nb = m // batch // tb
    qw = SWA_HEADS * HEAD_DIM
    kvw = SWA_KV_HEADS * HEAD_DIM
    group = SWA_HEADS // SWA_KV_HEADS
    kcol, vcol = qw // kvw, qw // kvw + 1
    row = lambda b, n: b * nb + n
    whole = lambda shape: pl.BlockSpec(shape, lambda b, n: (0,) * len(shape))
    sink_col = jnp.broadcast_to((sinks * LOG2E).reshape(SWA_KV_HEADS, group, 1, 1),
                                (SWA_KV_HEADS, group, BLOCK, LANES))
    sink_col = sink_col.reshape(SWA_KV_HEADS, group * BLOCK, LANES)
    band, meta, step = tables
    return pl.pallas_call(
        functools.partial(_swa_kernel, nsub=nsub),
        grid=(batch, nb),
        in_specs=[
            pl.BlockSpec((tb, qw), lambda b, n: (row(b, n), 0)),
            pl.BlockSpec((tb, kvw), lambda b, n: (row(b, n), kcol)),
            pl.BlockSpec((tb, kvw), lambda b, n: (row(b, n), vcol)),
            pl.BlockSpec((N_META, kvw), lambda b, n: (0, kcol)),
            pl.BlockSpec((N_META, kvw), lambda b, n: (0, vcol)),
            pl.BlockSpec((1, qw), lambda b, n: (0, 0)),
            pl.BlockSpec((1, kvw), lambda b, n: (0, 0)),
            whole(band.shape), whole(meta.shape), whole(step.shape), whole(sink_col.shape),
        ],
        out_specs=pl.BlockSpec((tb, qw), lambda b, n: (row(b, n), 0)),
        out_shape=jax.ShapeDtypeStruct((m, qw), F32),
        scratch_shapes=[pltpu.VMEM((SWA_KV_HEADS, 3 * BLOCK, LANES), BF16),
                        pltpu.VMEM((SWA_KV_HEADS, 3 * BLOCK, LANES), BF16)],
        compiler_params=_cparams(("arbitrary", "arbitrary")),
        name="swa",
    )(proj, proj, proj, proj_meta, proj_meta, qg, kg, band, meta, step, sink_col)


def _sb_kernel(*refs, tq, hp, n_q):
    q_refs = refs[:n_q]
    kbf_ref, vbf_ref, km_ref, vm_ref, o_ref, kmp_ref, vmp_ref, acc_ref, car_ref, more_ref = refs[n_q:]
    per_q = hp // n_q
    i = pl.program_id(2)

    @pl.when(i == 0)
    def _():
        kmp_ref[...] = jnp.zeros_like(kmp_ref)
        vmp_ref[...] = jnp.zeros_like(vmp_ref)
        kmp_ref[0:N_META, :] = km_ref[...].astype(BF16)
        vmp_ref[0:N_META, :] = vm_ref[...].astype(BF16)

    lanes = lambda p: slice(p * LANES, (p + 1) * LANES)
    rows = 2 * tq

    def queries():
        return [jnp.concatenate(_lane_halves(
                    (q_refs[p // per_q][:, lanes(p % per_q)] * SB_QSCALE).astype(BF16)), axis=0)
                for p in range(hp)]

    def note_carries(carries):
        m = functools.reduce(jnp.minimum, carries)
        while m.shape[0] > 8:
            half = m.shape[0] // 2
            m = jnp.minimum(m[:half], m[half:])
        more_ref[0] = (jnp.min(m) <= F32_UNDERFLOW_LOG2).astype(jnp.int32)

    def block_step(kv_of_pair, width, mask, first):
        qs = queries()
        kv = [kv_of_pair(p) for p in range(hp)]
        z2 = [_dot_nt(qs[p], kv[p][0]) for p in range(hp)]
        sp = [_softplus2(z) for z in z2]
        if mask is not None:
            sp = [jnp.where(mask, s, 0.0) for s in sp]
        zs = [z2[p] - sp[p] for p in range(hp)]
        later = _dot(jnp.concatenate([s.astype(BF16) for s in sp], axis=0), _suffix_matrix(width))
        carries = []
        for p in range(hp):
            carry = 0.0 if first else car_ref[p]
            a = jnp.exp2(zs[p] - (later[p * rows:(p + 1) * rows] + carry))
            if mask is not None:
                a = jnp.where(mask, a, 0.0)
            d = _dot(a.astype(BF16), kv[p][1])
            acc_ref[p] = d if first else acc_ref[p] + d
            carries.append(carry + jnp.sum(sp[p], axis=-1, keepdims=True))
            car_ref[p] = carries[-1]
        note_carries(carries)

    def kv_block(start):
        return lambda p: (kbf_ref[pl.ds(start, tq), lanes(p)], vbf_ref[pl.ds(start, tq), lanes(p)])

    below_diag = lambda: _iota((rows, tq), 1) < (_iota((rows, tq), 0) % tq)
    prev_start = pl.multiple_of(jnp.maximum(i - 1, 0) * tq, tq)

    @pl.when(i == 0)
    def _():
        block_step(kv_block(0), tq, below_diag(), True)

    @pl.when(i > 0)
    def _():
        qs = queries()
        in_cur = below_diag()
        cur_start = pl.multiple_of(i * tq, tq)
        zs, spb = [], []
        for p in range(hp):
            z_cur = _dot_nt(qs[p], kbf_ref[pl.ds(cur_start, tq), lanes(p)])
            z_prev = _dot_nt(qs[p], kbf_ref[pl.ds(prev_start, tq), lanes(p)])
            z2 = jnp.where(in_cur, z_cur, z_prev)
            sp = _softplus2(z2)
            zs.append(z2 - sp)
            spb.append(sp.astype(BF16))
        zero = jnp.zeros((rows, tq), BF16)
        prev_part = lambda x: jnp.where(in_cur, zero, x)
        cur_part = lambda x: jnp.where(in_cur, x, zero)
        suf = _dot(jnp.concatenate(spb, axis=0), _suffix_matrix(tq))
        prev_tot = _dot(jnp.concatenate([prev_part(s) for s in spb], axis=0), jnp.ones((tq, LANES), BF16))
        carries = []
        for p in range(hp):
            sf = suf[p * rows:(p + 1) * rows]
            pt = prev_tot[p * rows:(p + 1) * rows]
            total = sf[:, 0:1] + spb[p][:, 0:1].astype(F32)
            widen = lambda x: jnp.concatenate([x] * (tq // LANES), axis=1)
            later = sf + jnp.where(in_cur, widen(-pt), widen(total - pt))
            a = jnp.exp2(zs[p] - later).astype(BF16)
            v2 = jnp.concatenate([vbf_ref[pl.ds(prev_start, tq), lanes(p)],
                                  vbf_ref[pl.ds(cur_start, tq), lanes(p)]], axis=0)
            acc_ref[p] = _dot(jnp.concatenate([prev_part(a), cur_part(a)], axis=1), v2)
            car_ref[p] = total
            carries.append(total)
        note_carries(carries)

    @pl.when(more_ref[0] > 0)
    def _():
        @pl.when(i > 0)
        def _():
            block_step(kv_block(prev_start), tq, below_diag(), False)

        def body(t):
            block_step(kv_block(pl.multiple_of((i - 2 - t) * tq, tq)), tq, None, False)
            return t + 1

        lax.while_loop(lambda t: (t < i - 1) & (more_ref[0] > 0), body, jnp.int32(0))

        @pl.when(more_ref[0] > 0)
        def _():
            block_step(lambda p: (kmp_ref[:, lanes(p)], vmp_ref[:, lanes(p)]), LANES,
                       _iota((rows, LANES), 1) < N_META, False)

    for p in range(hp):
        o_ref[:, lanes(p)] = _merge_halves(acc_ref[p, 0:tq], acc_ref[p, tq:rows])


def _sb(proj, proj_b, proj_meta_b, batch, tq, hp):
    m = proj.shape[0]
    seq = m // batch
    nq = seq // tq
    w = hp * LANES
    steps = SB_HEADS * HEAD_DIM // w
    q_start = (SWA_HEADS + 2 * SWA_KV_HEADS) * HEAD_DIM
    qw = min(w, 4 * LANES)
    n_q = w // qw
    assert q_start % qw == 0
    q_spec = lambda t: pl.BlockSpec((tq, qw), lambda b, p, i: (b * nq + i, q_start // qw + p * n_q + t))
    return pl.pallas_call(
        functools.partial(_sb_kernel, tq=tq, hp=hp, n_q=n_q),
        grid=(batch, steps, nq),
        in_specs=[q_spec(t) for t in range(n_q)] + [
            pl.BlockSpec((seq, w), lambda b, p, i: (b, steps + p)),
            pl.BlockSpec((seq, w), lambda b, p, i: (b, 2 * steps + p)),
            pl.BlockSpec((N_META, w), lambda b, p, i: (0, steps + p)),
            pl.BlockSpec((N_META, w), lambda b, p, i: (0, 2 * steps + p)),
        ],
        out_specs=pl.BlockSpec((tq, w), lambda b, p, i: (b * nq + i, p)),
        out_shape=jax.ShapeDtypeStruct((m, SB_HEADS * HEAD_DIM), F32),
        scratch_shapes=[pltpu.VMEM((LANES, w), BF16), pltpu.VMEM((LANES, w), BF16),
                        pltpu.VMEM((hp, 2 * tq, LANES), F32), pltpu.VMEM((hp, 2 * tq, 1), F32),
                        pltpu.SMEM((1,), jnp.int32)],
        compiler_params=_cparams(("arbitrary", "arbitrary", "arbitrary")),
        name="stickbreak",
    )(*([proj] * n_q), proj_b, proj_b, proj_meta_b, proj_meta_b)


def _meta_attn_kernel(sink_ref, p_ref, qg_ref, kg_ref, oa_ref, ob_ref):
    qw = SWA_HEADS * HEAD_DIM
    kvw = SWA_KV_HEADS * HEAD_DIM
    group = SWA_HEADS // SWA_KV_HEADS
    x = p_ref[...]
    xk = _pad_rows(x, LANES)
    qi = _iota((N_META, LANES), 0)
    kj = _iota((N_META, LANES), 1)

    qn = [(t * SCALE).astype(BF16) for t in _head_rmsnorm(x[:, 0:qw], qg_ref[...])]
    kn = jnp.concatenate(_head_rmsnorm(xk[:, qw:qw + kvw], kg_ref[...]), axis=1).astype(BF16)
    vb = xk[:, qw + kvw:qw + 2 * kvw].astype(BF16)
    delta_f = (qi - kj).astype(F32)
    causal = kj <= qi
    for pair in range(SWA_HEADS // 2):
        kv = (2 * pair) // group
        kdup, vdup = _dup_kv(kn, kv), _dup_kv(vb, kv)
        outs = []
        for e, qe in enumerate(_lane_halves(qn[pair])):
            h = 2 * pair + e
            outs.append(_swa_head(qe, kdup, vdup, _alibi_slope(h) * delta_f, causal, sink_ref[h]))
        oa_ref[:, pair * LANES:(pair + 1) * LANES] = _merge_halves(outs[0], outs[1])

    base = qw + 2 * kvw
    sbw = SB_HEADS * HEAD_DIM
    strict = kj < qi
    suffix = _suffix_matrix(LANES)
    zero_c = jnp.zeros((N_META, 1), F32)
    for pair in range(SB_HEADS // 2):
        sl = lambda src, part: src[:, base + part * sbw + pair * LANES: base + part * sbw + (pair + 1) * LANES]
        kb, vb2 = sl(xk, 1).astype(BF16), sl(xk, 2).astype(BF16)
        outs = [_sb_block(qe, kb, vb2, suffix, zero_c, strict)[0]
                for qe in _lane_halves((sl(x, 0) * SB_QSCALE).astype(BF16))]
        ob_ref[:, pair * LANES:(pair + 1) * LANES] = _merge_halves(outs[0], outs[1])


def _meta_attn(proj_meta, sinks, qg, kg, batch):
    pw = proj_meta.shape[1]
    qw = SWA_HEADS * HEAD_DIM
    kvw = SWA_KV_HEADS * HEAD_DIM
    sbw = SB_HEADS * HEAD_DIM
    return pl.pallas_call(
        _meta_attn_kernel,
        grid=(batch,),
        in_specs=[
            pl.BlockSpec(memory_space=pltpu.SMEM),
            pl.BlockSpec((N_META, pw), lambda b: (b, 0)),
            pl.BlockSpec((1, qw), lambda b: (0, 0)),
            pl.BlockSpec((1, kvw), lambda b: (0, 0)),
        ],
        out_specs=[pl.BlockSpec((N_META, qw), lambda b: (b, 0)),
                   pl.BlockSpec((N_META, sbw), lambda b: (b, 0))],
        out_shape=[jax.ShapeDtypeStruct((batch * N_META, qw), F32),
                   jax.ShapeDtypeStruct((batch * N_META, sbw), F32)],
        compiler_params=_cparams(("arbitrary",)),
        name="meta_attn",
    )(sinks, proj_meta, qg, kg)


def kernel(x, meta_tokens, attn_norm_g, w_in, q_norm_g, k_norm_g, attn_sinks,
           swa_out_g, sb_out_g, w_o, ffn_norm_g, w_gate, w_up, w_down):
    batch, seq, d = x.shape
    depth = w_in.shape[0]
    assert (batch * seq) % 1024 == 0 and seq % 256 == 0 and meta_tokens.shape[0] == N_META

    tm = 512
    tmeta = N_META
    tm_in = 1024
    tn_in = 1536
    assert tn_in == (SWA_HEADS + 2 * SWA_KV_HEADS) * HEAD_DIM
    tn_out = 2048
    tm_ffn = 1024
    tf = 512
    swa_sub = 4
    tq = 256
    hp = 8

    h = x.reshape(batch * seq, d)
    hm = meta_tokens.astype(x.dtype)

    w_in_b, w_o_b = w_in.astype(BF16), w_o.astype(BF16)
    w_gate_b, w_up_b, w_down_b = w_gate.astype(BF16), w_up.astype(BF16), w_down.astype(BF16)

    swa_tables = _swa_tables()
    for l in range(depth):
        g_attn = attn_norm_g[l].reshape(1, d)
        qg = jnp.tile(q_norm_g[l], SWA_HEADS).reshape(1, -1)
        kg = jnp.tile(k_norm_g[l], SWA_KV_HEADS).reshape(1, -1)
        sinks = attn_sinks[l].astype(F32)
        ga, gb = swa_out_g[l].reshape(1, -1), sb_out_g[l].reshape(1, -1)
        g_ffn = ffn_norm_g[l].reshape(1, d)

        proj, proj_b = _norm_matmul(h, g_attn, w_in_b, l, tm_in, tn_in)
        proj_m, proj_mb = _norm_matmul(hm, g_attn, w_in_b, l, tmeta, tn_in)

        out_a = _swa(proj, proj_m, sinks, qg * (SCALE * LOG2E), kg, swa_tables, batch, swa_sub)
        out_b = _sb(proj, proj_b, proj_mb, batch, tq, hp)
        out_am, out_bm = _meta_attn(proj_m, sinks, qg, kg, 1)

        h = _outproj(out_a, out_b, ga, gb, w_o_b, l, h, tm, tn_out)
        hm = _outproj(out_am, out_bm, ga, gb, w_o_b, l, hm, tmeta, tn_out)

        h = _ffn(h, g_ffn, w_gate_b, w_up_b, w_down_b, l, tm_ffn, tf)
        hm = _ffn(hm, g_ffn, w_gate_b, w_up_b, w_down_b, l, tmeta, tf)

    return h.reshape(batch, seq, d)
```

```python
import functools

import jax
import jax.numpy as jnp
from jax import lax
from jax.experimental import pallas as pl
from jax.experimental.pallas import tpu as pltpu

F32 = jnp.float32
BF16 = jnp.bfloat16

N_META = 16
BLOCK = 128
HEAD_DIM = 64
LANES = 128
SWA_HEADS = 16
SWA_KV_HEADS = 4
SB_HEADS = 16
EPS = 1e-6
NEG = -1e30
SCALE = HEAD_DIM ** -0.5
LOG2E = 1.4426950408889634
SB_QSCALE = SCALE * LOG2E
F32_UNDERFLOW_LOG2 = 160.0
VMEM_LIMIT = 56 * 1024 * 1024


def _cparams(sem):
    return pltpu.CompilerParams(dimension_semantics=sem, vmem_limit_bytes=VMEM_LIMIT)


def _dot(a, b):
    return jnp.dot(a, b, preferred_element_type=F32)


def _dot_nt(a, b):
    return lax.dot_general(a, b, (((1,), (1,)), ((), ())), preferred_element_type=F32)


def _iota(shape, dim):
    return lax.broadcasted_iota(jnp.int32, shape, dim)


def _norm_mm_kernel(x_ref, g_ref, w_ref, o_ref, ob_ref, xn_ref):
    def column_tile(xn):
        y = _dot(xn, w_ref[...])
        o_ref[...] = y
        ob_ref[...] = y.astype(BF16)

    @pl.when(pl.program_id(1) == 0)
    def _():
        x = x_ref[...]
        ms = jnp.mean(x * x, axis=-1, keepdims=True)
        xn = ((x * lax.rsqrt(ms + EPS)) * g_ref[...]).astype(BF16)
        xn_ref[...] = xn
        column_tile(xn)

    @pl.when(pl.program_id(1) > 0)
    def _():
        column_tile(xn_ref[...])


def _norm_matmul(x, g, w, layer, tm, tn):
    m, d = x.shape
    n = w.shape[2]
    return pl.pallas_call(
        _norm_mm_kernel,
        grid=(m // tm, n // tn),
        in_specs=[
            pl.BlockSpec((tm, d), lambda i, j: (i, 0)),
            pl.BlockSpec((1, d), lambda i, j: (0, 0)),
            pl.BlockSpec((None, d, tn), lambda i, j: (layer, 0, j)),
        ],
        out_specs=[pl.BlockSpec((tm, tn), lambda i, j: (i, j)),
                   pl.BlockSpec((tm, tn), lambda i, j: (i, jnp.maximum(j - 1, 0)))],
        out_shape=[jax.ShapeDtypeStruct((m, n), F32), jax.ShapeDtypeStruct((m, n - tn), BF16)],
        scratch_shapes=[pltpu.VMEM((tm, d), BF16)],
        compiler_params=_cparams(("arbitrary", "arbitrary")),
        name="norm_inproj",
    )(x, g, w)


def _outproj_kernel(a_ref, b_ref, ga_ref, gb_ref, wa_ref, wb_ref, h_ref, o_ref):
    def normed(src, g):
        x = src[...]
        ms = jnp.mean(x * x, axis=-1, keepdims=True)
        return ((x * lax.rsqrt(ms + EPS)) * g[...]).astype(BF16)

    o_ref[...] = h_ref[...] + (_dot(normed(a_ref, ga_ref), wa_ref[...])
                               + _dot(normed(b_ref, gb_ref), wb_ref[...]))


def _outproj(a, b, ga, gb, w, layer, h, tm, tn):
    m, wdt = a.shape
    d = h.shape[1]
    assert tn == d
    return pl.pallas_call(
        _outproj_kernel,
        grid=(m // tm, d // tn),
        in_specs=[
            pl.BlockSpec((tm, wdt), lambda i, j: (i, 0)),
            pl.BlockSpec((tm, wdt), lambda i, j: (i, 0)),
            pl.BlockSpec((1, wdt), lambda i, j: (0, 0)),
            pl.BlockSpec((1, wdt), lambda i, j: (0, 0)),
            pl.BlockSpec((None, wdt, tn), lambda i, j: (layer, 0, j)),
            pl.BlockSpec((None, wdt, tn), lambda i, j: (layer, 1, j)),
            pl.BlockSpec((tm, tn), lambda i, j: (i, j)),
        ],
        out_specs=pl.BlockSpec((tm, tn), lambda i, j: (i, j)),
        out_shape=jax.ShapeDtypeStruct((m, d), F32),
        compiler_params=_cparams(("arbitrary", "arbitrary")),
        name="outproj",
    )(a, b, ga, gb, w, w, h)


def _ffn_kernel(x_ref, g_ref, wg_ref, wu_ref, wd_ref, o_ref, xn_ref):
    def hidden_tile(xn):
        gate = _dot(xn, wg_ref[...])
        up = _dot(xn, wu_ref[...])
        act = (gate / (1.0 + jnp.exp(-gate))) * up
        return _dot(act.astype(BF16), wd_ref[...])

    @pl.when(pl.program_id(1) == 0)
    def _():
        x = x_ref[...]
        ms = jnp.mean(x * x, axis=-1, keepdims=True)
        xn = ((x * lax.rsqrt(ms + EPS)) * g_ref[...]).astype(BF16)
        xn_ref[...] = xn
        o_ref[...] = x + hidden_tile(xn)

    @pl.when(pl.program_id(1) > 0)
    def _():
        o_ref[...] += hidden_tile(xn_ref[...])


def _ffn(x, g, wg, wu, wd, layer, tm, tf):
    m, d = x.shape
    f = wg.shape[2]
    return pl.pallas_call(
        _ffn_kernel,
        grid=(m // tm, f // tf),
        in_specs=[
            pl.BlockSpec((tm, d), lambda i, j: (i, 0)),
            pl.BlockSpec((1, d), lambda i, j: (0, 0)),
            pl.BlockSpec((None, d, tf), lambda i, j: (layer, 0, j)),
            pl.BlockSpec((None, d, tf), lambda i, j: (layer, 0, j)),
            pl.BlockSpec((None, tf, d), lambda i, j: (layer, j, 0)),
        ],
        out_specs=pl.BlockSpec((tm, d), lambda i, j: (i, 0)),
        out_shape=jax.ShapeDtypeStruct((m, d), F32),
        scratch_shapes=[pltpu.VMEM((tm, d), BF16)],
        compiler_params=_cparams(("arbitrary", "arbitrary")),
        name="ffn",
    )(x, g, wg, wu, wd)


def _head_rmsnorm(x, g):
    r, c = x.shape
    same_head = (_iota((LANES, LANES), 0) // HEAD_DIM) == (_iota((LANES, LANES), 1) // HEAD_DIM)
    bd = jnp.where(same_head, 1.0 / HEAD_DIM, 0.0).astype(BF16)
    outs = []
    for t in range(c // LANES):
        xb = x[:, t * LANES:(t + 1) * LANES]
        ms = _dot((xb * xb).astype(BF16), bd)
        outs.append((xb * lax.rsqrt(ms + EPS)) * g[:, t * LANES:(t + 1) * LANES])
    return outs


def _half_select(half):
    src = _iota((LANES, LANES), 0)
    dst = _iota((LANES, LANES), 1)
    return jnp.where(src == (dst % HEAD_DIM) + HEAD_DIM * half, 1.0, 0.0).astype(BF16)


def _lane_halves(x):
    lo_half = _iota(x.shape, 1) < HEAD_DIM
    zero = jnp.zeros_like(x)
    return jnp.where(lo_half, x, zero), jnp.where(lo_half, zero, x)


def _merge_halves(lo, hi):
    return jnp.where(_iota(lo.shape, 1) < HEAD_DIM, lo, hi)


def _alibi_slope(h):
    return 2.0 ** (-8.0 * (h + 1) / SWA_HEADS)


def _softplus2(z2):
    neg_abs = lax.bitcast_convert_type(
        lax.bitcast_convert_type(z2, jnp.uint32) | jnp.uint32(0x80000000), F32)
    return jnp.maximum(z2, 0.0) + jnp.log2(1.0 + jnp.exp2(neg_abs))


def _suffix_matrix(n):
    return jnp.where(_iota((n, n), 0) > _iota((n, n), 1), 1.0, 0.0).astype(BF16)


def _sb_block(qe, kblk, vblk, suffix, carry, mask):
    z2 = _dot_nt(qe, kblk)
    sp = _softplus2(z2)
    if mask is not None:
        sp = jnp.where(mask, sp, 0.0)
    later = _dot(sp.astype(BF16), suffix)
    a = jnp.exp2((z2 - sp) - (later + carry))
    if mask is not None:
        a = jnp.where(mask, a, 0.0)
    contrib = _dot(a.astype(BF16), vblk)
    return contrib, carry + jnp.sum(sp, axis=-1, keepdims=True)


def _swa_head(qe, kdup, vdup, bias, mask, sink):
    s = jnp.where(mask, _dot_nt(qe, kdup) - bias, NEG)
    mx = jnp.maximum(jnp.max(s, axis=-1, keepdims=True), sink)
    e = jnp.exp(s - mx)
    den = jnp.sum(e, axis=-1, keepdims=True) + jnp.exp(sink - mx)
    return _dot(e.astype(BF16), vdup) / den


def _pad_rows(x, rows):
    return jnp.concatenate([x, jnp.zeros((rows - x.shape[0], x.shape[1]), x.dtype)], axis=0)


def _dup_kv(x_bf, kv_head):
    t = kv_head // 2
    return _dot(x_bf[:, t * LANES:(t + 1) * LANES], _half_select(kv_head % 2)).astype(BF16)


def _swa_tables():
    group = SWA_HEADS // SWA_KV_HEADS
    slope = jnp.asarray([_alibi_slope(h) * LOG2E for h in range(SWA_HEADS)], F32)
    slope = jnp.repeat(slope.reshape(SWA_KV_HEADS, group, 1, 1), BLOCK, axis=2)
    qi = jnp.arange(BLOCK, dtype=F32)[:, None]
    j = jnp.arange(BLOCK, dtype=F32)[None, :]
    dist_band = jnp.where(j <= qi, qi - j, qi - j + BLOCK)
    dist_meta = jnp.where(j < N_META, N_META + qi - j, 0.0)
    shape = (SWA_KV_HEADS, group * BLOCK, BLOCK)
    return ((slope * dist_band).reshape(shape), (slope * dist_meta).reshape(shape),
            jnp.broadcast_to(slope * BLOCK, slope.shape[:3] + (BLOCK,)).reshape(shape))


def _swa_kernel(q_ref, kc_ref, vc_ref, km_ref, vm_ref, qg_ref, kg_ref,
                band_ref, meta_ref, step_ref, sink_ref, o_ref, kd_ref, vd_ref, *, nsub):
    n = pl.program_id(1)
    group = SWA_HEADS // SWA_KV_HEADS
    rows = group * BLOCK

    def store_rows(k_rows, v_rows, start):
        kn = jnp.concatenate(_head_rmsnorm(k_rows, kg_ref[...]), axis=1).astype(BF16)
        vb = v_rows.astype(BF16)
        for g in range(SWA_KV_HEADS):
            kd_ref[g, start:start + BLOCK] = _dup_kv(kn, g)
            vd_ref[g, start:start + BLOCK] = _dup_kv(vb, g)

    @pl.when(n == 0)
    def _():
        store_rows(_pad_rows(km_ref[...], BLOCK), _pad_rows(vm_ref[...], BLOCK), 0)
        kd_ref[:, 2 * BLOCK:3 * BLOCK] = jnp.zeros((SWA_KV_HEADS, BLOCK, LANES), BF16)
        vd_ref[:, 2 * BLOCK:3 * BLOCK] = jnp.zeros((SWA_KV_HEADS, BLOCK, LANES), BF16)

    qi = _iota((rows, BLOCK), 0) % BLOCK
    j = _iota((rows, BLOCK), 1)
    in_cur = j <= qi
    meta_ok = j < N_META

    def one_block(sub):
        blk = nsub * n + sub
        rs = slice(sub * BLOCK, (sub + 1) * BLOCK)
        kd_ref[:, BLOCK:2 * BLOCK] = kd_ref[:, 2 * BLOCK:3 * BLOCK]
        vd_ref[:, BLOCK:2 * BLOCK] = vd_ref[:, 2 * BLOCK:3 * BLOCK]
        store_rows(kc_ref[rs, :], vc_ref[rs, :], 2 * BLOCK)
        qn = [x.astype(BF16) for x in _head_rmsnorm(q_ref[rs, :], qg_ref[...])]
        band_ok = in_cur | (blk > 0)
        blocks_before = blk.astype(F32)
        logits, weights, dens = {}, {}, {}

        def score(g):
            logits[g] = _dot_nt(
                jnp.concatenate(_lane_halves(qn[2 * g]) + _lane_halves(qn[2 * g + 1]), axis=0), kd_ref[g])

        def softmax(g):
            sink = sink_ref[g]
            s = logits.pop(g)
            s_band = jnp.where(in_cur, s[:, 2 * BLOCK:], s[:, BLOCK:2 * BLOCK]) - band_ref[g]
            s_band = jnp.where(band_ok, s_band, NEG)
            s_meta = jnp.where(meta_ok, (s[:, :BLOCK] - meta_ref[g]) - step_ref[g] * blocks_before, NEG)
            mx = jnp.maximum(jnp.max(jnp.maximum(s_band, s_meta), axis=-1, keepdims=True), sink)
            e_band = jnp.exp2(s_band - mx)
            e_meta = jnp.exp2(s_meta - mx)
            dens[g] = jnp.sum(e_band + e_meta, axis=-1, keepdims=True) + jnp.exp2(sink - mx)
            weights[g] = jnp.concatenate(
                [e_meta, jnp.where(in_cur, 0.0, e_band), jnp.where(in_cur, e_band, 0.0)], axis=1).astype(BF16)

        def attend(g):
            o = _dot(weights.pop(g), vd_ref[g]) / dens.pop(g)
            for t in range(2):
                o_ref[rs, (2 * g + t) * LANES:(2 * g + t + 1) * LANES] = _merge_halves(
                    o[2 * t * BLOCK:(2 * t + 1) * BLOCK], o[(2 * t + 1) * BLOCK:(2 * t + 2) * BLOCK])

        for stage, g in ((score, 0), (score, 1), (softmax, 0), (score, 2), (softmax, 1), (attend, 0),
                         (score, 3), (softmax, 2), (attend, 1), (softmax, 3), (attend, 2), (attend, 3)):
            stage(g)

    for sub in range(nsub):
        one_block(sub)


def _swa(proj, proj_meta, sinks, qg, kg, tables, batch, nsub):
    m = proj.shape[0]
    tb = nsub * BLOCK
    nb = m // batch // tb
    qw = SWA_HEADS * HEAD_DIM
    kvw = SWA_KV_HEADS * HEAD_DIM
    group = SWA_HEADS // SWA_KV_HEADS
    kcol, vcol = qw // kvw, qw // kvw + 1
    row = lambda b, n: b * nb + n
    whole = lambda shape: pl.BlockSpec(shape, lambda b, n: (0,) * len(shape))
    sink_col = jnp.broadcast_to((sinks * LOG2E).reshape(SWA_KV_HEADS, group, 1, 1),
                                (SWA_KV_HEADS, group, BLOCK, LANES))
    sink_col = sink_col.reshape(SWA_KV_HEADS, group * BLOCK, LANES)
    band, meta, step = tables
    return pl.pallas_call(
        functools.partial(_swa_kernel, nsub=nsub),
        grid=(batch, nb),
        in_specs=[
            pl.BlockSpec((tb, qw), lambda b, n: (row(b, n), 0)),
            pl.BlockSpec((tb, kvw), lambda b, n: (row(b, n), kcol)),
            pl.BlockSpec((tb, kvw), lambda b, n: (row(b, n), vcol)),
            pl.BlockSpec((N_META, kvw), lambda b, n: (0, kcol)),
            pl.BlockSpec((N_META, kvw), lambda b, n: (0, vcol)),
            pl.BlockSpec((1, qw), lambda b, n: (0, 0)),
            pl.BlockSpec((1, kvw), lambda b, n: (0, 0)),
            whole(band.shape), whole(meta.shape), whole(step.shape), whole(sink_col.shape),
        ],
        out_specs=pl.BlockSpec((tb, qw), lambda b, n: (row(b, n), 0)),
        out_shape=jax.ShapeDtypeStruct((m, qw), F32),
        scratch_shapes=[pltpu.VMEM((SWA_KV_HEADS, 3 * BLOCK, LANES), BF16),
                        pltpu.VMEM((SWA_KV_HEADS, 3 * BLOCK, LANES), BF16)],
        compiler_params=_cparams(("arbitrary", "arbitrary")),
        name="swa",
    )(proj, proj, proj, proj_meta, proj_meta, qg, kg, band, meta, step, sink_col)


def _sb_kernel(*refs, tq, hp, n_q):
    q_refs = refs[:n_q]
    kbf_ref, vbf_ref, km_ref, vm_ref, o_ref, kmp_ref, vmp_ref, acc_ref, car_ref, more_ref = refs[n_q:]
    per_q = hp // n_q
    i = pl.program_id(2)

    @pl.when(i == 0)
    def _():
        kmp_ref[...] = jnp.zeros_like(kmp_ref)
        vmp_ref[...] = jnp.zeros_like(vmp_ref)
        kmp_ref[0:N_META, :] = km_ref[...].astype(BF16)
        vmp_ref[0:N_META, :] = vm_ref[...].astype(BF16)

    lanes = lambda p: slice(p * LANES, (p + 1) * LANES)
    rows = 2 * tq

    def queries():
        return [jnp.concatenate(_lane_halves(
                    (q_refs[p // per_q][:, lanes(p % per_q)] * SB_QSCALE).astype(BF16)), axis=0)
                for p in range(hp)]

    def publish(p, acc):
        acc_ref[p] = acc
        o_ref[:, lanes(p)] = _merge_halves(acc[0:tq], acc[tq:rows])

    def note_carries(carries):
        m = functools.reduce(jnp.minimum, carries)
        while m.shape[0] > 8:
            half = m.shape[0] // 2
            m = jnp.minimum(m[:half], m[half:])
        more_ref[0] = (jnp.min(m) <= F32_UNDERFLOW_LOG2).astype(jnp.int32)

    def block_step(kv_of_pair, width, mask, first):
        qs = queries()
        kv = [kv_of_pair(p) for p in range(hp)]
        z2 = [_dot_nt(qs[p], kv[p][0]) for p in range(hp)]
        sp = [_softplus2(z) for z in z2]
        if mask is not None:
            sp = [jnp.where(mask, s, 0.0) for s in sp]
        zs = [z2[p] - sp[p] for p in range(hp)]
        later = _dot(jnp.concatenate([s.astype(BF16) for s in sp], axis=0), _suffix_matrix(width))
        carries = []
        for p in range(hp):
            carry = 0.0 if first else car_ref[p]
            a = jnp.exp2(zs[p] - (later[p * rows:(p + 1) * rows] + carry))
            if mask is not None:
                a = jnp.where(mask, a, 0.0)
            d = _dot(a.astype(BF16), kv[p][1])
            publish(p, d if first else acc_ref[p] + d)
            carries.append(carry + jnp.sum(sp[p], axis=-1, keepdims=True))
            car_ref[p] = carries[-1]
        note_carries(carries)

    def kv_block(start):
        return lambda p: (kbf_ref[pl.ds(start, tq), lanes(p)], vbf_ref[pl.ds(start, tq), lanes(p)])

    below_diag = lambda: _iota((rows, tq), 1) < (_iota((rows, tq), 0) % tq)
    prev_start = pl.multiple_of(jnp.maximum(i - 1, 0) * tq, tq)

    @pl.when(i == 0)
    def _():
        block_step(kv_block(0), tq, below_diag(), True)

    @pl.when(i > 0)
    def _():
        qs = queries()
        in_cur = below_diag()
        cur_start = pl.multiple_of(i * tq, tq)
        zs, spb = [], []
        for p in range(hp):
            z_cur = _dot_nt(qs[p], kbf_ref[pl.ds(cur_start, tq), lanes(p)])
            z_prev = _dot_nt(qs[p], kbf_ref[pl.ds(prev_start, tq), lanes(p)])
            z2 = jnp.where(in_cur, z_cur, z_prev)
            sp = _softplus2(z2)
            zs.append(z2 - sp)
            spb.append(sp.astype(BF16))
        zero = jnp.zeros((rows, tq), BF16)
        prev_part = lambda x: jnp.where(in_cur, zero, x)
        cur_part = lambda x: jnp.where(in_cur, x, zero)
        suf = _dot(jnp.concatenate(spb, axis=0), _suffix_matrix(tq))
        prev_tot = _dot(jnp.concatenate([prev_part(s) for s in spb], axis=0), jnp.ones((tq, LANES), BF16))
        carries = []
        for p in range(hp):
            sf = suf[p * rows:(p + 1) * rows]
            pt = prev_tot[p * rows:(p + 1) * rows]
            total = sf[:, 0:1] + spb[p][:, 0:1].astype(F32)
            widen = lambda x: jnp.concatenate([x] * (tq // LANES), axis=1)
            later = sf + jnp.where(in_cur, widen(-pt), widen(total - pt))
            a = jnp.exp2(zs[p] - later).astype(BF16)
            v2 = jnp.concatenate([vbf_ref[pl.ds(prev_start, tq), lanes(p)],
                                  vbf_ref[pl.ds(cur_start, tq), lanes(p)]], axis=0)
            publish(p, _dot(jnp.concatenate([prev_part(a), cur_part(a)], axis=1), v2))
            car_ref[p] = total
            carries.append(total)
        note_carries(carries)

    @pl.when(more_ref[0] > 0)
    def _():
        @pl.when(i > 0)
        def _():
            block_step(kv_block(prev_start), tq, below_diag(), False)

        def body(t):
            block_step(kv_block(pl.multiple_of((i - 2 - t) * tq, tq)), tq, None, False)
            return t + 1

        lax.while_loop(lambda t: (t < i - 1) & (more_ref[0] > 0), body, jnp.int32(0))

        @pl.when(more_ref[0] > 0)
        def _():
            block_step(lambda p: (kmp_ref[:, lanes(p)], vmp_ref[:, lanes(p)]), LANES,
                       _iota((rows, LANES), 1) < N_META, False)


def _sb(proj, proj_b, proj_meta_b, batch, tq, hp):
    m = proj.shape[0]
    seq = m // batch
    nq = seq // tq
    w = hp * LANES
    steps = SB_HEADS * HEAD_DIM // w
    q_start = (SWA_HEADS + 2 * SWA_KV_HEADS) * HEAD_DIM
    qw = min(w, 4 * LANES)
    n_q = w // qw
    assert q_start % qw == 0
    q_spec = lambda t: pl.BlockSpec((tq, qw), lambda b, p, i: (b * nq + i, q_start // qw + p * n_q + t))
    return pl.pallas_call(
        functools.partial(_sb_kernel, tq=tq, hp=hp, n_q=n_q),
        grid=(batch, steps, nq),
        in_specs=[q_spec(t) for t in range(n_q)] + [
            pl.BlockSpec((seq, w), lambda b, p, i: (b, steps + p)),
            pl.BlockSpec((seq, w), lambda b, p, i: (b, 2 * steps + p)),
            pl.BlockSpec((N_META, w), lambda b, p, i: (0, steps + p)),
            pl.BlockSpec((N_META, w), lambda b, p, i: (0, 2 * steps + p)),
        ],
        out_specs=pl.BlockSpec((tq, w), lambda b, p, i: (b * nq + i, p)),
        out_shape=jax.ShapeDtypeStruct((m, SB_HEADS * HEAD_DIM), F32),
        scratch_shapes=[pltpu.VMEM((LANES, w), BF16), pltpu.VMEM((LANES, w), BF16),
                        pltpu.VMEM((hp, 2 * tq, LANES), F32), pltpu.VMEM((hp, 2 * tq, 1), F32),
                        pltpu.SMEM((1,), jnp.int32)],
        compiler_params=_cparams(("arbitrary", "arbitrary", "arbitrary")),
        name="stickbreak",
    )(*([proj] * n_q), proj_b, proj_b, proj_meta_b, proj_meta_b)


def _meta_attn_kernel(sink_ref, p_ref, qg_ref, kg_ref, oa_ref, ob_ref):
    qw = SWA_HEADS * HEAD_DIM
    kvw = SWA_KV_HEADS * HEAD_DIM
    group = SWA_HEADS // SWA_KV_HEADS
    x = p_ref[...]
    xk = _pad_rows(x, LANES)
    qi = _iota((N_META, LANES), 0)
    kj = _iota((N_META, LANES), 1)

    qn = [(t * SCALE).astype(BF16) for t in _head_rmsnorm(x[:, 0:qw], qg_ref[...])]
    kn = jnp.concatenate(_head_rmsnorm(xk[:, qw:qw + kvw], kg_ref[...]), axis=1).astype(BF16)
    vb = xk[:, qw + kvw:qw + 2 * kvw].astype(BF16)
    delta_f = (qi - kj).astype(F32)
    causal = kj <= qi
    for pair in range(SWA_HEADS // 2):
        kv = (2 * pair) // group
        kdup, vdup = _dup_kv(kn, kv), _dup_kv(vb, kv)
        outs = []
        for e, qe in enumerate(_lane_halves(qn[pair])):
            h = 2 * pair + e
            outs.append(_swa_head(qe, kdup, vdup, _alibi_slope(h) * delta_f, causal, sink_ref[h]))
        oa_ref[:, pair * LANES:(pair + 1) * LANES] = _merge_halves(outs[0], outs[1])

    base = qw + 2 * kvw
    sbw = SB_HEADS * HEAD_DIM
    strict = kj < qi
    suffix = _suffix_matrix(LANES)
    zero_c = jnp.zeros((N_META, 1), F32)
    for pair in range(SB_HEADS // 2):
        sl = lambda src, part: src[:, base + part * sbw + pair * LANES: base + part * sbw + (pair + 1) * LANES]
        kb, vb2 = sl(xk, 1).astype(BF16), sl(xk, 2).astype(BF16)
        outs = [_sb_block(qe, kb, vb2, suffix, zero_c, strict)[0]
                for qe in _lane_halves((sl(x, 0) * SB_QSCALE).astype(BF16))]
        ob_ref[:, pair * LANES:(pair + 1) * LANES] = _merge_halves(outs[0], outs[1])


def _meta_attn(proj_meta, sinks, qg, kg, batch):
    pw = proj_meta.shape[1]
    qw = SWA_HEADS * HEAD_DIM
    kvw = SWA_KV_HEADS * HEAD_DIM
    sbw = SB_HEADS * HEAD_DIM
    return pl.pallas_call(
        _meta_attn_kernel,
        grid=(batch,),
        in_specs=[
            pl.BlockSpec(memory_space=pltpu.SMEM),
            pl.BlockSpec((N_META, pw), lambda b: (b, 0)),
            pl.BlockSpec((1, qw), lambda b: (0, 0)),
            pl.BlockSpec((1, kvw), lambda b: (0, 0)),
        ],
        out_specs=[pl.BlockSpec((N_META, qw), lambda b: (b, 0)),
                   pl.BlockSpec((N_META, sbw), lambda b: (b, 0))],
        out_shape=[jax.ShapeDtypeStruct((batch * N_META, qw), F32),
                   jax.ShapeDtypeStruct((batch * N_META, sbw), F32)],
        compiler_params=_cparams(("arbitrary",)),
        name="meta_attn",
    )(sinks, proj_meta, qg, kg)


def kernel(x, meta_tokens, attn_norm_g, w_in, q_norm_g, k_norm_g, attn_sinks,
           swa_out_g, sb_out_g, w_o, ffn_norm_g, w_gate, w_up, w_down):
    batch, seq, d = x.shape
    depth = w_in.shape[0]
    assert (batch * seq) % 1024 == 0 and seq % 256 == 0 and meta_tokens.shape[0] == N_META

    tm = 512
    tmeta = N_META
    tm_in = 1024
    tn_in = 1536
    assert tn_in == (SWA_HEADS + 2 * SWA_KV_HEADS) * HEAD_DIM
    tn_out = 2048
    tm_ffn = 1024
    tf = 512
    swa_sub = 4
    tq = 256
    hp = 8

    h = x.reshape(batch * seq, d)
    hm = meta_tokens.astype(x.dtype)

    w_in_b, w_o_b = w_in.astype(BF16), w_o.astype(BF16)
    w_gate_b, w_up_b, w_down_b = w_gate.astype(BF16), w_up.astype(BF16), w_down.astype(BF16)

    swa_tables = _swa_tables()
    for l in range(depth):
        g_attn = attn_norm_g[l].reshape(1, d)
        qg = jnp.tile(q_norm_g[l], SWA_HEADS).reshape(1, -1)
        kg = jnp.tile(k_norm_g[l], SWA_KV_HEADS).reshape(1, -1)
        sinks = attn_sinks[l].astype(F32)
        ga, gb = swa_out_g[l].reshape(1, -1), sb_out_g[l].reshape(1, -1)
        g_ffn = ffn_norm_g[l].reshape(1, d)

        proj, proj_b = _norm_matmul(h, g_attn, w_in_b, l, tm_in, tn_in)
        proj_m, proj_mb = _norm_matmul(hm, g_attn, w_in_b, l, tmeta, tn_in)

        out_a = _swa(proj, proj_m, sinks, qg * (SCALE * LOG2E), kg, swa_tables, batch, swa_sub)
        out_b = _sb(proj, proj_b, proj_mb, batch, tq, hp)
        out_am, out_bm = _meta_attn(proj_m, sinks, qg, kg, 1)

        h = _outproj(out_a, out_b, ga, gb, w_o_b, l, h, tm, tn_out)
        hm = _outproj(out_am, out_bm, ga, gb, w_o_b, l, hm, tmeta, tn_out)

        h = _ffn(h, g_ffn, w_gate_b, w_up_b, w_down_b, l, tm_ffn, tf)
        hm = _ffn(hm, g_ffn, w_gate_b, w_up_b, w_down_b, l, tmeta, tf)

    return h.reshape(batch, seq, d)
```

```python
import functools

import jax
import jax.numpy as jnp
from jax import lax
from jax.experimental import pallas as pl
from jax.experimental.pallas import tpu as pltpu

F32 = jnp.float32
BF16 = jnp.bfloat16

N_META = 16
BLOCK = 128
HEAD_DIM = 64
LANES = 128
SWA_HEADS = 16
SWA_KV_HEADS = 4
SB_HEADS = 16
EPS = 1e-6
NEG = -1e30
SCALE = HEAD_DIM ** -0.5
LOG2E = 1.4426950408889634
SB_QSCALE = SCALE * LOG2E
F32_UNDERFLOW_LOG2 = float("inf")
VMEM_LIMIT = 56 * 1024 * 1024


def _cparams(sem):
    return pltpu.CompilerParams(dimension_semantics=sem, vmem_limit_bytes=VMEM_LIMIT)


def _dot(a, b):
    return jnp.dot(a, b, preferred_element_type=F32)


def _dot_nt(a, b):
    return lax.dot_general(a, b, (((1,), (1,)), ((), ())), preferred_element_type=F32)


def _iota(shape, dim):
    return lax.broadcasted_iota(jnp.int32, shape, dim)


def _norm_mm_kernel(x_ref, g_ref, w_ref, o_ref, ob_ref, xn_ref):
    def column_tile(xn):
        y = _dot(xn, w_ref[...])
        o_ref[...] = y
        ob_ref[...] = y.astype(BF16)

    @pl.when(pl.program_id(1) == 0)
    def _():
        x = x_ref[...]
        ms = jnp.mean(x * x, axis=-1, keepdims=True)
        xn = ((x * lax.rsqrt(ms + EPS)) * g_ref[...]).astype(BF16)
        xn_ref[...] = xn
        column_tile(xn)

    @pl.when(pl.program_id(1) > 0)
    def _():
        column_tile(xn_ref[...])


def _norm_matmul(x, g, w, layer, tm, tn):
    m, d = x.shape
    n = w.shape[2]
    return pl.pallas_call(
        _norm_mm_kernel,
        grid=(m // tm, n // tn),
        in_specs=[
            pl.BlockSpec((tm, d), lambda i, j: (i, 0)),
            pl.BlockSpec((1, d), lambda i, j: (0, 0)),
            pl.BlockSpec((None, d, tn), lambda i, j: (layer, 0, j)),
        ],
        out_specs=[pl.BlockSpec((tm, tn), lambda i, j: (i, j)),
                   pl.BlockSpec((tm, tn), lambda i, j: (i, jnp.maximum(j - 1, 0)))],
        out_shape=[jax.ShapeDtypeStruct((m, n), F32), jax.ShapeDtypeStruct((m, n - tn), BF16)],
        scratch_shapes=[pltpu.VMEM((tm, d), BF16)],
        compiler_params=_cparams(("arbitrary", "arbitrary")),
        name="norm_inproj",
    )(x, g, w)


def _outproj_kernel(a_ref, b_ref, ga_ref, gb_ref, wa_ref, wb_ref, h_ref, o_ref):
    def normed(src, g):
        x = src[...]
        ms = jnp.mean(x * x, axis=-1, keepdims=True)
        return ((x * lax.rsqrt(ms + EPS)) * g[...]).astype(BF16)

    o_ref[...] = h_ref[...] + (_dot(normed(a_ref, ga_ref), wa_ref[...])
                               + _dot(normed(b_ref, gb_ref), wb_ref[...]))


def _outproj(a, b, ga, gb, w, layer, h, tm, tn):
    m, wdt = a.shape
    d = h.shape[1]
    assert tn == d
    return pl.pallas_call(
        _outproj_kernel,
        grid=(m // tm, d // tn),
        in_specs=[
            pl.BlockSpec((tm, wdt), lambda i, j: (i, 0)),
            pl.BlockSpec((tm, wdt), lambda i, j: (i, 0)),
            pl.BlockSpec((1, wdt), lambda i, j: (0, 0)),
            pl.BlockSpec((1, wdt), lambda i, j: (0, 0)),
            pl.BlockSpec((None, wdt, tn), lambda i, j: (layer, 0, j)),
            pl.BlockSpec((None, wdt, tn), lambda i, j: (layer, 1, j)),
            pl.BlockSpec((tm, tn), lambda i, j: (i, j)),
        ],
        out_specs=pl.BlockSpec((tm, tn), lambda i, j: (i, j)),
        out_shape=jax.ShapeDtypeStruct((m, d), F32),
        compiler_params=_cparams(("arbitrary", "arbitrary")),
        name="outproj",
    )(a, b, ga, gb, w, w, h)


def _ffn_kernel(x_ref, g_ref, wg_ref, wu_ref, wd_ref, o_ref, xn_ref):
    def hidden_tile(xn):
        gate = _dot(xn, wg_ref[...])
        up = _dot(xn, wu_ref[...])
        act = (gate / (1.0 + jnp.exp(-gate))) * up
        return _dot(act.astype(BF16), wd_ref[...])

    @pl.when(pl.program_id(1) == 0)
    def _():
        x = x_ref[...]
        ms = jnp.mean(x * x, axis=-1, keepdims=True)
        xn = ((x * lax.rsqrt(ms + EPS)) * g_ref[...]).astype(BF16)
        xn_ref[...] = xn
        o_ref[...] = x + hidden_tile(xn)

    @pl.when(pl.program_id(1) > 0)
    def _():
        o_ref[...] += hidden_tile(xn_ref[...])


def _ffn(x, g, wg, wu, wd, layer, tm, tf):
    m, d = x.shape
    f = wg.shape[2]
    return pl.pallas_call(
        _ffn_kernel,
        grid=(m // tm, f // tf),
        in_specs=[
            pl.BlockSpec((tm, d), lambda i, j: (i, 0)),
            pl.BlockSpec((1, d), lambda i, j: (0, 0)),
            pl.BlockSpec((None, d, tf), lambda i, j: (layer, 0, j)),
            pl.BlockSpec((None, d, tf), lambda i, j: (layer, 0, j)),
            pl.BlockSpec((None, tf, d), lambda i, j: (layer, j, 0)),
        ],
        out_specs=pl.BlockSpec((tm, d), lambda i, j: (i, 0)),
        out_shape=jax.ShapeDtypeStruct((m, d), F32),
        scratch_shapes=[pltpu.VMEM((tm, d), BF16)],
        compiler_params=_cparams(("arbitrary", "arbitrary")),
        name="ffn",
    )(x, g, wg, wu, wd)


def _head_rmsnorm(x, g):
    r, c = x.shape
    same_head = (_iota((LANES, LANES), 0) // HEAD_DIM) == (_iota((LANES, LANES), 1) // HEAD_DIM)
    bd = jnp.where(same_head, 1.0 / HEAD_DIM, 0.0).astype(BF16)
    outs = []
    for t in range(c // LANES):
        xb = x[:, t * LANES:(t + 1) * LANES]
        ms = _dot((xb * xb).astype(BF16), bd)
        outs.append((xb * lax.rsqrt(ms + EPS)) * g[:, t * LANES:(t + 1) * LANES])
    return outs


def _half_select(half):
    src = _iota((LANES, LANES), 0)
    dst = _iota((LANES, LANES), 1)
    return jnp.where(src == (dst % HEAD_DIM) + HEAD_DIM * half, 1.0, 0.0).astype(BF16)


def _lane_halves(x):
    lo_half = _iota(x.shape, 1) < HEAD_DIM
    zero = jnp.zeros_like(x)
    return jnp.where(lo_half, x, zero), jnp.where(lo_half, zero, x)


def _merge_halves(lo, hi):
    return jnp.where(_iota(lo.shape, 1) < HEAD_DIM, lo, hi)


def _alibi_slope(h):
    return 2.0 ** (-8.0 * (h + 1) / SWA_HEADS)


def _softplus2(z2):
    neg_abs = lax.bitcast_convert_type(
        lax.bitcast_convert_type(z2, jnp.uint32) | jnp.uint32(0x80000000), F32)
    return jnp.maximum(z2, 0.0) + jnp.log2(1.0 + jnp.exp2(neg_abs))


def _suffix_matrix(n):
    return jnp.where(_iota((n, n), 0) > _iota((n, n), 1), 1.0, 0.0).astype(BF16)


def _sb_block(qe, kblk, vblk, suffix, carry, mask):
    z2 = _dot_nt(qe, kblk)
    sp = _softplus2(z2)
    if mask is not None:
        sp = jnp.where(mask, sp, 0.0)
    later = _dot(sp.astype(BF16), suffix)
    a = jnp.exp2((z2 - sp) - (later + carry))
    if mask is not None:
        a = jnp.where(mask, a, 0.0)
    contrib = _dot(a.astype(BF16), vblk)
    return contrib, carry + jnp.sum(sp, axis=-1, keepdims=True)


def _swa_head(qe, kdup, vdup, bias, mask, sink):
    s = jnp.where(mask, _dot_nt(qe, kdup) - bias, NEG)
    mx = jnp.maximum(jnp.max(s, axis=-1, keepdims=True), sink)
    e = jnp.exp(s - mx)
    den = jnp.sum(e, axis=-1, keepdims=True) + jnp.exp(sink - mx)
    return _dot(e.astype(BF16), vdup) / den


def _pad_rows(x, rows):
    return jnp.concatenate([x, jnp.zeros((rows - x.shape[0], x.shape[1]), x.dtype)], axis=0)


def _dup_kv(x_bf, kv_head):
    t = kv_head // 2
    return _dot(x_bf[:, t * LANES:(t + 1) * LANES], _half_select(kv_head % 2)).astype(BF16)


def _swa_tables():
    group = SWA_HEADS // SWA_KV_HEADS
    slope = jnp.asarray([_alibi_slope(h) * LOG2E for h in range(SWA_HEADS)], F32)
    slope = jnp.repeat(slope.reshape(SWA_KV_HEADS, group, 1, 1), BLOCK, axis=2)
    qi = jnp.arange(BLOCK, dtype=F32)[:, None]
    j = jnp.arange(BLOCK, dtype=F32)[None, :]
    dist_band = jnp.where(j <= qi, qi - j, qi - j + BLOCK)
    dist_meta = jnp.where(j < N_META, N_META + qi - j, 0.0)
    shape = (SWA_KV_HEADS, group * BLOCK, BLOCK)
    return ((slope * dist_band).reshape(shape), (slope * dist_meta).reshape(shape),
            jnp.broadcast_to(slope * BLOCK, slope.shape[:3] + (BLOCK,)).reshape(shape))


def _swa_kernel(q_ref, kc_ref, vc_ref, km_ref, vm_ref, qg_ref, kg_ref,
                band_ref, meta_ref, step_ref, sink_ref, o_ref, kd_ref, vd_ref, *, nsub):
    n = pl.program_id(1)
    group = SWA_HEADS // SWA_KV_HEADS
    rows = group * BLOCK

    def store_rows(k_rows, v_rows, start):
        kn = jnp.concatenate(_head_rmsnorm(k_rows, kg_ref[...]), axis=1).astype(BF16)
        vb = v_rows.astype(BF16)
        for g in range(SWA_KV_HEADS):
            kd_ref[g, start:start + BLOCK] = _dup_kv(kn, g)
            vd_ref[g, start:start + BLOCK] = _dup_kv(vb, g)

    @pl.when(n == 0)
    def _():
        store_rows(_pad_rows(km_ref[...], BLOCK), _pad_rows(vm_ref[...], BLOCK), 0)
        kd_ref[:, 2 * BLOCK:3 * BLOCK] = jnp.zeros((SWA_KV_HEADS, BLOCK, LANES), BF16)
        vd_ref[:, 2 * BLOCK:3 * BLOCK] = jnp.zeros((SWA_KV_HEADS, BLOCK, LANES), BF16)

    qi = _iota((rows, BLOCK), 0) % BLOCK
    j = _iota((rows, BLOCK), 1)
    in_cur = j <= qi
    meta_ok = j < N_META

    def one_block(sub):
        blk = nsub * n + sub
        rs = slice(sub * BLOCK, (sub + 1) * BLOCK)
        kd_ref[:, BLOCK:2 * BLOCK] = kd_ref[:, 2 * BLOCK:3 * BLOCK]
        vd_ref[:, BLOCK:2 * BLOCK] = vd_ref[:, 2 * BLOCK:3 * BLOCK]
        store_rows(kc_ref[rs, :], vc_ref[rs, :], 2 * BLOCK)
        qn = [x.astype(BF16) for x in _head_rmsnorm(q_ref[rs, :], qg_ref[...])]
        band_ok = in_cur | (blk > 0)
        blocks_before = blk.astype(F32)
        logits, weights, dens = {}, {}, {}

        def score(g):
            logits[g] = _dot_nt(
                jnp.concatenate(_lane_halves(qn[2 * g]) + _lane_halves(qn[2 * g + 1]), axis=0), kd_ref[g])

        def softmax(g):
            sink = sink_ref[g]
            s = logits.pop(g)
            s_band = jnp.where(in_cur, s[:, 2 * BLOCK:], s[:, BLOCK:2 * BLOCK]) - band_ref[g]
            s_band = jnp.where(band_ok, s_band, NEG)
            s_meta = jnp.where(meta_ok, (s[:, :BLOCK] - meta_ref[g]) - step_ref[g] * blocks_before, NEG)
            mx = jnp.maximum(jnp.max(jnp.maximum(s_band, s_meta), axis=-1, keepdims=True), sink)
            e_band = jnp.exp2(s_band - mx)
            e_meta = jnp.exp2(s_meta - mx)
            dens[g] = jnp.sum(e_band + e_meta, axis=-1, keepdims=True) + jnp.exp2(sink - mx)
            weights[g] = jnp.concatenate(
                [e_meta, jnp.where(in_cur, 0.0, e_band), jnp.where(in_cur, e_band, 0.0)], axis=1).astype(BF16)

        def attend(g):
            o = _dot(weights.pop(g), vd_ref[g]) / dens.pop(g)
            for t in range(2):
                o_ref[rs, (2 * g + t) * LANES:(2 * g + t + 1) * LANES] = _merge_halves(
                    o[2 * t * BLOCK:(2 * t + 1) * BLOCK], o[(2 * t + 1) * BLOCK:(2 * t + 2) * BLOCK])

        for stage, g in ((score, 0), (score, 1), (softmax, 0), (score, 2), (softmax, 1), (attend, 0),
                         (score, 3), (softmax, 2), (attend, 1), (softmax, 3), (attend, 2), (attend, 3)):
            stage(g)

    for sub in range(nsub):
        one_block(sub)


def _swa(proj, proj_meta, sinks, qg, kg, tables, batch, nsub):
    m = proj.shape[0]
    tb = nsub * BLOCK
    nb = m // batch // tb
    qw = SWA_HEADS * HEAD_DIM
    kvw = SWA_KV_HEADS * HEAD_DIM
    group = SWA_HEADS // SWA_KV_HEADS
    kcol, vcol = qw // kvw, qw // kvw + 1
    row = lambda b, n: b * nb + n
    whole = lambda shape: pl.BlockSpec(shape, lambda b, n: (0,) * len(shape))
    sink_col = jnp.broadcast_to((sinks * LOG2E).reshape(SWA_KV_HEADS, group, 1, 1),
                                (SWA_KV_HEADS, group, BLOCK, LANES))
    sink_col = sink_col.reshape(SWA_KV_HEADS, group * BLOCK, LANES)
    band, meta, step = tables
    return pl.pallas_call(
        functools.partial(_swa_kernel, nsub=nsub),
        grid=(batch, nb),
        in_specs=[
            pl.BlockSpec((tb, qw), lambda b, n: (row(b, n), 0)),
            pl.BlockSpec((tb, kvw), lambda b, n: (row(b, n), kcol)),
            pl.BlockSpec((tb, kvw), lambda b, n: (row(b, n), vcol)),
            pl.BlockSpec((N_META, kvw), lambda b, n: (0, kcol)),
            pl.BlockSpec((N_META, kvw), lambda b, n: (0, vcol)),
            pl.BlockSpec((1, qw), lambda b, n: (0, 0)),
            pl.BlockSpec((1, kvw), lambda b, n: (0, 0)),
            whole(band.shape), whole(meta.shape), whole(step.shape), whole(sink_col.shape),
        ],
        out_specs=pl.BlockSpec((tb, qw), lambda b, n: (row(b, n), 0)),
        out_shape=jax.ShapeDtypeStruct((m, qw), F32),
        scratch_shapes=[pltpu.VMEM((SWA_KV_HEADS, 3 * BLOCK, LANES), BF16),
                        pltpu.VMEM((SWA_KV_HEADS, 3 * BLOCK, LANES), BF16)],
        compiler_params=_cparams(("arbitrary", "arbitrary")),
        name="swa",
    )(proj, proj, proj, proj_meta, proj_meta, qg, kg, band, meta, step, sink_col)


def _sb_kernel(*refs, tq, hp, n_q):
    q_refs = refs[:n_q]
    kbf_ref, vbf_ref, km_ref, vm_ref, o_ref, kmp_ref, vmp_ref, acc_ref, car_ref, more_ref = refs[n_q:]
    per_q = hp // n_q
    i = pl.program_id(2)

    @pl.when(i == 0)
    def _():
        kmp_ref[...] = jnp.zeros_like(kmp_ref)
        vmp_ref[...] = jnp.zeros_like(vmp_ref)
        kmp_ref[0:N_META, :] = km_ref[...].astype(BF16)
        vmp_ref[0:N_META, :] = vm_ref[...].astype(BF16)

    lanes = lambda p: slice(p * LANES, (p + 1) * LANES)
    rows = 2 * tq

    def queries():
        return [jnp.concatenate(_lane_halves(
                    (q_refs[p // per_q][:, lanes(p % per_q)] * SB_QSCALE).astype(BF16)), axis=0)
                for p in range(hp)]

    def publish(p, acc):
        acc_ref[p] = acc
        o_ref[:, lanes(p)] = _merge_halves(acc[0:tq], acc[tq:rows])

    def note_carries(carries):
        m = functools.reduce(jnp.minimum, carries)
        while m.shape[0] > 8:
            half = m.shape[0] // 2
            m = jnp.minimum(m[:half], m[half:])
        more_ref[0] = (jnp.min(m) <= F32_UNDERFLOW_LOG2).astype(jnp.int32)

    def block_step(kv_of_pair, width, mask, first):
        qs = queries()
        kv = [kv_of_pair(p) for p in range(hp)]
        z2 = [_dot_nt(qs[p], kv[p][0]) for p in range(hp)]
        sp = [_softplus2(z) for z in z2]
        if mask is not None:
            sp = [jnp.where(mask, s, 0.0) for s in sp]
        zs = [z2[p] - sp[p] for p in range(hp)]
        later = _dot(jnp.concatenate([s.astype(BF16) for s in sp], axis=0), _suffix_matrix(width))
        carries = []
        for p in range(hp):
            carry = 0.0 if first else car_ref[p]
            a = jnp.exp2(zs[p] - (later[p * rows:(p + 1) * rows] + carry))
            if mask is not None:
                a = jnp.where(mask, a, 0.0)
            d = _dot(a.astype(BF16), kv[p][1])
            publish(p, d if first else acc_ref[p] + d)
            carries.append(carry + jnp.sum(sp[p], axis=-1, keepdims=True))
            car_ref[p] = carries[-1]
        note_carries(carries)

    def kv_block(start):
        return lambda p: (kbf_ref[pl.ds(start, tq), lanes(p)], vbf_ref[pl.ds(start, tq), lanes(p)])

    below_diag = lambda: _iota((rows, tq), 1) < (_iota((rows, tq), 0) % tq)
    prev_start = pl.multiple_of(jnp.maximum(i - 1, 0) * tq, tq)

    @pl.when(i == 0)
    def _():
        block_step(kv_block(0), tq, below_diag(), True)

    @pl.when(i > 0)
    def _():
        qs = queries()
        in_cur = below_diag()
        cur_start = pl.multiple_of(i * tq, tq)
        zs, spb = [], []
        for p in range(hp):
            z_cur = _dot_nt(qs[p], kbf_ref[pl.ds(cur_start, tq), lanes(p)])
            z_prev = _dot_nt(qs[p], kbf_ref[pl.ds(prev_start, tq), lanes(p)])
            z2 = jnp.where(in_cur, z_cur, z_prev)
            sp = _softplus2(z2)
            zs.append(z2 - sp)
            spb.append(sp.astype(BF16))
        zero = jnp.zeros((rows, tq), BF16)
        prev_part = lambda x: jnp.where(in_cur, zero, x)
        cur_part = lambda x: jnp.where(in_cur, x, zero)
        suf = _dot(jnp.concatenate(spb, axis=0), _suffix_matrix(tq))
        prev_tot = _dot(jnp.concatenate([prev_part(s) for s in spb], axis=0), jnp.ones((tq, LANES), BF16))
        carries = []
        for p in range(hp):
            sf = suf[p * rows:(p + 1) * rows]
            pt = prev_tot[p * rows:(p + 1) * rows]
            total = sf[:, 0:1] + spb[p][:, 0:1].astype(F32)
            widen = lambda x: jnp.concatenate([x] * (tq // LANES), axis=1)
            later = sf + jnp.where(in_cur, widen(-pt), widen(total - pt))
            a = jnp.exp2(zs[p] - later).astype(BF16)
            v2 = jnp.concatenate([vbf_ref[pl.ds(prev_start, tq), lanes(p)],
                                  vbf_ref[pl.ds(cur_start, tq), lanes(p)]], axis=0)
            publish(p, _dot(jnp.concatenate([prev_part(a), cur_part(a)], axis=1), v2))
            car_ref[p] = total
            carries.append(total)
        note_carries(carries)

    @pl.when(more_ref[0] > 0)
    def _():
        @pl.when(i > 0)
        def _():
            block_step(kv_block(prev_start), tq, below_diag(), False)

        def body(t):
            block_step(kv_block(pl.multiple_of((i - 2 - t) * tq, tq)), tq, None, False)
            return t + 1

        lax.while_loop(lambda t: (t < i - 1) & (more_ref[0] > 0), body, jnp.int32(0))

        @pl.when(more_ref[0] > 0)
        def _():
            block_step(lambda p: (kmp_ref[:, lanes(p)], vmp_ref[:, lanes(p)]), LANES,
                       _iota((rows, LANES), 1) < N_META, False)


def _sb(proj, proj_b, proj_meta_b, batch, tq, hp):
    m = proj.shape[0]
    seq = m // batch
    nq = seq // tq
    w = hp * LANES
    steps = SB_HEADS * HEAD_DIM // w
    q_start = (SWA_HEADS + 2 * SWA_KV_HEADS) * HEAD_DIM
    qw = min(w, 4 * LANES)
    n_q = w // qw
    assert q_start % qw == 0
    q_spec = lambda t: pl.BlockSpec((tq, qw), lambda b, p, i: (b * nq + i, q_start // qw + p * n_q + t))
    return pl.pallas_call(
        functools.partial(_sb_kernel, tq=tq, hp=hp, n_q=n_q),
        grid=(batch, steps, nq),
        in_specs=[q_spec(t) for t in range(n_q)] + [
            pl.BlockSpec((seq, w), lambda b, p, i: (b, steps + p)),
            pl.BlockSpec((seq, w), lambda b, p, i: (b, 2 * steps + p)),
            pl.BlockSpec((N_META, w), lambda b, p, i: (0, steps + p)),
            pl.BlockSpec((N_META, w), lambda b, p, i: (0, 2 * steps + p)),
        ],
        out_specs=pl.BlockSpec((tq, w), lambda b, p, i: (b * nq + i, p)),
        out_shape=jax.ShapeDtypeStruct((m, SB_HEADS * HEAD_DIM), F32),
        scratch_shapes=[pltpu.VMEM((LANES, w), BF16), pltpu.VMEM((LANES, w), BF16),
                        pltpu.VMEM((hp, 2 * tq, LANES), F32), pltpu.VMEM((hp, 2 * tq, 1), F32),
                        pltpu.SMEM((1,), jnp.int32)],
        compiler_params=_cparams(("arbitrary", "arbitrary", "arbitrary")),
        name="stickbreak",
    )(*([proj] * n_q), proj_b, proj_b, proj_meta_b, proj_meta_b)


def _meta_attn_kernel(sink_ref, p_ref, qg_ref, kg_ref, oa_ref, ob_ref):
    qw = SWA_HEADS * HEAD_DIM
    kvw = SWA_KV_HEADS * HEAD_DIM
    group = SWA_HEADS // SWA_KV_HEADS
    x = p_ref[...]
    xk = _pad_rows(x, LANES)
    qi = _iota((N_META, LANES), 0)
    kj = _iota((N_META, LANES), 1)

    qn = [(t * SCALE).astype(BF16) for t in _head_rmsnorm(x[:, 0:qw], qg_ref[...])]
    kn = jnp.concatenate(_head_rmsnorm(xk[:, qw:qw + kvw], kg_ref[...]), axis=1).astype(BF16)
    vb = xk[:, qw + kvw:qw + 2 * kvw].astype(BF16)
    delta_f = (qi - kj).astype(F32)
    causal = kj <= qi
    for pair in range(SWA_HEADS // 2):
        kv = (2 * pair) // group
        kdup, vdup = _dup_kv(kn, kv), _dup_kv(vb, kv)
        outs = []
        for e, qe in enumerate(_lane_halves(qn[pair])):
            h = 2 * pair + e
            outs.append(_swa_head(qe, kdup, vdup, _alibi_slope(h) * delta_f, causal, sink_ref[h]))
        oa_ref[:, pair * LANES:(pair + 1) * LANES] = _merge_halves(outs[0], outs[1])

    base = qw + 2 * kvw
    sbw = SB_HEADS * HEAD_DIM
    strict = kj < qi
    suffix = _suffix_matrix(LANES)
    zero_c = jnp.zeros((N_META, 1), F32)
    for pair in range(SB_HEADS // 2):
        sl = lambda src, part: src[:, base + part * sbw + pair * LANES: base + part * sbw + (pair + 1) * LANES]
        kb, vb2 = sl(xk, 1).astype(BF16), sl(xk, 2).astype(BF16)
        outs = [_sb_block(qe, kb, vb2, suffix, zero_c, strict)[0]
                for qe in _lane_halves((sl(x, 0) * SB_QSCALE).astype(BF16))]
        ob_ref[:, pair * LANES:(pair + 1) * LANES] = _merge_halves(outs[0], outs[1])


def _meta_attn(proj_meta, sinks, qg, kg, batch):
    pw = proj_meta.shape[1]
    qw = SWA_HEADS * HEAD_DIM
    kvw = SWA_KV_HEADS * HEAD_DIM
    sbw = SB_HEADS * HEAD_DIM
    return pl.pallas_call(
        _meta_attn_kernel,
        grid=(batch,),
        in_specs=[
            pl.BlockSpec(memory_space=pltpu.SMEM),
            pl.BlockSpec((N_META, pw), lambda b: (b, 0)),
            pl.BlockSpec((1, qw), lambda b: (0, 0)),
            pl.BlockSpec((1, kvw), lambda b: (0, 0)),
        ],
        out_specs=[pl.BlockSpec((N_META, qw), lambda b: (b, 0)),
                   pl.BlockSpec((N_META, sbw), lambda b: (b, 0))],
        out_shape=[jax.ShapeDtypeStruct((batch * N_META, qw), F32),
                   jax.ShapeDtypeStruct((batch * N_META, sbw), F32)],
        compiler_params=_cparams(("arbitrary",)),
        name="meta_attn",
    )(sinks, proj_meta, qg, kg)


def kernel(x, meta_tokens, attn_norm_g, w_in, q_norm_g, k_norm_g, attn_sinks,
           swa_out_g, sb_out_g, w_o, ffn_norm_g, w_gate, w_up, w_down):
    batch, seq, d = x.shape
    depth = w_in.shape[0]
    assert (batch * seq) % 1024 == 0 and seq % 256 == 0 and meta_tokens.shape[0] == N_META

    tm = 512
    tmeta = N_META
    tm_in = 1024
    tn_in = 1536
    assert tn_in == (SWA_HEADS + 2 * SWA_KV_HEADS) * HEAD_DIM
    tn_out = 2048
    tm_ffn = 1024
    tf = 512
    swa_sub = 4
    tq = 256
    hp = 8

    h = x.reshape(batch * seq, d)
    hm = meta_tokens.astype(x.dtype)

    w_in_b, w_o_b = w_in.astype(BF16), w_o.astype(BF16)
    w_gate_b, w_up_b, w_down_b = w_gate.astype(BF16), w_up.astype(BF16), w_down.astype(BF16)

    swa_tables = _swa_tables()
    for l in range(depth):
        g_attn = attn_norm_g[l].reshape(1, d)
        qg = jnp.tile(q_norm_g[l], SWA_HEADS).reshape(1, -1)
        kg = jnp.tile(k_norm_g[l], SWA_KV_HEADS).reshape(1, -1)
        sinks = attn_sinks[l].astype(F32)
        ga, gb = swa_out_g[l].reshape(1, -1), sb_out_g[l].reshape(1, -1)
        g_ffn = ffn_norm_g[l].reshape(1, d)

        proj, proj_b = _norm_matmul(h, g_attn, w_in_b, l, tm_in, tn_in)
        proj_m, proj_mb = _norm_matmul(hm, g_attn, w_in_b, l, tmeta, tn_in)

        out_a = _swa(proj, proj_m, sinks, qg * (SCALE * LOG2E), kg, swa_tables, batch, swa_sub)
        out_b = _sb(proj, proj_b, proj_mb, batch, tq, hp)
        out_am, out_bm = _meta_attn(proj_m, sinks, qg, kg, 1)

        h = _outproj(out_a, out_b, ga, gb, w_o_b, l, h, tm, tn_out)
        hm = _outproj(out_am, out_bm, ga, gb, w_o_b, l, hm, tmeta, tn_out)

        h = _ffn(h, g_ffn, w_gate_b, w_up_b, w_down_b, l, tm_ffn, tf)
        hm = _ffn(hm, g_ffn, w_gate_b, w_up_b, w_down_b, l, tmeta, tf)

    return h.reshape(batch, seq, d)
```
